```python
import math
import jax
import jax.numpy as jnp
from jax import lax
import numpy as np

D_MODEL = 2048
BATCH = 2
SEQ = 8192
DEPTH = 2

CHUNK = 64
Q_BLOCK = 128
EPS = 1e-6
CONV_WIDTH = 4

ML_HEADS = 4
ML_DV = D_MODEL // ML_HEADS
ML_DK = ML_DV // 2
ML_WIDTH = ML_HEADS * ML_DV

SSM_HEAD_DIM = 64
SSM_WIDTH = D_MODEL
SSM_HEADS = SSM_WIDTH // SSM_HEAD_DIM
SSM_GROUPS = 4
SSM_STATE = 128
SSM_CONV_CH = SSM_WIDTH + 2 * SSM_GROUPS * SSM_STATE

ATT_HEADS = D_MODEL // 128
ATT_HEAD_DIM = 128
ATT_WIDTH = ATT_HEADS * ATT_HEAD_DIM
KV_RANK = 512
IDX_HEADS = 16
IDX_DIM = 64
TOPK_MAX = 256
REL_BUCKETS = 32
REL_MAX_DIST = 128

GDN_QK_HEADS = D_MODEL // 128
GDN_V_HEADS = 2 * GDN_QK_HEADS
GDN_HEAD_DIM = 128
GDN_QK_WIDTH = GDN_QK_HEADS * GDN_HEAD_DIM
GDN_V_WIDTH = GDN_V_HEADS * GDN_HEAD_DIM
GDN_CONV_CH = 2 * GDN_QK_WIDTH + GDN_V_WIDTH

SPLIT_AB = (ML_HEADS * ML_DK, ML_HEADS * ML_DK, ML_WIDTH, ML_WIDTH, ML_WIDTH, ML_HEADS, ML_HEADS,
            SSM_WIDTH, SSM_CONV_CH, SSM_HEADS)
SPLIT_CD = (ATT_WIDTH, KV_RANK, ATT_WIDTH, IDX_HEADS * IDX_DIM, IDX_DIM, IDX_HEADS,
            GDN_CONV_CH, GDN_V_WIDTH, GDN_V_HEADS, GDN_V_HEADS)
IN_AB = sum(SPLIT_AB)
OUT_AB = ML_WIDTH + SSM_WIDTH
IN_CD = sum(SPLIT_CD)
OUT_CD = ATT_WIDTH + GDN_V_WIDTH

kernel_name = 'chunk_causal_hybrid_mlstm_ssd_dsa_gdn'

F32 = jnp.float32


def rms_norm(x, g):
    xf = x.astype(F32)
    y = xf * lax.rsqrt(jnp.mean(xf * xf, axis=-1, keepdims=True) + EPS)
    return (y * g.astype(F32)).astype(x.dtype)


def l2_norm(x):
    return x * lax.rsqrt(jnp.sum(x * x, axis=-1, keepdims=True) + EPS)


def split_cols(h, sizes):
    offs = np.cumsum(sizes)[:-1].tolist()
    return jnp.split(h, offs, axis=-1)


def causal_conv(x, w, b=None):
    k, c = w.shape
    y = lax.conv_general_dilated(x, w[:, None, :].astype(x.dtype), window_strides=(1,),
                                 padding=[(k - 1, 0)], dimension_numbers=('NWC', 'WIO', 'NWC'),
                                 feature_group_count=c)
    if b is not None:
        y = y + b.astype(y.dtype)
    return y


def to_chunks(a):
    b, t = a.shape[:2]
    return jnp.moveaxis(a.reshape(b, t // CHUNK, CHUNK, *a.shape[2:]), 1, 0)


def from_chunks(a):
    nc, b, l = a.shape[:3]
    return jnp.moveaxis(a, 0, 1).reshape(b, nc * l, *a.shape[3:])


def mlstm_scan(q, k, v, i_pre, log_f):
    bsz = q.shape[0]
    causal = jnp.tril(jnp.ones((CHUNK, CHUNK), bool))

    def step(carry, inp):
        c_st, n_st, m_st = carry
        qc, kc, vc, ic, fc = inp
        bcum = jnp.cumsum(fc, axis=1)
        d = bcum[:, :, None, :] - bcum[:, None, :, :] + ic[:, None, :, :]
        d = jnp.where(causal[None, :, :, None], d, -jnp.inf)
        inter = bcum + m_st[:, None, :]
        m_row = jnp.maximum(inter, jnp.max(d, axis=2))
        w_inter = jnp.exp(inter - m_row)
        p = jnp.exp(d - m_row[:, :, None, :]) * jnp.einsum('blhd,bshd->blsh', qc, kc)
        num = (w_inter[..., None] * jnp.einsum('blhd,bhde->blhe', qc, c_st)
               + jnp.einsum('blsh,bshe->blhe', p, vc))
        den = w_inter * jnp.einsum('blhd,bhd->blh', qc, n_st) + jnp.sum(p, axis=2)
        h = num / jnp.maximum(jnp.abs(den), jnp.exp(-m_row))[..., None]
        b_last = bcum[:, -1]
        g = b_last[:, None] - bcum + ic
        m_new = jnp.maximum(b_last + m_st, jnp.max(g, axis=1))
        w_old = jnp.exp(b_last + m_st - m_new)
        w_in = jnp.exp(g - m_new[:, None])
        c_new = w_old[..., None, None] * c_st + jnp.einsum('blh,blhd,blhe->bhde', w_in, kc, vc)
        n_new = w_old[..., None] * n_st + jnp.einsum('blh,blhd->bhd', w_in, kc)
        return (c_new, n_new, m_new), h

    init = (jnp.zeros((bsz, ML_HEADS, ML_DK, ML_DV), F32), jnp.zeros((bsz, ML_HEADS, ML_DK), F32),
            jnp.zeros((bsz, ML_HEADS), F32))
    _, h = lax.scan(step, init, (to_chunks(q), to_chunks(k), to_chunks(v), to_chunks(i_pre), to_chunks(log_f)))
    return from_chunks(h)


def mlstm_branch(q, k, v, o_pre, z, i_pre, f_pre, b_i, b_f, norm_g):
    bsz, t, _ = q.shape
    q = q.astype(F32).reshape(bsz, t, ML_HEADS, ML_DK)
    k = k.astype(F32).reshape(bsz, t, ML_HEADS, ML_DK) * (ML_DK ** -0.5)
    v = v.astype(F32).reshape(bsz, t, ML_HEADS, ML_DV)
    i_gate = i_pre.astype(F32) + b_i.astype(F32)
    log_f = jax.nn.log_sigmoid(f_pre.astype(F32) + b_f.astype(F32))
    h = mlstm_scan(q, k, v, i_gate, log_f)
    h = rms_norm(h, norm_g.reshape(ML_HEADS, ML_DV)).reshape(bsz, t, ML_WIDTH)
    return h * jax.nn.sigmoid(o_pre.astype(F32)) * jax.nn.silu(z.astype(F32))


def ssd_scan(x, dt, a, bm, cm):
    bsz = x.shape[0]
    hg = a.shape[1]
    causal = jnp.tril(jnp.ones((CHUNK, CHUNK), bool))

    def step(s, inp):
        xc, dtc, bc, cc = inp
        acum = jnp.cumsum(dtc * a, axis=1)
        seg = acum[:, :, None] - acum[:, None, :]
        decay = jnp.exp(jnp.where(causal[None, :, :, None, None], seg, -jnp.inf))
        cb = jnp.einsum('blgn,bsgn->blsg', cc, bc)
        xdt = xc * dtc[..., None]
        y = (jnp.einsum('blsg,blsgh,bsghp->blghp', cb, decay, xdt)
             + jnp.einsum('blgn,bghpn->blghp', cc, s) * jnp.exp(acum)[..., None])
        a_last = acum[:, -1]
        s_new = (jnp.exp(a_last)[..., None, None] * s
                 + jnp.einsum('bsgh,bsghp,bsgn->bghpn', jnp.exp(a_last[:, None] - acum), xdt, bc))
        return s_new, y

    s0 = jnp.zeros((bsz, SSM_GROUPS, hg, SSM_HEAD_DIM, SSM_STATE), F32)
    _, y = lax.scan(step, s0, (to_chunks(x), to_chunks(dt), to_chunks(bm), to_chunks(cm)))
    return from_chunks(y)


def mamba2_branch(z, xbc, dt_raw, conv_w, conv_b, dt_bias, a_log, d_skip, norm_g):
    bsz, t, _ = xbc.shape
    hg = SSM_HEADS // SSM_GROUPS
    xbc = jax.nn.silu(causal_conv(xbc, conv_w, conv_b)).astype(F32)
    xs, bm, cm = split_cols(xbc, (SSM_WIDTH, SSM_GROUPS * SSM_STATE, SSM_GROUPS * SSM_STATE))
    xs = xs.reshape(bsz, t, SSM_GROUPS, hg, SSM_HEAD_DIM)
    bm = bm.reshape(bsz, t, SSM_GROUPS, SSM_STATE)
    cm = cm.reshape(bsz, t, SSM_GROUPS, SSM_STATE)
    dt = jax.nn.softplus(dt_raw.astype(F32) + dt_bias.astype(F32)).reshape(bsz, t, SSM_GROUPS, hg)
    a = -jnp.exp(a_log.astype(F32)).reshape(SSM_GROUPS, hg)
    y = ssd_scan(xs, dt, a, bm, cm) + xs * d_skip.astype(F32).reshape(SSM_GROUPS, hg)[..., None]
    y = y.reshape(bsz, t, SSM_GROUPS, hg * SSM_HEAD_DIM)
    y = y * jax.nn.silu(z.astype(F32)).reshape(bsz, t, SSM_GROUPS, hg * SSM_HEAD_DIM)
    y = rms_norm(y, norm_g.reshape(SSM_GROUPS, hg * SSM_HEAD_DIM))
    return y.reshape(bsz, t, SSM_WIDTH)


def t5_bucket(rel):
    nb = REL_BUCKETS // 2
    max_exact = nb // 2
    ret = jnp.where(rel > 0, nb, 0)
    n = jnp.abs(rel)
    nf = jnp.maximum(n, max_exact).astype(F32)
    large = max_exact + (jnp.log(nf / max_exact) / math.log(REL_MAX_DIST / max_exact)
                         * (nb - max_exact)).astype(jnp.int32)
    large = jnp.minimum(large, nb - 1)
    return ret + jnp.where(n < max_exact, n, large)


def dsa_branch(q, ckv, z, iq, ik, iw, kv_norm_g, w_uk, w_uv, rel_bias):
    bsz, t, _ = q.shape
    top_k = min(TOPK_MAX, t // 4)
    q = q.astype(F32).reshape(bsz, t, ATT_HEADS, ATT_HEAD_DIM)
    ckv = rms_norm(ckv.astype(F32), kv_norm_g)
    iq = iq.astype(F32).reshape(bsz, t, IDX_HEADS, IDX_DIM)
    ik = ik.astype(F32)
    iw = iw.astype(F32) * (IDX_HEADS ** -0.5 * IDX_DIM ** -0.5)
    w_uk = w_uk.astype(F32)
    w_uv = w_uv.astype(F32)
    rel_bias = rel_bias.astype(F32)
    key_pos = jnp.arange(t)
    scale = ATT_HEAD_DIM ** -0.5

    def block(start):
        qpos = start + jnp.arange(Q_BLOCK)
        limit = (qpos // CHUNK + 1) * CHUNK
        admissible = key_pos[None, :] < limit[:, None]
        iq_b = lax.dynamic_slice_in_dim(iq, start, Q_BLOCK, axis=1)
        iw_b = lax.dynamic_slice_in_dim(iw, start, Q_BLOCK, axis=1)
        q_b = lax.dynamic_slice_in_dim(q, start, Q_BLOCK, axis=1)
        score = jnp.einsum('bqhd,bsd->bqhs', iq_b, ik)
        index = jnp.einsum('bqh,bqhs->bqs', iw_b, jax.nn.relu(score))
        index = jnp.where(admissible[None], index, -jnp.inf)
        _, sel = lax.top_k(index, top_k)
        valid = sel < limit[None, :, None]
        c_sel = jax.vmap(lambda c, i: c[i])(ckv, sel)
        q_abs = jnp.einsum('bqhd,rhd->bqhr', q_b, w_uk)
        logits = jnp.einsum('bqhr,bqkr->bqhk', q_abs, c_sel) * scale
        bias = rel_bias[t5_bucket(sel - qpos[None, :, None])]
        logits = logits + jnp.moveaxis(bias, -1, 2)
        logits = jnp.where(valid[:, :, None, :], logits, -jnp.inf)
        p = jax.nn.softmax(logits, axis=-1)
        o_lat = jnp.einsum('bqhk,bqkr->bqhr', p, c_sel)
        return jnp.einsum('bqhr,rhd->bqhd', o_lat, w_uv)

    starts = jnp.arange(t // Q_BLOCK) * Q_BLOCK
    o = lax.map(block, starts)
    o = jnp.moveaxis(o, 0, 1).reshape(bsz, t, ATT_WIDTH)
    return o * jax.nn.silu(z.astype(F32))


def gated_delta_scan(q, k, v, beta, g):
    bsz = q.shape[0]
    rep = GDN_V_HEADS // GDN_QK_HEADS
    causal = jnp.tril(jnp.ones((CHUNK, CHUNK), bool))
    strict = jnp.tril(jnp.ones((CHUNK, CHUNK), bool), k=-1)
    eye = jnp.eye(CHUNK, dtype=F32)

    def step(s, inp):
        qc, kc, vc, bc, gr = inp
        qc = jnp.repeat(qc, rep, axis=2)
        kc = jnp.repeat(kc, rep, axis=2)
        gc = jnp.cumsum(gr, axis=1)
        seg = gc[:, :, None] - gc[:, None, :]
        decay = jnp.exp(jnp.where(causal[None, :, :, None], seg, -jnp.inf))
        kk = jnp.einsum('blhd,bshd->blsh', kc, kc)
        a_mat = jnp.where(strict[None, :, :, None], bc[:, :, None, :] * kk * decay, 0.0)
        lhs = eye + jnp.moveaxis(a_mat, 3, 1)
        vb = jnp.moveaxis(vc * bc[..., None], 2, 1)
        kb = jnp.moveaxis(kc * (bc * jnp.exp(gc))[..., None], 2, 1)
        sol = lax.linalg.triangular_solve(lhs, jnp.concatenate([vb, kb], axis=-1),
                                          left_side=True, lower=True, unit_diagonal=True)
        u = sol[..., :GDN_HEAD_DIM] - jnp.einsum('bhlk,bhkv->bhlv', sol[..., GDN_HEAD_DIM:], s)
        qk = jnp.einsum('blhd,bshd->bhls', qc, kc) * jnp.moveaxis(decay, 3, 1)
        o = (jnp.einsum('blhd,bhdv->bhlv', qc, s) * jnp.moveaxis(jnp.exp(gc), 1, 2)[..., None]
             + jnp.einsum('bhls,bhsv->bhlv', qk, u))
        g_last = gc[:, -1]
        s_new = (jnp.exp(g_last)[..., None, None] * s
                 + jnp.einsum('bsh,bshd,bhsv->bhdv', jnp.exp(g_last[:, None] - gc), kc, u))
        return s_new, jnp.moveaxis(o, 1, 2)

    s0 = jnp.zeros((bsz, GDN_V_HEADS, GDN_HEAD_DIM, GDN_HEAD_DIM), F32)
    _, o = lax.scan(step, s0, (to_chunks(q), to_chunks(k), to_chunks(v), to_chunks(beta), to_chunks(g)))
    return from_chunks(o)


def gdn_branch(qkv, z, b_pre, a_pre, conv_w, dt_bias, a_log, norm_g):
    bsz, t, _ = qkv.shape
    qkv = jax.nn.silu(causal_conv(qkv, conv_w)).astype(F32)
    q, k, v = split_cols(qkv, (GDN_QK_WIDTH, GDN_QK_WIDTH, GDN_V_WIDTH))
    q = l2_norm(q.reshape(bsz, t, GDN_QK_HEADS, GDN_HEAD_DIM)) * (GDN_HEAD_DIM ** -0.5)
    k = l2_norm(k.reshape(bsz, t, GDN_QK_HEADS, GDN_HEAD_DIM))
    v = v.reshape(bsz, t, GDN_V_HEADS, GDN_HEAD_DIM)
    beta = jax.nn.sigmoid(b_pre.astype(F32))
    g = -jnp.exp(a_log.astype(F32)) * jax.nn.softplus(a_pre.astype(F32) + dt_bias.astype(F32))
    o = gated_delta_scan(q, k, v, beta, g)
    o = rms_norm(o, norm_g) * jax.nn.silu(z.astype(F32)).reshape(bsz, t, GDN_V_HEADS, GDN_HEAD_DIM)
    return o.reshape(bsz, t, GDN_V_WIDTH)


def layer_ab(x, norm_g, w_in, ml_b_i, ml_b_f, ml_norm, ssm_conv_w, ssm_conv_b, ssm_dt_bias,
             ssm_a_log, ssm_d, ssm_norm, w_out):
    h = rms_norm(x, norm_g)
    proj = jnp.einsum('btd,de->bte', h, w_in)
    q, k, v, o_pre, z_a, i_pre, f_pre, z_b, xbc, dt_raw = split_cols(proj, SPLIT_AB)
    y_a = mlstm_branch(q, k, v, o_pre, z_a, i_pre, f_pre, ml_b_i, ml_b_f, ml_norm)
    y_b = mamba2_branch(z_b, xbc, dt_raw, ssm_conv_w, ssm_conv_b, ssm_dt_bias, ssm_a_log, ssm_d, ssm_norm)
    y = jnp.concatenate([y_a, y_b], axis=-1).astype(x.dtype)
    return x + jnp.einsum('bte,ed->btd', y, w_out)


def layer_cd(x, norm_g, w_in, kv_norm, w_uk, w_uv, gdn_conv_w, gdn_dt_bias, gdn_a_log, gdn_norm,
             w_out, rel_bias):
    h = rms_norm(x, norm_g)
    proj = jnp.einsum('btd,de->bte', h, w_in)
    q_c, ckv, z_c, iq, ik, iw, qkv, z_d, b_pre, a_pre = split_cols(proj, SPLIT_CD)
    y_c = dsa_branch(q_c, ckv, z_c, iq, ik, iw, kv_norm, w_uk, w_uv, rel_bias)
    y_d = gdn_branch(qkv, z_d, b_pre, a_pre, gdn_conv_w, gdn_dt_bias, gdn_a_log, gdn_norm)
    y = jnp.concatenate([y_c, y_d], axis=-1).astype(x.dtype)
    return x + jnp.einsum('bte,ed->btd', y, w_out)


def setup_inputs(seed: int = 0) -> dict:
    key = jax.random.key(seed)
    keys = iter(jax.random.split(key, 40))
    n_ab = (DEPTH + 1) // 2
    n_cd = DEPTH // 2

    def normal(shape, scale):
        return jax.random.normal(next(keys), shape, F32) * scale

    def unif(shape, lo, hi):
        return jax.random.uniform(next(keys), shape, F32, lo, hi)

    def gain(shape):
        return 1.0 + normal(shape, 0.02)

    def dt_bias(shape):
        dt = jnp.exp(unif(shape, math.log(1e-3), math.log(1e-1)))
        return dt + jnp.log(-jnp.expm1(-dt))

    return {
        'x': normal((BATCH, SEQ, D_MODEL), 1.0),
        'rel_bias': normal((REL_BUCKETS, ATT_HEADS), 0.1),
        'ab_norm': gain((n_ab, D_MODEL)),
        'ab_w_in': normal((n_ab, D_MODEL, IN_AB), D_MODEL ** -0.5),
        'ab_ml_b_i': normal((n_ab, ML_HEADS), 0.1),
        'ab_ml_b_f': unif((n_ab, ML_HEADS), 3.0, 6.0),
        'ab_ml_norm': gain((n_ab, ML_WIDTH)),
        'ab_ssm_conv_w': normal((n_ab, CONV_WIDTH, SSM_CONV_CH), CONV_WIDTH ** -0.5),
        'ab_ssm_conv_b': normal((n_ab, SSM_CONV_CH), 0.02),
        'ab_ssm_dt_bias': dt_bias((n_ab, SSM_HEADS)),
        'ab_ssm_a_log': jnp.log(unif((n_ab, SSM_HEADS), 1.0, 16.0)),
        'ab_ssm_d': gain((n_ab, SSM_HEADS)),
        'ab_ssm_norm': gain((n_ab, SSM_WIDTH)),
        'ab_w_out': normal((n_ab, OUT_AB, D_MODEL), OUT_AB ** -0.5),
        'cd_norm': gain((n_cd, D_MODEL)),
        'cd_w_in': normal((n_cd, D_MODEL, IN_CD), D_MODEL ** -0.5),
        'cd_kv_norm': gain((n_cd, KV_RANK)),
        'cd_w_uk': normal((n_cd, KV_RANK, ATT_HEADS, ATT_HEAD_DIM), KV_RANK ** -0.5),
        'cd_w_uv': normal((n_cd, KV_RANK, ATT_HEADS, ATT_HEAD_DIM), KV_RANK ** -0.5),
        'cd_gdn_conv_w': normal((n_cd, CONV_WIDTH, GDN_CONV_CH), CONV_WIDTH ** -0.5),
        'cd_gdn_dt_bias': dt_bias((n_cd, GDN_V_HEADS)),
        'cd_gdn_a_log': jnp.log(unif((n_cd, GDN_V_HEADS), 1.0, 16.0)),
        'cd_gdn_norm': gain((n_cd, GDN_HEAD_DIM)),
        'cd_w_out': normal((n_cd, OUT_CD, D_MODEL), OUT_CD ** -0.5),
        'final_norm': gain((D_MODEL,)),
    }


def reference(x, rel_bias, ab_norm, ab_w_in, ab_ml_b_i, ab_ml_b_f, ab_ml_norm, ab_ssm_conv_w,
              ab_ssm_conv_b, ab_ssm_dt_bias, ab_ssm_a_log, ab_ssm_d, ab_ssm_norm, ab_w_out,
              cd_norm, cd_w_in, cd_kv_norm, cd_w_uk, cd_w_uv, cd_gdn_conv_w, cd_gdn_dt_bias,
              cd_gdn_a_log, cd_gdn_norm, cd_w_out, final_norm):
    for layer in range(DEPTH):
        i = layer // 2
        if layer % 2 == 0:
            x = layer_ab(x, ab_norm[i], ab_w_in[i], ab_ml_b_i[i], ab_ml_b_f[i], ab_ml_norm[i],
                         ab_ssm_conv_w[i], ab_ssm_conv_b[i], ab_ssm_dt_bias[i], ab_ssm_a_log[i],
                         ab_ssm_d[i], ab_ssm_norm[i], ab_w_out[i])
        else:
            x = layer_cd(x, cd_norm[i], cd_w_in[i], cd_kv_norm[i], cd_w_uk[i], cd_w_uv[i],
                         cd_gdn_conv_w[i], cd_gdn_dt_bias[i], cd_gdn_a_log[i], cd_gdn_norm[i],
                         cd_w_out[i], rel_bias)
    return rms_norm(x, final_norm)
```

```python
import functools
import math

import jax
import jax.numpy as jnp
from jax import lax
from jax.experimental import pallas as pl
from jax.experimental.pallas import tpu as pltpu

F32 = jnp.float32
BF16 = jnp.bfloat16
I32 = jnp.int32
HI = lax.Precision.HIGHEST

EPS = 1e-6
CONV_WIDTH = 4
D_MODEL = 2048

ML_HEADS = 4
ML_DV = 512
ML_DK = 256

SSM_GROUPS = 4
SSM_HEADS_PER_GROUP = 8
SSM_HEAD_DIM = 64
SSM_STATE = 128
SSM_GROUP_WIDTH = SSM_HEADS_PER_GROUP * SSM_HEAD_DIM

ATT_HEADS = 16
ATT_HEAD_DIM = 128
KV_RANK = 512
IDX_HEADS = 16
IDX_DIM = 64
TOPK_MAX = 256
DSA_CHUNK = 64
REL_BUCKETS = 32
REL_MAX_DIST = 128

GDN_QK_HEADS = 16
GDN_V_HEADS = 32
GDN_HEAD_DIM = 128
GDN_GROUPS = 4
GDN_CHUNK = 64

VMEM_BYTES_V7X = 64 * 1024 * 1024
LANES = 128

ML_CHUNK = 256
SSD_CHUNK = 128
IDX_Q = 128
IDX_KT = 512
ATT_T = 512
ATT_SUB = 128
INT_MIN = -(2 ** 31)


def _cparams(sem, vmem_mib):
    return pltpu.CompilerParams(dimension_semantics=sem, vmem_limit_bytes=vmem_mib * 1024 * 1024)


def _bdot(a, b):
    return jnp.dot(a.astype(BF16), b.astype(BF16), preferred_element_type=F32)


def _bdot_nt(a, b):
    return lax.dot_general(a.astype(BF16), b.astype(BF16), (((1,), (1,)), ((), ())), preferred_element_type=F32)


def _bdot_tn(a, b):
    return lax.dot_general(a.astype(BF16), b.astype(BF16), (((0,), (0,)), ((), ())), preferred_element_type=F32)


def _fdot(a, b):
    return jnp.dot(a, b, preferred_element_type=F32, precision=HI)


def _softplus(x):
    return jnp.maximum(x, 0.0) + jnp.log1p(jnp.exp(-jnp.abs(x)))


def _sigmoid(x):
    return 1.0 / (1.0 + jnp.exp(-x))


def _silu(x):
    return x * _sigmoid(x)


def _norm_matmul_body(x_ref, g_ref, w_ref, o_ref, h_ref, *, blocked):
    @pl.when(pl.program_id(1) == 0)
    def _():
        x = x_ref[...].astype(F32)
        ms = jnp.mean(x * x, axis=-1, keepdims=True)
        h_ref[...] = (x * lax.rsqrt(ms + EPS) * g_ref[...]).astype(h_ref.dtype)

    acc = jnp.dot(h_ref[...], w_ref[...], preferred_element_type=F32)
    if blocked:
        bw = o_ref.shape[-1]
        for t in range(o_ref.shape[0]):
            o_ref[t] = acc[:, t * bw:(t + 1) * bw].astype(o_ref.dtype)
    else:
        o_ref[...] = acc.astype(o_ref.dtype)


def _norm_matmul(x, x_col_block, k_dim, gain, w, *, out_dtype, blocked_width=None, tm=1024, tn=512, name):
    m = x.shape[0]
    n = w.shape[1]
    tm = min(tm, m)
    assert m % tm == 0 and n % tn == 0 and w.shape[0] == k_dim
    grid = (m // tm, n // tn)
    if blocked_width is None:
        out_shape = jax.ShapeDtypeStruct((m, n), out_dtype)
        out_spec = pl.BlockSpec((tm, tn), lambda i, j: (i, j))
    else:
        nb = tn // blocked_width
        out_shape = jax.ShapeDtypeStruct((n // blocked_width, m, blocked_width), out_dtype)
        out_spec = pl.BlockSpec((nb, tm, blocked_width), lambda i, j: (j, i, 0))
    return pl.pallas_call(
        functools.partial(_norm_matmul_body, blocked=blocked_width is not None),
        grid=grid,
        in_specs=[
            pl.BlockSpec((tm, k_dim), lambda i, j: (i, x_col_block)),
            pl.BlockSpec((1, k_dim), lambda i, j: (0, 0)),
            pl.BlockSpec((k_dim, tn), lambda i, j: (0, j)),
        ],
        out_specs=out_spec,
        out_shape=out_shape,
        scratch_shapes=[pltpu.VMEM((tm, k_dim), BF16)],
        compiler_params=_cparams(("parallel", "arbitrary"), 48),
        name=name,
    )(x, gain.reshape(1, k_dim).astype(F32), w)


def _out_proj_body(*refs, n_y, k_tiles, final_norm):
    y_refs = refs[:n_y]
    w_ref, res_ref, g_ref, o_ref, acc_ref = refs[n_y:]
    k = pl.program_id(1)

    @pl.when(k == 0)
    def _():
        acc_ref[...] = jnp.zeros_like(acc_ref)

    lo = 0
    for y_ref, nk in zip(y_refs, k_tiles):
        @pl.when((k >= lo) & (k < lo + nk))
        def _(y_ref=y_ref):
            acc_ref[...] += jnp.dot(y_ref[...], w_ref[...], preferred_element_type=F32)
        lo += nk

    @pl.when(k == lo - 1)
    def _():
        xn = res_ref[...] + acc_ref[...]
        if final_norm:
            ms = jnp.mean(xn * xn, axis=-1, keepdims=True)
            xn = xn * lax.rsqrt(ms + EPS) * g_ref[...]
        o_ref[...] = xn


def _out_proj(ys, w, res, gain, *, final_norm, tm=512, tk=512, name):
    m, d = res.shape
    tm = min(tm, m)
    k_tiles = [y.shape[1] // tk for y in ys]
    assert sum(k_tiles) * tk == w.shape[0]
    y_specs = []
    lo = 0
    for nk in k_tiles:
        y_specs.append(pl.BlockSpec((tm, tk), lambda i, k, lo=lo, nk=nk: (i, jnp.clip(k - lo, 0, nk - 1))))
        lo += nk
    return pl.pallas_call(
        functools.partial(_out_proj_body, n_y=len(ys), k_tiles=k_tiles, final_norm=final_norm),
        grid=(m // tm, lo),
        in_specs=y_specs + [
            pl.BlockSpec((tk, d), lambda i, k: (k, 0)),
            pl.BlockSpec((tm, d), lambda i, k: (i, 0)),
            pl.BlockSpec((1, d), lambda i, k: (0, 0)),
        ],
        out_specs=pl.BlockSpec((tm, d), lambda i, k: (i, 0)),
        out_shape=jax.ShapeDtypeStruct((m, d), F32),
        scratch_shapes=[pltpu.VMEM((tm, d), F32)],
        compiler_params=_cparams(("parallel", "arbitrary"), 48),
        name=name,
    )(*ys, w, res, gain.reshape(1, d).astype(F32))


def _mlstm_body(bi_ref, bf_ref, q_ref, k_ref, v_ref, op_ref, z_ref, g_ref, gain_ref, y_ref,
                c_scr, n_scr, m_scr):
    L = q_ref.shape[0]
    h = pl.program_id(1)

    @pl.when(pl.program_id(2) == 0)
    def _():
        c_scr[...] = jnp.zeros_like(c_scr)
        n_scr[...] = jnp.zeros_like(n_scr)
        m_scr[...] = jnp.zeros_like(m_scr)

    gates = g_ref[...]
    ic = gates[0:1, :] + bi_ref[h]
    fpre = gates[1:2, :] + bf_ref[h]
    fc = -_softplus(-fpre)

    ri = lax.broadcasted_iota(I32, (L, L), 0)
    ci = lax.broadcasted_iota(I32, (L, L), 1)
    causal = ci <= ri
    eye = ci == ri
    bcum_col = jnp.sum(jnp.where(causal, fc, 0.0), axis=1, keepdims=True)
    fc_col = jnp.sum(jnp.where(eye, fc, 0.0), axis=1, keepdims=True)
    bcum_row = jnp.sum(jnp.where(ri <= ci, fc_col, 0.0), axis=0, keepdims=True)

    m_st = m_scr[...]
    d = jnp.where(causal, bcum_col - bcum_row + ic, -jnp.inf)
    inter = bcum_col + m_st
    m_row = jnp.maximum(inter, jnp.max(d, axis=1, keepdims=True))
    w_inter = jnp.exp(inter - m_row)

    q = q_ref[...]
    k = k_ref[...] * (ML_DK ** -0.5)
    v = v_ref[...]
    qb = q.astype(BF16)
    vb = v.astype(BF16)
    p = jnp.exp(d - m_row) * _bdot_nt(qb, k)
    c_st = c_scr[...]
    n_st = n_scr[...]
    num = w_inter * _bdot(qb, c_st) + _bdot(p, vb)
    den = w_inter * jnp.sum(q * n_st, axis=1, keepdims=True) + jnp.sum(p, axis=1, keepdims=True)
    hout = num / jnp.maximum(jnp.abs(den), jnp.exp(-m_row))

    b_last = jnp.sum(fc, axis=1, keepdims=True)
    g_row = b_last - bcum_row + ic
    m_new = jnp.maximum(b_last + m_st, jnp.max(g_row, axis=1, keepdims=True))
    w_old = jnp.exp(b_last + m_st - m_new)
    w_in_col = jnp.sum(jnp.where(eye, jnp.exp(g_row - m_new), 0.0), axis=1, keepdims=True)
    kw = k * w_in_col
    c_scr[...] = w_old * c_st + _bdot_tn(kw, vb)
    n_scr[...] = w_old * n_st + jnp.sum(kw, axis=0, keepdims=True)
    m_scr[...] = m_new

    hn = hout * lax.rsqrt(jnp.mean(hout * hout, axis=-1, keepdims=True) + EPS) * gain_ref[...]
    y_ref[...] = (hn * _sigmoid(op_ref[...]) * _silu(z_ref[...])).astype(y_ref.dtype)


def _mlstm(proj, gates, b_i, b_f, gain, bsz, t):
    L = min(ML_CHUNK, t)
    nc = t // L
    row = lambda b, h, c: b * nc + c
    return pl.pallas_call(
        _mlstm_body,
        grid=(bsz, ML_HEADS, nc),
        in_specs=[
            pl.BlockSpec(memory_space=pltpu.SMEM),
            pl.BlockSpec(memory_space=pltpu.SMEM),
            pl.BlockSpec((L, ML_DK), lambda b, h, c: (row(b, h, c), h)),
            pl.BlockSpec((L, ML_DK), lambda b, h, c: (row(b, h, c), ML_HEADS + h)),
            pl.BlockSpec((L, ML_DV), lambda b, h, c: (row(b, h, c), ML_HEADS + h)),
            pl.BlockSpec((L, ML_DV), lambda b, h, c: (row(b, h, c), 2 * ML_HEADS + h)),
            pl.BlockSpec((L, ML_DV), lambda b, h, c: (row(b, h, c), 3 * ML_HEADS + h)),
            pl.BlockSpec((None, None, None, 2, L), lambda b, h, c: (b, h, c, 0, 0)),
            pl.BlockSpec((None, 1, ML_DV), lambda b, h, c: (h, 0, 0)),
        ],
        out_specs=pl.BlockSpec((L, ML_DV), lambda b, h, c: (row(b, h, c), h)),
        out_shape=jax.ShapeDtypeStruct((bsz * t, ML_HEADS * ML_DV), BF16),
        scratch_shapes=[pltpu.VMEM((ML_DK, ML_DV), F32), pltpu.VMEM((1, ML_DK), F32), pltpu.VMEM((1, 1), F32)],
        compiler_params=_cparams(("parallel", "parallel", "arbitrary"), 40),
        name="mlstm",
    )(b_i.astype(F32), b_f.astype(F32), proj, proj, proj, proj, proj, gates,
      gain.reshape(ML_HEADS, 1, ML_DV).astype(F32))


def _causal_conv(x, tail_ref, w):
    L = x.shape[0]
    xx = jnp.concatenate([tail_ref[...], x], axis=0)
    y = w[CONV_WIDTH - 1:CONV_WIDTH, :] * x
    for j in range(CONV_WIDTH - 1):
        y = y + w[j:j + 1, :] * pltpu.roll(xx, CONV_WIDTH - 1 - j, axis=0)[8:, :]
    tail_ref[...] = x[L - 8:, :]
    return y


def _head_expand(n_heads, width):
    r = lax.broadcasted_iota(I32, (n_heads, n_heads * width), 0)
    c = lax.broadcasted_iota(I32, (n_heads, n_heads * width), 1)
    return ((c >= r * width) & (c < (r + 1) * width)).astype(F32)


def _ssd_body(x_ref, bm_ref, cm_ref, z_ref, dtc_ref, dtr_ref, wx_ref, wb_ref, wc_ref, bx_ref, bb_ref, bc_ref,
              dtbc_ref, dtbr_ref, alc_ref, alr_ref, dsk_ref, gain_ref, y_ref, st_scr, tx_scr, tb_scr, tc_scr):
    L = x_ref.shape[0]
    nh, hd = SSM_HEADS_PER_GROUP, SSM_HEAD_DIM

    @pl.when(pl.program_id(2) == 0)
    def _():
        st_scr[...] = jnp.zeros_like(st_scr)
        tx_scr[...] = jnp.zeros_like(tx_scr)
        tb_scr[...] = jnp.zeros_like(tb_scr)
        tc_scr[...] = jnp.zeros_like(tc_scr)

    xc = _silu(_causal_conv(x_ref[...], tx_scr, wx_ref[...]) + bx_ref[...])
    bc = _silu(_causal_conv(bm_ref[...], tb_scr, wb_ref[...]) + bb_ref[...])
    cc = _silu(_causal_conv(cm_ref[...], tc_scr, wc_ref[...]) + bc_ref[...])

    ri = lax.broadcasted_iota(I32, (L, L), 0)
    ci = lax.broadcasted_iota(I32, (L, L), 1)
    causal = ci <= ri
    dt_col = _softplus(dtc_ref[...] + dtbc_ref[...])
    acum = _fdot(causal.astype(F32), dt_col * (-jnp.exp(alc_ref[...])))
    dt_row = _softplus(dtr_ref[...] + dtbr_ref[...])
    acum_row = _fdot(dt_row * (-jnp.exp(alr_ref[...])), (ri <= ci).astype(F32))

    expand = _head_expand(nh, hd)
    acum_e = _fdot(acum, expand)
    xdt = xc * _fdot(dt_col, expand)
    xdt_b = xdt.astype(BF16)
    cb = _bdot_nt(cc, bc)

    lane = lax.broadcasted_iota(I32, (L, 2 * hd), 1)
    parts = []
    for hp in range(nh // 2):
        xpair = xdt_b[:, 2 * hd * hp:2 * hd * (hp + 1)]
        acc = None
        for e in range(2):
            hh = 2 * hp + e
            seg = acum[:, hh:hh + 1] - acum_row[hh:hh + 1, :]
            dec = jnp.exp(jnp.where(causal, seg, -jnp.inf))
            xsel = jnp.where((lane >= hd * e) & (lane < hd * (e + 1)), xpair, jnp.zeros_like(xpair))
            t = _bdot(cb * dec, xsel)
            acc = t if acc is None else acc + t
        parts.append(acc)
    y = jnp.concatenate(parts, axis=1)

    st = st_scr[...]
    y = y + _bdot(cc, st) * jnp.exp(acum_e)
    a_last = acum_e[L - 1:L, :]
    st_scr[...] = jnp.exp(a_last) * st + _bdot_tn(bc, xdt * jnp.exp(a_last - acum_e))

    y = y + xc * dsk_ref[...]
    y = y * _silu(z_ref[...])
    y = y * lax.rsqrt(jnp.mean(y * y, axis=-1, keepdims=True) + EPS) * gain_ref[...]
    y_ref[...] = y.astype(y_ref.dtype)


def _ssd(proj, dt_col, dt_row, conv_w, conv_b, dt_bias, a_log, d_skip, gain, bsz, t):
    L = min(SSD_CHUNK, t)
    nc = t // L
    G, nh, gw, ns = SSM_GROUPS, SSM_HEADS_PER_GROUP, SSM_GROUP_WIDTH, SSM_STATE
    row = lambda b, g, c: b * nc + c
    x_blk = 10240 // gw
    z_blk = 8192 // gw
    b_blk = 12288 // ns
    c_blk = 12800 // ns
    wb_blk = G * gw // ns
    conv_b2 = conv_b.reshape(1, -1).astype(F32)
    f = lambda a: a.astype(F32)
    return pl.pallas_call(
        _ssd_body,
        grid=(bsz, G, nc),
        in_specs=[
            pl.BlockSpec((L, gw), lambda b, g, c: (row(b, g, c), x_blk + g)),
            pl.BlockSpec((L, ns), lambda b, g, c: (row(b, g, c), b_blk + g)),
            pl.BlockSpec((L, ns), lambda b, g, c: (row(b, g, c), c_blk + g)),
            pl.BlockSpec((L, gw), lambda b, g, c: (row(b, g, c), z_blk + g)),
            pl.BlockSpec((None, None, L, nh), lambda b, g, c: (b, g, c, 0)),
            pl.BlockSpec((None, None, nh, L), lambda b, g, c: (b, g, 0, c)),
            pl.BlockSpec((CONV_WIDTH, gw), lambda b, g, c: (0, g)),
            pl.BlockSpec((CONV_WIDTH, ns), lambda b, g, c: (0, wb_blk + g)),
            pl.BlockSpec((CONV_WIDTH, ns), lambda b, g, c: (0, wb_blk + G + g)),
            pl.BlockSpec((1, gw), lambda b, g, c: (0, g)),
            pl.BlockSpec((1, ns), lambda b, g, c: (0, wb_blk + g)),
            pl.BlockSpec((1, ns), lambda b, g, c: (0, wb_blk + G + g)),
            pl.BlockSpec((None, 1, nh), lambda b, g, c: (g, 0, 0)),
            pl.BlockSpec((None, nh, 1), lambda b, g, c: (g, 0, 0)),
            pl.BlockSpec((None, 1, nh), lambda b, g, c: (g, 0, 0)),
            pl.BlockSpec((None, nh, 1), lambda b, g, c: (g, 0, 0)),
            pl.BlockSpec((None, 1, gw), lambda b, g, c: (g, 0, 0)),
            pl.BlockSpec((None, 1, gw), lambda b, g, c: (g, 0, 0)),
        ],
        out_specs=pl.BlockSpec((L, gw), lambda b, g, c: (row(b, g, c), g)),
        out_shape=jax.ShapeDtypeStruct((bsz * t, G * gw), BF16),
        scratch_shapes=[pltpu.VMEM((ns, gw), F32), pltpu.VMEM((8, gw), F32),
                        pltpu.VMEM((8, ns), F32), pltpu.VMEM((8, ns), F32)],
        compiler_params=_cparams(("parallel", "parallel", "arbitrary"), 40),
        name="ssd",
    )(proj, proj, proj, proj, dt_col, dt_row, f(conv_w), f(conv_w), f(conv_w), conv_b2, conv_b2, conv_b2,
      f(dt_bias).reshape(G, 1, nh), f(dt_bias).reshape(G, nh, 1), f(a_log).reshape(G, 1, nh),
      f(a_log).reshape(G, nh, 1), jnp.repeat(f(d_skip), SSM_HEAD_DIM).reshape(G, 1, gw), f(gain).reshape(G, 1, gw))


def _layer_ab(x2, bsz, t, norm_g, w_in, ml_b_i, ml_b_f, ml_norm, ssm_conv_w, ssm_conv_b, ssm_dt_bias,
              ssm_a_log, ssm_d, ssm_norm, w_out):
    m = bsz * t
    small = jnp.concatenate([w_in[:, 8192:8200], w_in[:, 13320:13352]], axis=1)
    w_main = jnp.concatenate(
        [w_in[:, :8192], w_in[:, 8200:13320], jnp.pad(small, ((0, 0), (0, 512 - small.shape[1])))], axis=1
    ).astype(BF16)
    proj = _norm_matmul(x2, 0, D_MODEL, norm_g, w_main, out_dtype=F32, name="proj_ab")
    gate_cols = proj[:, 13312:13352]
    ml_gates = gate_cols[:, :8].reshape(bsz, t // min(ML_CHUNK, t), min(ML_CHUNK, t), 2, ML_HEADS)
    ml_gates = jnp.transpose(ml_gates, (0, 4, 1, 3, 2))
    dt_raw = gate_cols[:, 8:40].reshape(bsz, t, SSM_GROUPS, SSM_HEADS_PER_GROUP)
    dt_col = jnp.transpose(dt_raw, (0, 2, 1, 3))
    dt_row = jnp.transpose(dt_raw, (0, 2, 3, 1))
    y_a = _mlstm(proj, ml_gates, ml_b_i, ml_b_f, ml_norm, bsz, t)
    y_b = _ssd(proj, dt_col, dt_row, ssm_conv_w, ssm_conv_b, ssm_dt_bias, ssm_a_log, ssm_d, ssm_norm, bsz, t)
    return _out_proj([y_a, y_b], w_out.astype(BF16), x2, jnp.ones((D_MODEL,), F32), final_norm=False,
                     name="out_ab")


def _indexer_body(iq_ref, iw_ref, ik_ref, mask_ref, key_scr, pos_scr, *, top_k):
    n_kt = key_scr.shape[0]
    i = pl.program_id(1)
    start = i * IDX_Q
    n_tiles = (start + IDX_Q + IDX_KT - 1) // IDX_KT
    iqb = iq_ref[...].astype(BF16)
    iw = iw_ref[...] * (IDX_HEADS ** -0.5 * IDX_DIM ** -0.5)
    qpos = start + lax.broadcasted_iota(I32, (IDX_Q, 1), 0)
    limit = (lax.shift_right_logical(qpos, int(math.log2(DSA_CHUNK))) + 1) * DSA_CHUNK
    zeros_k = jnp.zeros((IDX_DIM, IDX_KT), BF16)
    kiota = lax.broadcasted_iota(I32, (1, IDX_KT), 1)

    def score_tile(tt, carry):
        ikt = ik_ref[tt]
        ik_even = jnp.concatenate([ikt, zeros_k], axis=0)
        ik_odd = jnp.concatenate([zeros_k, ikt], axis=0)
        acc = jnp.zeros((IDX_Q, IDX_KT), F32)
        for p in range(IDX_HEADS // 2):
            xq = iqb[:, 2 * IDX_DIM * p:2 * IDX_DIM * (p + 1)]
            s0 = jnp.dot(xq, ik_even, preferred_element_type=F32)
            s1 = jnp.dot(xq, ik_odd, preferred_element_type=F32)
            acc = acc + iw[:, 2 * p:2 * p + 1] * jnp.maximum(s0, 0.0)
            acc = acc + iw[:, 2 * p + 1:2 * p + 2] * jnp.maximum(s1, 0.0)
        bits = lax.bitcast_convert_type(acc + 0.0, I32)
        key = jnp.where(bits < 0, bits ^ jnp.int32(0x7FFFFFFF), bits)
        key_scr[tt] = jnp.where(tt * IDX_KT + kiota < limit, key, jnp.int32(INT_MIN))
        return carry

    lax.fori_loop(0, n_tiles, score_tile, 0)

    def count(pred):
        def body(tt, acc):
            w = jnp.where(pred(key_scr[tt], tt * IDX_KT + kiota), 1.0, 0.0)
            for s in range(IDX_KT // LANES):
                acc = acc + w[:, s * LANES:(s + 1) * LANES]
            return acc
        acc = lax.fori_loop(0, n_tiles, body, jnp.zeros((IDX_Q, LANES), F32))
        return jnp.sum(acc, axis=1, keepdims=True)

    thr = jnp.full((IDX_Q, 1), INT_MIN, I32)
    for bit in range(31, -1, -1):
        cand = jnp.zeros_like(thr) if bit == 31 else thr + jnp.int32(1 << bit)
        thr = jnp.where(count(lambda k, pos, cand=cand: k >= cand) >= top_k, cand, thr)

    n_gt = count(lambda k, pos: k > thr)
    n_ge = count(lambda k, pos: k >= thr)
    room = top_k - n_gt
    pos_scr[...] = jnp.full((IDX_Q, 1), 2 * n_kt * IDX_KT, I32)

    @pl.when(jnp.max(jnp.where(thr > INT_MIN, n_ge, 0.0)) > top_k)
    def _():
        pcut = jnp.zeros((IDX_Q, 1), I32)
        for bit in range(int(math.log2(n_kt * IDX_KT)), -1, -1):
            cand = pcut + jnp.int32(1 << bit)
            n_tie = count(lambda k, pos, cand=cand: (k == thr) & (pos < cand))
            pcut = jnp.where(n_tie <= room, cand, pcut)
        pos_scr[...] = pcut

    pcut = pos_scr[...]

    def write_tile(tt, carry):
        k = key_scr[tt]
        pos = tt * IDX_KT + kiota
        sel = (k > thr) | ((k == thr) & (pos < pcut))
        sel = sel & (k > INT_MIN)
        mask_ref[tt] = jnp.where(sel, 1, 0).astype(jnp.int8)
        return carry

    lax.fori_loop(0, n_tiles, write_tile, 0)

    def zero_tile(tt, carry):
        mask_ref[tt] = jnp.zeros((IDX_Q, IDX_KT), jnp.int8)
        return carry

    lax.fori_loop(n_tiles, n_kt, zero_tile, 0)


def _indexer(proj, iq_blk, iw, ik_tiles, bsz, t):
    nq = t // IDX_Q
    n_kt = t // IDX_KT
    top_k = min(TOPK_MAX, t // 4)
    return pl.pallas_call(
        functools.partial(_indexer_body, top_k=top_k),
        grid=(bsz, nq),
        in_specs=[
            pl.BlockSpec((IDX_Q, IDX_HEADS * IDX_DIM), lambda b, i: (b * nq + i, iq_blk)),
            pl.BlockSpec((IDX_Q, IDX_HEADS), lambda b, i: (b * nq + i, 0)),
            pl.BlockSpec((None, n_kt, IDX_DIM, IDX_KT), lambda b, i: (b, 0, 0, 0)),
        ],
        out_specs=pl.BlockSpec((None, n_kt, IDX_Q, IDX_KT), lambda b, i: (b * nq + i, 0, 0, 0)),
        out_shape=jax.ShapeDtypeStruct((bsz * nq, n_kt, IDX_Q, IDX_KT), jnp.int8),
        scratch_shapes=[pltpu.VMEM((n_kt, IDX_Q, IDX_KT), I32), pltpu.VMEM((IDX_Q, 1), I32)],
        compiler_params=_cparams(("parallel", "arbitrary"), 40),
        name="dsa_indexer",
    )(proj, iw, ik_tiles)


NEG_BIG = -3.0e38


def _attn_body(q_ref, k_ref, v_ref, mask_ref, z_ref, tb_ref, o_ref, acc_scr, m_scr, l_scr):
    T = ATT_T
    D = ATT_HEAD_DIM
    nsub = T // ATT_SUB
    i = pl.program_id(1)
    j = pl.program_id(2)

    @pl.when(j == 0)
    def _():
        acc_scr[...] = jnp.zeros_like(acc_scr)
        l_scr[...] = jnp.zeros_like(l_scr)
        m_scr[...] = jnp.full_like(m_scr, NEG_BIG)

    def tile_step(near):
        sel = jnp.concatenate([mask_ref[a, 0].astype(F32) for a in range(mask_ref.shape[0])], axis=0)
        mask_bias = (sel - 1.0) * (-NEG_BIG)

        def pair_body(p, carry):
            q2, k2, v2 = q_ref[p], k_ref[p], v_ref[p]
            for e in range(2):
                hh = 2 * p + e
                s = _bdot_nt(q2[:, D * e:D * (e + 1)], k2[:, D * e:D * (e + 1)]) + mask_bias
                if near is not None:
                    tb = tb_ref[hh]
                    zero = jnp.zeros((ATT_SUB, ATT_SUB), F32)
                    rows = []
                    for a in range(nsub):
                        cols = []
                        for bb in range(nsub):
                            rel_blocks = (bb - a) if near == "diag" else (bb - a - nsub)
                            cols.append(tb[0] if rel_blocks == 0 else tb[1] if rel_blocks == -1 else zero)
                        rows.append(jnp.concatenate(cols, axis=1))
                    s = s + jnp.concatenate(rows, axis=0)
                m_old = m_scr[hh]
                m_new = jnp.maximum(m_old, jnp.max(s, axis=1, keepdims=True))
                alpha = jnp.exp(m_old - m_new)
                pr = jnp.exp(s - m_new)
                l_scr[hh] = alpha * l_scr[hh] + jnp.sum(pr, axis=1, keepdims=True)
                acc_scr[p, :, D * e:D * (e + 1)] = (alpha * acc_scr[p, :, D * e:D * (e + 1)]
                                                    + _bdot(pr, v2[:, D * e:D * (e + 1)]))
                m_scr[hh] = m_new
            return carry

        lax.fori_loop(0, ATT_HEADS // 2, pair_body, 0)

    @pl.when(j < i - 1)
    def _():
        tile_step(None)

    @pl.when(j == i - 1)
    def _():
        tile_step("prev")

    @pl.when(j == i)
    def _():
        tile_step("diag")
        for p in range(ATT_HEADS // 2):
            for e in range(2):
                hh = 2 * p + e
                o = acc_scr[p, :, D * e:D * (e + 1)] / l_scr[hh]
                o = o * _silu(z_ref[p, :, D * e:D * (e + 1)].astype(F32))
                o_ref[:, D * hh:D * (hh + 1)] = o.astype(o_ref.dtype)


def _attention(qz, kv, mask, tb, bsz, t):
    T = ATT_T
    nq = t // T
    npair = ATT_HEADS // 2
    pw = 2 * ATT_HEAD_DIM
    nrow = T // IDX_Q
    ncol = T // IDX_KT
    return pl.pallas_call(
        _attn_body,
        grid=(bsz, nq, nq),
        in_specs=[
            pl.BlockSpec((npair, T, pw), lambda b, i, j: (0, b * nq + i, 0)),
            pl.BlockSpec((npair, T, pw), lambda b, i, j: (0, b * nq + jnp.minimum(j, i), 0)),
            pl.BlockSpec((npair, T, pw), lambda b, i, j: (1, b * nq + jnp.minimum(j, i), 0)),
            pl.BlockSpec((nrow, ncol, IDX_Q, IDX_KT), lambda b, i, j: (b * nq + i, jnp.minimum(j, i), 0, 0)),
            pl.BlockSpec((npair, T, pw), lambda b, i, j: (1, b * nq + i, 0)),
            pl.BlockSpec((ATT_HEADS, 2, ATT_SUB, ATT_SUB), lambda b, i, j: (0, 0, 0, 0)),
        ],
        out_specs=pl.BlockSpec((T, ATT_HEADS * ATT_HEAD_DIM), lambda b, i, j: (b * nq + i, 0)),
        out_shape=jax.ShapeDtypeStruct((bsz * t, ATT_HEADS * ATT_HEAD_DIM), BF16),
        scratch_shapes=[pltpu.VMEM((npair, T, pw), F32), pltpu.VMEM((ATT_HEADS, T, 1), F32),
                        pltpu.VMEM((ATT_HEADS, T, 1), F32)],
        compiler_params=_cparams(("parallel", "parallel", "arbitrary"), 56),
        name="dsa_attention",
    )(qz, kv, kv, mask, qz, tb)


def _t5_bucket(rel):
    nb = REL_BUCKETS // 2
    max_exact = nb // 2
    ret = jnp.where(rel > 0, nb, 0)
    n = jnp.abs(rel)
    nf = jnp.maximum(n, max_exact).astype(F32)
    large = max_exact + (jnp.log(nf / max_exact) / math.log(REL_MAX_DIST / max_exact)
                         * (nb - max_exact)).astype(jnp.int32)
    large = jnp.minimum(large, nb - 1)
    return ret + jnp.where(n < max_exact, n, large)


def _rel_bias_blocks(rel_bias):
    r = jnp.arange(ATT_SUB)[:, None]
    c = jnp.arange(ATT_SUB)[None, :]
    rel = jnp.stack([c - r, c - r - ATT_SUB])
    assert ATT_SUB >= REL_MAX_DIST and ATT_SUB % DSA_CHUNK == 0
    rb = rel_bias.astype(F32)
    far = rb[_t5_bucket(jnp.int32(-2 * ATT_SUB))]
    return jnp.transpose(rb[_t5_bucket(rel)] - far, (3, 0, 1, 2))


def _unit_lower_inverse(a):
    L = a.shape[0]
    ri = lax.broadcasted_iota(I32, (L, L), 0)
    ci = lax.broadcasted_iota(I32, (L, L), 1)
    t = jnp.where(ri == ci, 1.0, 0.0) - a
    x = a
    for _ in range(int(math.log2(L)) - 1):
        x = _fdot(x, x)
        t = t + _fdot(t, x)
    return t


def _gdn_body(v_ref, z_ref, q_ref, k_ref, bc_ref, br_ref, ac_ref, ar_ref, wq_ref, wk_ref, wv_ref,
              dtbc_ref, dtbr_ref, alc_ref, alr_ref, gain_ref, y_ref, s_scr, tq_scr, tk_scr, tv_scr):
    L = q_ref.shape[0]
    D = GDN_HEAD_DIM
    nv = GDN_V_HEADS // GDN_GROUPS
    rep = GDN_V_HEADS // GDN_QK_HEADS

    @pl.when(pl.program_id(2) == 0)
    def _():
        s_scr[...] = jnp.zeros_like(s_scr)
        tq_scr[...] = jnp.zeros_like(tq_scr)
        tk_scr[...] = jnp.zeros_like(tk_scr)
        tv_scr[...] = jnp.zeros_like(tv_scr)

    qc = _silu(_causal_conv(q_ref[...], tq_scr, wq_ref[...]))
    kc = _silu(_causal_conv(k_ref[...], tk_scr, wk_ref[...]))
    vc = _silu(_causal_conv(v_ref[...], tv_scr, wv_ref[...]))
    z = z_ref[...]

    ri = lax.broadcasted_iota(I32, (L, L), 0)
    ci = lax.broadcasted_iota(I32, (L, L), 1)
    causal = ci <= ri
    strict = ci < ri
    beta = _sigmoid(bc_ref[...])
    g_col = -jnp.exp(alc_ref[...]) * _softplus(ac_ref[...] + dtbc_ref[...])
    gc_col = _fdot(causal.astype(F32), g_col)
    g_row = -jnp.exp(alr_ref[...]) * _softplus(ar_ref[...] + dtbr_ref[...])
    gc_row = _fdot(g_row, (ri <= ci).astype(F32))

    for kh in range(nv // rep):
        qh = qc[:, D * kh:D * (kh + 1)]
        kh_ = kc[:, D * kh:D * (kh + 1)]
        qh = qh * lax.rsqrt(jnp.sum(qh * qh, axis=-1, keepdims=True) + EPS) * (D ** -0.5)
        kh_ = kh_ * lax.rsqrt(jnp.sum(kh_ * kh_, axis=-1, keepdims=True) + EPS)
        kk = _bdot_nt(kh_, kh_)
        qk0 = _bdot_nt(qh, kh_)
        for e in range(rep):
            vh = rep * kh + e
            gcol = gc_col[:, vh:vh + 1]
            dec = jnp.exp(jnp.where(causal, gcol - gc_row[vh:vh + 1, :], -jnp.inf))
            bcol = beta[:, vh:vh + 1]
            tinv = _unit_lower_inverse(jnp.where(strict, bcol * kk * dec, 0.0))
            eg = jnp.exp(gcol)
            rhs = jnp.concatenate([vc[:, D * vh:D * (vh + 1)] * bcol, kh_ * (bcol * eg)], axis=1)
            sol = _bdot(tinv, rhs)
            s_st = s_scr[vh]
            u = sol[:, :D] - _bdot(sol[:, D:], s_st)
            o = _bdot(qh, s_st) * eg + _bdot(qk0 * dec, u)
            g_last = gcol[L - 1:L, :]
            s_scr[vh] = jnp.exp(g_last) * s_st + _bdot_tn(kh_ * jnp.exp(g_last - gcol), u)
            o = o * lax.rsqrt(jnp.mean(o * o, axis=-1, keepdims=True) + EPS) * gain_ref[...]
            y_ref[:, D * vh:D * (vh + 1)] = (o * _silu(z[:, D * vh:D * (vh + 1)])).astype(y_ref.dtype)


def _gdn(proj, b_col, b_row, a_col, a_row, conv_w, dt_bias, a_log, gain, bsz, t):
    L = GDN_CHUNK
    nc = t // L
    G = GDN_GROUPS
    nv = GDN_V_HEADS // G
    qw = GDN_QK_HEADS // G * GDN_HEAD_DIM
    vw = nv * GDN_HEAD_DIM
    row = lambda b, g, c: b * nc + c
    z_blk, q_blk, k_blk = 4096 // vw, 8192 // qw, 10240 // qw
    f = lambda a: a.astype(F32)
    cw = f(conv_w)
    col_spec = pl.BlockSpec((None, None, L, nv), lambda b, g, c: (b, g, c, 0))
    row_spec = pl.BlockSpec((None, None, None, nv, L), lambda b, g, c: (b, g, c, 0, 0))
    pc_spec = pl.BlockSpec((None, 1, nv), lambda b, g, c: (g, 0, 0))
    pr_spec = pl.BlockSpec((None, nv, 1), lambda b, g, c: (g, 0, 0))
    return pl.pallas_call(
        _gdn_body,
        grid=(bsz, G, nc),
        in_specs=[
            pl.BlockSpec((L, vw), lambda b, g, c: (row(b, g, c), g)),
            pl.BlockSpec((L, vw), lambda b, g, c: (row(b, g, c), z_blk + g)),
            pl.BlockSpec((L, qw), lambda b, g, c: (row(b, g, c), q_blk + g)),
            pl.BlockSpec((L, qw), lambda b, g, c: (row(b, g, c), k_blk + g)),
            col_spec, row_spec, col_spec, row_spec,
            pl.BlockSpec((CONV_WIDTH, qw), lambda b, g, c: (0, g)),
            pl.BlockSpec((CONV_WIDTH, qw), lambda b, g, c: (0, G + g)),
            pl.BlockSpec((CONV_WIDTH, vw), lambda b, g, c: (0, (2 * G * qw) // vw + g)),
            pc_spec, pr_spec, pc_spec, pr_spec,
            pl.BlockSpec((1, GDN_HEAD_DIM), lambda b, g, c: (0, 0)),
        ],
        out_specs=pl.BlockSpec((L, vw), lambda b, g, c: (row(b, g, c), g)),
        out_shape=jax.ShapeDtypeStruct((bsz * t, GDN_V_HEADS * GDN_HEAD_DIM), BF16),
        scratch_shapes=[pltpu.VMEM((nv, GDN_HEAD_DIM, GDN_HEAD_DIM), F32), pltpu.VMEM((8, qw), F32),
                        pltpu.VMEM((8, qw), F32), pltpu.VMEM((8, vw), F32)],
        compiler_params=_cparams(("parallel", "parallel", "arbitrary"), 40),
        name="gdn",
    )(proj, proj, proj, proj, b_col, b_row, a_col, a_row, cw, cw, cw,
      f(dt_bias).reshape(G, 1, nv), f(dt_bias).reshape(G, nv, 1), f(a_log).reshape(G, 1, nv),
      f(a_log).reshape(G, nv, 1), f(gain).reshape(1, GDN_HEAD_DIM))


def _layer_cd(x2, bsz, t, norm_g, w_in, kv_norm, w_uk, w_uv, gdn_conv_w, gdn_dt_bias, gdn_a_log, gdn_norm,
              w_out, rel_bias, final_g, final_norm):
    m = bsz * t
    small = jnp.concatenate([w_in[:, 5632:5712], w_in[:, 18000:18064]], axis=1)
    w_main = jnp.concatenate(
        [w_in[:, 9808:13904], w_in[:, 13904:18000], w_in[:, 5712:9808], w_in[:, 4608:5632], w_in[:, 2048:2560],
         jnp.pad(small, ((0, 0), (0, 512 - small.shape[1])))], axis=1).astype(BF16)
    proj = _norm_matmul(x2, 0, D_MODEL, norm_g, w_main, out_dtype=F32, name="proj_cd")
    w_attn = jnp.concatenate([w_in[:, :2048] * (ATT_HEAD_DIM ** -0.5), w_in[:, 2560:4608]], axis=1).astype(BF16)
    qz = _norm_matmul(x2, 0, D_MODEL, norm_g, w_attn, out_dtype=BF16, blocked_width=2 * ATT_HEAD_DIM,
                      name="proj_cd_attn")
    w_kv = jnp.concatenate([w_uk.reshape(KV_RANK, -1), w_uv.reshape(KV_RANK, -1)], axis=1).astype(BF16)
    kv = _norm_matmul(proj, 13312 // KV_RANK, KV_RANK, kv_norm, w_kv, out_dtype=BF16,
                      blocked_width=2 * ATT_HEAD_DIM, name="dsa_kv")

    gate_cols = proj[:, 13824:13824 + 144]
    n_kt = t // IDX_KT
    ik_tiles = jnp.transpose(gate_cols[:, :IDX_DIM].reshape(bsz, n_kt, IDX_KT, IDX_DIM), (0, 1, 3, 2)).astype(BF16)
    iw = gate_cols[:, IDX_DIM:IDX_DIM + IDX_HEADS]
    mask = _indexer(proj, 12288 // (IDX_HEADS * IDX_DIM), iw, ik_tiles, bsz, t)
    y_c = _attention(qz, kv, mask, _rel_bias_blocks(rel_bias), bsz, t)

    nv = GDN_V_HEADS // GDN_GROUPS
    b_pre = gate_cols[:, 80:112].reshape(bsz, t, GDN_GROUPS, nv)
    a_pre = gate_cols[:, 112:144].reshape(bsz, t, GDN_GROUPS, nv)
    col = lambda a: jnp.transpose(a, (0, 2, 1, 3))
    rowf = lambda a: jnp.transpose(a.reshape(bsz, t // GDN_CHUNK, GDN_CHUNK, GDN_GROUPS, nv), (0, 3, 1, 4, 2))
    y_d = _gdn(proj, col(b_pre), rowf(b_pre), col(a_pre), rowf(a_pre), gdn_conv_w, gdn_dt_bias, gdn_a_log,
               gdn_norm, bsz, t)
    return _out_proj([y_c, y_d], w_out.astype(BF16), x2, final_g, final_norm=final_norm, name="out_cd")


def kernel(x, rel_bias, ab_norm, ab_w_in, ab_ml_b_i, ab_ml_b_f, ab_ml_norm, ab_ssm_conv_w, ab_ssm_conv_b, ab_ssm_dt_bias, ab_ssm_a_log, ab_ssm_d, ab_ssm_norm, ab_w_out, cd_norm, cd_w_in, cd_kv_norm, cd_w_uk, cd_w_uv, cd_gdn_conv_w, cd_gdn_dt_bias, cd_gdn_a_log, cd_gdn_norm, cd_w_out, final_norm):
    bsz, t, d = x.shape
    x2 = x.reshape(bsz * t, d)
    x2 = _layer_ab(x2, bsz, t, ab_norm[0], ab_w_in[0], ab_ml_b_i[0], ab_ml_b_f[0], ab_ml_norm[0],
                   ab_ssm_conv_w[0], ab_ssm_conv_b[0], ab_ssm_dt_bias[0], ab_ssm_a_log[0], ab_ssm_d[0],
                   ab_ssm_norm[0], ab_w_out[0])
    x2 = _layer_cd(x2, bsz, t, cd_norm[0], cd_w_in[0], cd_kv_norm[0], cd_w_uk[0], cd_w_uv[0], cd_gdn_conv_w[0],
                   cd_gdn_dt_bias[0], cd_gdn_a_log[0], cd_gdn_norm[0], cd_w_out[0], rel_bias, final_norm, True)
    return x2.reshape(bsz, t, d)
```

```python
import functools
import math

import jax
import jax.numpy as jnp
from jax import lax
from jax.experimental import pallas as pl
from jax.experimental.pallas import tpu as pltpu

F32 = jnp.float32
BF16 = jnp.bfloat16
I32 = jnp.int32
HI = lax.Precision.HIGHEST

EPS = 1e-6
CONV_WIDTH = 4
D_MODEL = 2048

ML_HEADS = 4
ML_DV = 512
ML_DK = 256

SSM_GROUPS = 4
SSM_HEADS_PER_GROUP = 8
SSM_HEAD_DIM = 64
SSM_STATE = 128
SSM_GROUP_WIDTH = SSM_HEADS_PER_GROUP * SSM_HEAD_DIM

ATT_HEADS = 16
ATT_HEAD_DIM = 128
KV_RANK = 512
IDX_HEADS = 16
IDX_DIM = 64
TOPK_MAX = 256
DSA_CHUNK = 64
REL_BUCKETS = 32
REL_MAX_DIST = 128

GDN_QK_HEADS = 16
GDN_V_HEADS = 32
GDN_HEAD_DIM = 128
GDN_GROUPS = 4
GDN_CHUNK = 64

VMEM_BYTES_V7X = 64 * 1024 * 1024
LANES = 128

ML_CHUNK = 256
SSD_CHUNK = 128
IDX_Q = 128
IDX_KT = 512
ATT_T = 512
ATT_SUB = 128
INT_MIN = -(2 ** 31)


def _cparams(sem, vmem_mib):
    return pltpu.CompilerParams(dimension_semantics=sem, vmem_limit_bytes=vmem_mib * 1024 * 1024)


def _bdot(a, b):
    return jnp.dot(a.astype(BF16), b.astype(BF16), preferred_element_type=F32)


def _bdot_nt(a, b):
    return lax.dot_general(a.astype(BF16), b.astype(BF16), (((1,), (1,)), ((), ())), preferred_element_type=F32)


def _bdot_tn(a, b):
    return lax.dot_general(a.astype(BF16), b.astype(BF16), (((0,), (0,)), ((), ())), preferred_element_type=F32)


def _fdot(a, b):
    return jnp.dot(a, b, preferred_element_type=F32, precision=HI)


def _softplus(x):
    return jnp.maximum(x, 0.0) + jnp.log1p(jnp.exp(-jnp.abs(x)))


def _sigmoid(x):
    return 1.0 / (1.0 + jnp.exp(-x))


def _silu(x):
    return x * _sigmoid(x)


def _norm_matmul_body(x_ref, g_ref, w_ref, o_ref, h_ref, *, blocked):
    @pl.when(pl.program_id(1) == 0)
    def _():
        x = x_ref[...].astype(F32)
        ms = jnp.mean(x * x, axis=-1, keepdims=True)
        h_ref[...] = (x * lax.rsqrt(ms + EPS) * g_ref[...]).astype(h_ref.dtype)

    acc = jnp.dot(h_ref[...], w_ref[...], preferred_element_type=F32)
    if blocked:
        bw = o_ref.shape[-1]
        for t in range(o_ref.shape[0]):
            o_ref[t] = acc[:, t * bw:(t + 1) * bw].astype(o_ref.dtype)
    else:
        o_ref[...] = acc.astype(o_ref.dtype)


def _norm_matmul(x, x_col_block, k_dim, gain, w, *, out_dtype, blocked_width=None, tm=1024, tn=512, name):
    m = x.shape[0]
    n = w.shape[1]
    tm = min(tm, m)
    assert m % tm == 0 and n % tn == 0 and w.shape[0] == k_dim
    grid = (m // tm, n // tn)
    if blocked_width is None:
        out_shape = jax.ShapeDtypeStruct((m, n), out_dtype)
        out_spec = pl.BlockSpec((tm, tn), lambda i, j: (i, j))
    else:
        nb = tn // blocked_width
        out_shape = jax.ShapeDtypeStruct((n // blocked_width, m, blocked_width), out_dtype)
        out_spec = pl.BlockSpec((nb, tm, blocked_width), lambda i, j: (j, i, 0))
    return pl.pallas_call(
        functools.partial(_norm_matmul_body, blocked=blocked_width is not None),
        grid=grid,
        in_specs=[
            pl.BlockSpec((tm, k_dim), lambda i, j: (i, x_col_block)),
            pl.BlockSpec((1, k_dim), lambda i, j: (0, 0)),
            pl.BlockSpec((k_dim, tn), lambda i, j: (0, j)),
        ],
        out_specs=out_spec,
        out_shape=out_shape,
        scratch_shapes=[pltpu.VMEM((tm, k_dim), BF16)],
        compiler_params=_cparams(("parallel", "arbitrary"), 48),
        name=name,
    )(x, gain.reshape(1, k_dim).astype(F32), w)


def _out_proj_body(*refs, n_y, k_tiles, final_norm):
    y_refs = refs[:n_y]
    w_ref, res_ref, g_ref, o_ref, acc_ref = refs[n_y:]
    k = pl.program_id(1)

    @pl.when(k == 0)
    def _():
        acc_ref[...] = jnp.zeros_like(acc_ref)

    lo = 0
    for y_ref, nk in zip(y_refs, k_tiles):
        @pl.when((k >= lo) & (k < lo + nk))
        def _(y_ref=y_ref):
            acc_ref[...] += jnp.dot(y_ref[...], w_ref[...], preferred_element_type=F32)
        lo += nk

    @pl.when(k == lo - 1)
    def _():
        xn = res_ref[...] + acc_ref[...]
        if final_norm:
            ms = jnp.mean(xn * xn, axis=-1, keepdims=True)
            xn = xn * lax.rsqrt(ms + EPS) * g_ref[...]
        o_ref[...] = xn


def _out_proj(ys, w, res, gain, *, final_norm, tm=512, tk=512, name):
    m, d = res.shape
    tm = min(tm, m)
    k_tiles = [y.shape[1] // tk for y in ys]
    assert sum(k_tiles) * tk == w.shape[0]
    y_specs = []
    lo = 0
    for nk in k_tiles:
        y_specs.append(pl.BlockSpec((tm, tk), lambda i, k, lo=lo, nk=nk: (i, jnp.clip(k - lo, 0, nk - 1))))
        lo += nk
    return pl.pallas_call(
        functools.partial(_out_proj_body, n_y=len(ys), k_tiles=k_tiles, final_norm=final_norm),
        grid=(m // tm, lo),
        in_specs=y_specs + [
            pl.BlockSpec((tk, d), lambda i, k: (k, 0)),
            pl.BlockSpec((tm, d), lambda i, k: (i, 0)),
            pl.BlockSpec((1, d), lambda i, k: (0, 0)),
        ],
        out_specs=pl.BlockSpec((tm, d), lambda i, k: (i, 0)),
        out_shape=jax.ShapeDtypeStruct((m, d), F32),
        scratch_shapes=[pltpu.VMEM((tm, d), F32)],
        compiler_params=_cparams(("parallel", "arbitrary"), 48),
        name=name,
    )(*ys, w, res, gain.reshape(1, d).astype(F32))


def _mlstm_body(bi_ref, bf_ref, q_ref, k_ref, v_ref, op_ref, z_ref, g_ref, gain_ref, y_ref,
                c_scr, n_scr, m_scr):
    L = q_ref.shape[0]
    h = pl.program_id(1)

    @pl.when(pl.program_id(2) == 0)
    def _():
        c_scr[...] = jnp.zeros_like(c_scr)
        n_scr[...] = jnp.zeros_like(n_scr)
        m_scr[...] = jnp.zeros_like(m_scr)

    gates = g_ref[...]
    ic = gates[0:1, :] + bi_ref[h]
    fpre = gates[1:2, :] + bf_ref[h]
    fc = -_softplus(-fpre)

    ri = lax.broadcasted_iota(I32, (L, L), 0)
    ci = lax.broadcasted_iota(I32, (L, L), 1)
    causal = ci <= ri
    eye = ci == ri
    bcum_col = jnp.sum(jnp.where(causal, fc, 0.0), axis=1, keepdims=True)
    fc_col = jnp.sum(jnp.where(eye, fc, 0.0), axis=1, keepdims=True)
    bcum_row = jnp.sum(jnp.where(ri <= ci, fc_col, 0.0), axis=0, keepdims=True)

    m_st = m_scr[...]
    d = jnp.where(causal, bcum_col - bcum_row + ic, -jnp.inf)
    inter = bcum_col + m_st
    m_row = jnp.maximum(inter, jnp.max(d, axis=1, keepdims=True))
    w_inter = jnp.exp(inter - m_row)

    q = q_ref[...]
    k = k_ref[...] * (ML_DK ** -0.5)
    v = v_ref[...]
    qb = q.astype(BF16)
    vb = v.astype(BF16)
    p = jnp.exp(d - m_row) * _bdot_nt(qb, k)
    c_st = c_scr[...]
    n_st = n_scr[...]
    num = w_inter * _bdot(qb, c_st) + _bdot(p, vb)
    den = w_inter * jnp.sum(q * n_st, axis=1, keepdims=True) + jnp.sum(p, axis=1, keepdims=True)
    hout = num / jnp.maximum(jnp.abs(den), jnp.exp(-m_row))

    b_last = jnp.sum(fc, axis=1, keepdims=True)
    g_row = b_last - bcum_row + ic
    m_new = jnp.maximum(b_last + m_st, jnp.max(g_row, axis=1, keepdims=True))
    w_old = jnp.exp(b_last + m_st - m_new)
    w_in_col = jnp.sum(jnp.where(eye, jnp.exp(g_row - m_new), 0.0), axis=1, keepdims=True)
    kw = k * w_in_col
    c_scr[...] = w_old * c_st + _bdot_tn(kw, vb)
    n_scr[...] = w_old * n_st + jnp.sum(kw, axis=0, keepdims=True)
    m_scr[...] = m_new

    hn = hout * lax.rsqrt(jnp.mean(hout * hout, axis=-1, keepdims=True) + EPS) * gain_ref[...]
    y_ref[...] = (hn * _sigmoid(op_ref[...]) * _silu(z_ref[...])).astype(y_ref.dtype)


def _mlstm(proj, gates, b_i, b_f, gain, bsz, t):
    L = min(ML_CHUNK, t)
    nc = t // L
    row = lambda b, h, c: b * nc + c
    return pl.pallas_call(
        _mlstm_body,
        grid=(bsz, ML_HEADS, nc),
        in_specs=[
            pl.BlockSpec(memory_space=pltpu.SMEM),
            pl.BlockSpec(memory_space=pltpu.SMEM),
            pl.BlockSpec((L, ML_DK), lambda b, h, c: (row(b, h, c), h)),
            pl.BlockSpec((L, ML_DK), lambda b, h, c: (row(b, h, c), ML_HEADS + h)),
            pl.BlockSpec((L, ML_DV), lambda b, h, c: (row(b, h, c), ML_HEADS + h)),
            pl.BlockSpec((L, ML_DV), lambda b, h, c: (row(b, h, c), 2 * ML_HEADS + h)),
            pl.BlockSpec((L, ML_DV), lambda b, h, c: (row(b, h, c), 3 * ML_HEADS + h)),
            pl.BlockSpec((None, None, None, 2, L), lambda b, h, c: (b, h, c, 0, 0)),
            pl.BlockSpec((None, 1, ML_DV), lambda b, h, c: (h, 0, 0)),
        ],
        out_specs=pl.BlockSpec((L, ML_DV), lambda b, h, c: (row(b, h, c), h)),
        out_shape=jax.ShapeDtypeStruct((bsz * t, ML_HEADS * ML_DV), BF16),
        scratch_shapes=[pltpu.VMEM((ML_DK, ML_DV), F32), pltpu.VMEM((1, ML_DK), F32), pltpu.VMEM((1, 1), F32)],
        compiler_params=_cparams(("parallel", "parallel", "arbitrary"), 40),
        name="mlstm",
    )(b_i.astype(F32), b_f.astype(F32), proj, proj, proj, proj, proj, gates,
      gain.reshape(ML_HEADS, 1, ML_DV).astype(F32))


def _causal_conv(x, tail_ref, w):
    L = x.shape[0]
    xx = jnp.concatenate([tail_ref[...], x], axis=0)
    y = w[CONV_WIDTH - 1:CONV_WIDTH, :] * x
    for j in range(CONV_WIDTH - 1):
        y = y + w[j:j + 1, :] * pltpu.roll(xx, CONV_WIDTH - 1 - j, axis=0)[8:, :]
    tail_ref[...] = x[L - 8:, :]
    return y


def _head_expand(n_heads, width):
    r = lax.broadcasted_iota(I32, (n_heads, n_heads * width), 0)
    c = lax.broadcasted_iota(I32, (n_heads, n_heads * width), 1)
    return ((c >= r * width) & (c < (r + 1) * width)).astype(F32)


def _ssd_body(x_ref, bm_ref, cm_ref, z_ref, dtc_ref, dtr_ref, wx_ref, wb_ref, wc_ref, bx_ref, bb_ref, bc_ref,
              dtbc_ref, dtbr_ref, alc_ref, alr_ref, dsk_ref, gain_ref, y_ref, st_scr, tx_scr, tb_scr, tc_scr):
    L = x_ref.shape[0]
    nh, hd = SSM_HEADS_PER_GROUP, SSM_HEAD_DIM

    @pl.when(pl.program_id(2) == 0)
    def _():
        st_scr[...] = jnp.zeros_like(st_scr)
        tx_scr[...] = jnp.zeros_like(tx_scr)
        tb_scr[...] = jnp.zeros_like(tb_scr)
        tc_scr[...] = jnp.zeros_like(tc_scr)

    xc = _silu(_causal_conv(x_ref[...], tx_scr, wx_ref[...]) + bx_ref[...])
    bc = _silu(_causal_conv(bm_ref[...], tb_scr, wb_ref[...]) + bb_ref[...])
    cc = _silu(_causal_conv(cm_ref[...], tc_scr, wc_ref[...]) + bc_ref[...])

    ri = lax.broadcasted_iota(I32, (L, L), 0)
    ci = lax.broadcasted_iota(I32, (L, L), 1)
    causal = ci <= ri
    dt_col = _softplus(dtc_ref[...] + dtbc_ref[...])
    acum = _fdot(causal.astype(F32), dt_col * (-jnp.exp(alc_ref[...])))
    dt_row = _softplus(dtr_ref[...] + dtbr_ref[...])
    acum_row = _fdot(dt_row * (-jnp.exp(alr_ref[...])), (ri <= ci).astype(F32))

    expand = _head_expand(nh, hd)
    acum_e = _fdot(acum, expand)
    xdt = xc * _fdot(dt_col, expand)
    xdt_b = xdt.astype(BF16)
    cb = _bdot_nt(cc, bc)

    lane = lax.broadcasted_iota(I32, (L, 2 * hd), 1)
    parts = []
    for hp in range(nh // 2):
        xpair = xdt_b[:, 2 * hd * hp:2 * hd * (hp + 1)]
        acc = None
        for e in range(2):
            hh = 2 * hp + e
            seg = acum[:, hh:hh + 1] - acum_row[hh:hh + 1, :]
            dec = jnp.exp(jnp.where(causal, seg, -jnp.inf))
            xsel = jnp.where((lane >= hd * e) & (lane < hd * (e + 1)), xpair, jnp.zeros_like(xpair))
            t = _bdot(cb * dec, xsel)
            acc = t if acc is None else acc + t
        parts.append(acc)
    y = jnp.concatenate(parts, axis=1)

    st = st_scr[...]
    y = y + _bdot(cc, st) * jnp.exp(acum_e)
    a_last = acum_e[L - 1:L, :]
    st_scr[...] = jnp.exp(a_last) * st + _bdot_tn(bc, xdt * jnp.exp(a_last - acum_e))

    y = y + xc * dsk_ref[...]
    y = y * _silu(z_ref[...])
    y = y * lax.rsqrt(jnp.mean(y * y, axis=-1, keepdims=True) + EPS) * gain_ref[...]
    y_ref[...] = y.astype(y_ref.dtype)


def _ssd(proj, dt_col, dt_row, conv_w, conv_b, dt_bias, a_log, d_skip, gain, bsz, t):
    L = min(SSD_CHUNK, t)
    nc = t // L
    G, nh, gw, ns = SSM_GROUPS, SSM_HEADS_PER_GROUP, SSM_GROUP_WIDTH, SSM_STATE
    row = lambda b, g, c: b * nc + c
    x_blk = 10240 // gw
    z_blk = 8192 // gw
    b_blk = 12288 // ns
    c_blk = 12800 // ns
    wb_blk = G * gw // ns
    conv_b2 = conv_b.reshape(1, -1).astype(F32)
    f = lambda a: a.astype(F32)
    return pl.pallas_call(
        _ssd_body,
        grid=(bsz, G, nc),
        in_specs=[
            pl.BlockSpec((L, gw), lambda b, g, c: (row(b, g, c), x_blk + g)),
            pl.BlockSpec((L, ns), lambda b, g, c: (row(b, g, c), b_blk + g)),
            pl.BlockSpec((L, ns), lambda b, g, c: (row(b, g, c), c_blk + g)),
            pl.BlockSpec((L, gw), lambda b, g, c: (row(b, g, c), z_blk + g)),
            pl.BlockSpec((None, None, L, nh), lambda b, g, c: (b, g, c, 0)),
            pl.BlockSpec((None, None, nh, L), lambda b, g, c: (b, g, 0, c)),
            pl.BlockSpec((CONV_WIDTH, gw), lambda b, g, c: (0, g)),
            pl.BlockSpec((CONV_WIDTH, ns), lambda b, g, c: (0, wb_blk + g)),
            pl.BlockSpec((CONV_WIDTH, ns), lambda b, g, c: (0, wb_blk + G + g)),
            pl.BlockSpec((1, gw), lambda b, g, c: (0, g)),
            pl.BlockSpec((1, ns), lambda b, g, c: (0, wb_blk + g)),
            pl.BlockSpec((1, ns), lambda b, g, c: (0, wb_blk + G + g)),
            pl.BlockSpec((None, 1, nh), lambda b, g, c: (g, 0, 0)),
            pl.BlockSpec((None, nh, 1), lambda b, g, c: (g, 0, 0)),
            pl.BlockSpec((None, 1, nh), lambda b, g, c: (g, 0, 0)),
            pl.BlockSpec((None, nh, 1), lambda b, g, c: (g, 0, 0)),
            pl.BlockSpec((None, 1, gw), lambda b, g, c: (g, 0, 0)),
            pl.BlockSpec((None, 1, gw), lambda b, g, c: (g, 0, 0)),
        ],
        out_specs=pl.BlockSpec((L, gw), lambda b, g, c: (row(b, g, c), g)),
        out_shape=jax.ShapeDtypeStruct((bsz * t, G * gw), BF16),
        scratch_shapes=[pltpu.VMEM((ns, gw), F32), pltpu.VMEM((8, gw), F32),
                        pltpu.VMEM((8, ns), F32), pltpu.VMEM((8, ns), F32)],
        compiler_params=_cparams(("parallel", "parallel", "arbitrary"), 40),
        name="ssd",
    )(proj, proj, proj, proj, dt_col, dt_row, f(conv_w), f(conv_w), f(conv_w), conv_b2, conv_b2, conv_b2,
      f(dt_bias).reshape(G, 1, nh), f(dt_bias).reshape(G, nh, 1), f(a_log).reshape(G, 1, nh),
      f(a_log).reshape(G, nh, 1), jnp.repeat(f(d_skip), SSM_HEAD_DIM).reshape(G, 1, gw), f(gain).reshape(G, 1, gw))


def _layer_ab(x2, bsz, t, norm_g, w_in, ml_b_i, ml_b_f, ml_norm, ssm_conv_w, ssm_conv_b, ssm_dt_bias,
              ssm_a_log, ssm_d, ssm_norm, w_out):
    m = bsz * t
    small = jnp.concatenate([w_in[:, 8192:8200], w_in[:, 13320:13352]], axis=1)
    w_main = jnp.concatenate(
        [w_in[:, :8192], w_in[:, 8200:13320], jnp.pad(small, ((0, 0), (0, 512 - small.shape[1])))], axis=1
    ).astype(BF16)
    proj = _norm_matmul(x2, 0, D_MODEL, norm_g, w_main, out_dtype=F32, name="proj_ab")
    gate_cols = proj[:, 13312:13352]
    ml_gates = gate_cols[:, :8].reshape(bsz, t // min(ML_CHUNK, t), min(ML_CHUNK, t), 2, ML_HEADS)
    ml_gates = jnp.transpose(ml_gates, (0, 4, 1, 3, 2))
    dt_raw = gate_cols[:, 8:40].reshape(bsz, t, SSM_GROUPS, SSM_HEADS_PER_GROUP)
    dt_col = jnp.transpose(dt_raw, (0, 2, 1, 3))
    dt_row = jnp.transpose(dt_raw, (0, 2, 3, 1))
    y_a = _mlstm(proj, ml_gates, ml_b_i, ml_b_f, ml_norm, bsz, t)
    y_b = _ssd(proj, dt_col, dt_row, ssm_conv_w, ssm_conv_b, ssm_dt_bias, ssm_a_log, ssm_d, ssm_norm, bsz, t)
    return _out_proj([y_a, y_b], w_out.astype(BF16), x2, jnp.ones((D_MODEL,), F32), final_norm=False,
                     name="out_ab")


def _indexer_body(iq_ref, iw_ref, ik_ref, mask_ref, key_scr, pos_scr, *, top_k):
    n_kt = key_scr.shape[0]
    i = pl.program_id(1)
    start = i * IDX_Q
    n_tiles = (start + IDX_Q + IDX_KT - 1) // IDX_KT
    iqb = iq_ref[...].astype(BF16)
    iw = iw_ref[...] * (IDX_HEADS ** -0.5 * IDX_DIM ** -0.5)
    qpos = start + lax.broadcasted_iota(I32, (IDX_Q, 1), 0)
    limit = (lax.shift_right_logical(qpos, int(math.log2(DSA_CHUNK))) + 1) * DSA_CHUNK
    zeros_k = jnp.zeros((IDX_DIM, IDX_KT), BF16)
    kiota = lax.broadcasted_iota(I32, (1, IDX_KT), 1)

    def score_tile(tt, carry):
        ikt = ik_ref[tt]
        ik_even = jnp.concatenate([ikt, zeros_k], axis=0)
        ik_odd = jnp.concatenate([zeros_k, ikt], axis=0)
        acc = jnp.zeros((IDX_Q, IDX_KT), F32)
        for p in range(IDX_HEADS // 2):
            xq = iqb[:, 2 * IDX_DIM * p:2 * IDX_DIM * (p + 1)]
            s0 = jnp.dot(xq, ik_even, preferred_element_type=F32)
            s1 = jnp.dot(xq, ik_odd, preferred_element_type=F32)
            acc = acc + iw[:, 2 * p:2 * p + 1] * jnp.maximum(s0, 0.0)
            acc = acc + iw[:, 2 * p + 1:2 * p + 2] * jnp.maximum(s1, 0.0)
        bits = lax.bitcast_convert_type(acc + 0.0, I32)
        key = jnp.where(bits < 0, bits ^ jnp.int32(0x7FFFFFFF), bits)
        key_scr[tt] = jnp.where(tt * IDX_KT + kiota < limit, key, jnp.int32(INT_MIN))
        return carry

    lax.fori_loop(0, n_tiles, score_tile, 0)

    def count(pred):
        def body(tt, acc):
            w = jnp.where(pred(key_scr[tt], tt * IDX_KT + kiota), 1.0, 0.0)
            for s in range(IDX_KT // LANES):
                acc = acc + w[:, s * LANES:(s + 1) * LANES]
            return acc
        acc = lax.fori_loop(0, n_tiles, body, jnp.zeros((IDX_Q, LANES), F32))
        return jnp.sum(acc, axis=1, keepdims=True)

    thr = jnp.full((IDX_Q, 1), INT_MIN, I32)
    for bit in range(31, -1, -1):
        cand = jnp.zeros_like(thr) if bit == 31 else thr + jnp.int32(1 << bit)
        thr = jnp.where(count(lambda k, pos, cand=cand: k >= cand) >= top_k, cand, thr)

    n_gt = count(lambda k, pos: k > thr)
    n_ge = count(lambda k, pos: k >= thr)
    room = top_k - n_gt
    pos_scr[...] = jnp.full((IDX_Q, 1), 2 * n_kt * IDX_KT, I32)

    @pl.when(jnp.max(jnp.where(thr > INT_MIN, n_ge, 0.0)) > top_k)
    def _():
        pcut = jnp.zeros((IDX_Q, 1), I32)
        for bit in range(int(math.log2(n_kt * IDX_KT)), -1, -1):
            cand = pcut + jnp.int32(1 << bit)
            n_tie = count(lambda k, pos, cand=cand: (k == thr) & (pos < cand))
            pcut = jnp.where(n_tie <= room, cand, pcut)
        pos_scr[...] = pcut

    pcut = pos_scr[...]

    def write_tile(tt, carry):
        k = key_scr[tt]
        pos = tt * IDX_KT + kiota
        sel = (k > thr) | ((k == thr) & (pos < pcut))
        sel = sel & (k > INT_MIN)
        mask_ref[tt] = jnp.where(sel, 1, 0).astype(jnp.int8)
        return carry

    lax.fori_loop(0, n_tiles, write_tile, 0)

    def zero_tile(tt, carry):
        mask_ref[tt] = jnp.zeros((IDX_Q, IDX_KT), jnp.int8)
        return carry

    lax.fori_loop(n_tiles, n_kt, zero_tile, 0)


def _indexer(proj, iq_blk, iw, ik_tiles, bsz, t):
    nq = t // IDX_Q
    n_kt = t // IDX_KT
    top_k = min(TOPK_MAX, t // 4)
    return pl.pallas_call(
        functools.partial(_indexer_body, top_k=top_k),
        grid=(bsz, nq),
        in_specs=[
            pl.BlockSpec((IDX_Q, IDX_HEADS * IDX_DIM), lambda b, i: (b * nq + i, iq_blk)),
            pl.BlockSpec((IDX_Q, IDX_HEADS), lambda b, i: (b * nq + i, 0)),
            pl.BlockSpec((None, n_kt, IDX_DIM, IDX_KT), lambda b, i: (b, 0, 0, 0)),
        ],
        out_specs=pl.BlockSpec((None, n_kt, IDX_Q, IDX_KT), lambda b, i: (b * nq + i, 0, 0, 0)),
        out_shape=jax.ShapeDtypeStruct((bsz * nq, n_kt, IDX_Q, IDX_KT), jnp.int8),
        scratch_shapes=[pltpu.VMEM((n_kt, IDX_Q, IDX_KT), I32), pltpu.VMEM((IDX_Q, 1), I32)],
        compiler_params=_cparams(("parallel", "arbitrary"), 40),
        name="dsa_indexer",
    )(proj, iw, ik_tiles)


NEG_BIG = -3.0e38


def _attn_body(q_ref, k_ref, v_ref, mask_ref, z_ref, tb_ref, o_ref, acc_scr, m_scr, l_scr):
    T = ATT_T
    D = ATT_HEAD_DIM
    nsub = T // ATT_SUB
    i = pl.program_id(1)
    j = pl.program_id(2)

    @pl.when(j == 0)
    def _():
        acc_scr[...] = jnp.zeros_like(acc_scr)
        l_scr[...] = jnp.zeros_like(l_scr)
        m_scr[...] = jnp.full_like(m_scr, NEG_BIG)

    def tile_step(near):
        sel = jnp.concatenate([mask_ref[a, 0].astype(F32) for a in range(mask_ref.shape[0])], axis=0)
        mask_bias = (sel - 1.0) * (-NEG_BIG)

        def pair_body(p, carry):
            q2, k2, v2 = q_ref[p], k_ref[p], v_ref[p]
            for e in range(2):
                hh = 2 * p + e
                s = _bdot_nt(q2[:, D * e:D * (e + 1)], k2[:, D * e:D * (e + 1)]) + mask_bias
                if near is not None:
                    tb = tb_ref[hh]
                    zero = jnp.zeros((ATT_SUB, ATT_SUB), F32)
                    rows = []
                    for a in range(nsub):
                        cols = []
                        for bb in range(nsub):
                            rel_blocks = (bb - a) if near == "diag" else (bb - a - nsub)
                            cols.append(tb[0] if rel_blocks == 0 else tb[1] if rel_blocks == -1 else zero)
                        rows.append(jnp.concatenate(cols, axis=1))
                    s = s + jnp.concatenate(rows, axis=0)
                m_old = m_scr[hh]
                m_new = jnp.maximum(m_old, jnp.max(s, axis=1, keepdims=True))
                alpha = jnp.exp(m_old - m_new)
                pr = jnp.exp(s - m_new)
                l_scr[hh] = alpha * l_scr[hh] + jnp.sum(pr, axis=1, keepdims=True)
                acc_scr[p, :, D * e:D * (e + 1)] = (alpha * acc_scr[p, :, D * e:D * (e + 1)]
                                                    + _bdot(pr, v2[:, D * e:D * (e + 1)]))
                m_scr[hh] = m_new
            return carry

        lax.fori_loop(0, ATT_HEADS // 2, pair_body, 0)

    @pl.when(j < i - 1)
    def _():
        tile_step(None)

    @pl.when(j == i - 1)
    def _():
        tile_step("prev")

    @pl.when(j == i)
    def _():
        tile_step("diag")
        for p in range(ATT_HEADS // 2):
            for e in range(2):
                hh = 2 * p + e
                o = acc_scr[p, :, D * e:D * (e + 1)] / l_scr[hh]
                o = o * _silu(z_ref[p, :, D * e:D * (e + 1)].astype(F32))
                o_ref[:, D * hh:D * (hh + 1)] = o.astype(o_ref.dtype)


def _attention(qz, kv, mask, tb, bsz, t):
    T = ATT_T
    nq = t // T
    npair = ATT_HEADS // 2
    pw = 2 * ATT_HEAD_DIM
    nrow = T // IDX_Q
    ncol = T // IDX_KT
    return pl.pallas_call(
        _attn_body,
        grid=(bsz, nq, nq),
        in_specs=[
            pl.BlockSpec((npair, T, pw), lambda b, i, j: (0, b * nq + i, 0)),
            pl.BlockSpec((npair, T, pw), lambda b, i, j: (0, b * nq + jnp.minimum(j, i), 0)),
            pl.BlockSpec((npair, T, pw), lambda b, i, j: (1, b * nq + jnp.minimum(j, i), 0)),
            pl.BlockSpec((nrow, ncol, IDX_Q, IDX_KT), lambda b, i, j: (b * nq + i, jnp.minimum(j, i), 0, 0)),
            pl.BlockSpec((npair, T, pw), lambda b, i, j: (1, b * nq + i, 0)),
            pl.BlockSpec((ATT_HEADS, 2, ATT_SUB, ATT_SUB), lambda b, i, j: (0, 0, 0, 0)),
        ],
        out_specs=pl.BlockSpec((T, ATT_HEADS * ATT_HEAD_DIM), lambda b, i, j: (b * nq + i, 0)),
        out_shape=jax.ShapeDtypeStruct((bsz * t, ATT_HEADS * ATT_HEAD_DIM), BF16),
        scratch_shapes=[pltpu.VMEM((npair, T, pw), F32), pltpu.VMEM((ATT_HEADS, T, 1), F32),
                        pltpu.VMEM((ATT_HEADS, T, 1), F32)],
        compiler_params=_cparams(("parallel", "parallel", "arbitrary"), 56),
        name="dsa_attention",
    )(qz, kv, kv, mask, qz, tb)


def _t5_bucket(rel):
    nb = REL_BUCKETS // 2
    max_exact = nb // 2
    ret = jnp.where(rel > 0, nb, 0)
    n = jnp.abs(rel)
    nf = jnp.maximum(n, max_exact).astype(F32)
    large = max_exact + (jnp.log(nf / max_exact) / math.log(REL_MAX_DIST / max_exact)
                         * (nb - max_exact)).astype(jnp.int32)
    large = jnp.minimum(large, nb - 1)
    return ret + jnp.where(n < max_exact, n, large)


def _rel_bias_blocks(rel_bias):
    r = jnp.arange(ATT_SUB)[:, None]
    c = jnp.arange(ATT_SUB)[None, :]
    rel = jnp.stack([c - r, c - r - ATT_SUB])
    assert ATT_SUB >= REL_MAX_DIST and ATT_SUB % DSA_CHUNK == 0
    rb = rel_bias.astype(F32)
    far = rb[_t5_bucket(jnp.int32(-2 * ATT_SUB))]
    return jnp.transpose(rb[_t5_bucket(rel)] - far, (3, 0, 1, 2))


GDN_PREP_ROWS = 256
GDN_SOLVE_LANES = 256


def _conv_with_prev(x, prev, w):
    xx = jnp.concatenate([prev, x], axis=0)
    y = w[CONV_WIDTH - 1:CONV_WIDTH, :] * x
    for j in range(CONV_WIDTH - 1):
        y = y + w[j:j + 1, :] * pltpu.roll(xx, CONV_WIDTH - 1 - j, axis=0)[8:, :]
    return y


def _gdn_prep_body(v_ref, q_ref, k_ref, pv_ref, pq_ref, pk_ref, bc_ref, ac_ref, ar_ref, wq_ref, wk_ref, wv_ref,
                   dtbc_ref, dtbr_ref, alc_ref, alr_ref,
                   qn_ref, kn_ref, vn_ref, a_ref, gcc_ref, gcr_ref, beta_ref):
    L = GDN_CHUNK
    D = GDN_HEAD_DIM
    nv = GDN_V_HEADS // GDN_GROUPS
    rep = GDN_V_HEADS // GDN_QK_HEADS
    seq_start = pl.program_id(2) == 0

    def conv(x_ref, p_ref, w_ref):
        prev = jnp.where(seq_start, 0.0, p_ref[...])
        return _silu(_conv_with_prev(x_ref[...], prev, w_ref[...]))

    qc = conv(q_ref, pq_ref, wq_ref)
    kc = conv(k_ref, pk_ref, wk_ref)
    vn_ref[...] = conv(v_ref, pv_ref, wv_ref).astype(vn_ref.dtype)
    kn = []
    for kh in range(nv // rep):
        qh = qc[:, D * kh:D * (kh + 1)]
        kk_ = kc[:, D * kh:D * (kh + 1)]
        qh = qh * lax.rsqrt(jnp.sum(qh * qh, axis=-1, keepdims=True) + EPS) * (D ** -0.5)
        kk_ = kk_ * lax.rsqrt(jnp.sum(kk_ * kk_, axis=-1, keepdims=True) + EPS)
        qn_ref[:, D * kh:D * (kh + 1)] = qh.astype(qn_ref.dtype)
        kn_ref[:, D * kh:D * (kh + 1)] = kk_.astype(kn_ref.dtype)
        kn.append(kk_.astype(BF16))

    beta = _sigmoid(bc_ref[...])
    beta_ref[...] = beta
    g_col = -jnp.exp(alc_ref[...]) * _softplus(ac_ref[...] + dtbc_ref[...])
    ri = lax.broadcasted_iota(I32, (L, L), 0)
    ci = lax.broadcasted_iota(I32, (L, L), 1)
    causal = ci <= ri
    strict = ci < ri
    for cc in range(a_ref.shape[0]):
        rows = slice(L * cc, L * (cc + 1))
        gc_col = _fdot(causal.astype(F32), g_col[rows, :])
        g_row = -jnp.exp(alr_ref[...]) * _softplus(ar_ref[cc] + dtbr_ref[...])
        gc_row = _fdot(g_row, (ri <= ci).astype(F32))
        gcc_ref[rows, :] = gc_col
        gcr_ref[cc] = gc_row
        for kh in range(nv // rep):
            kk = _bdot_nt(kn[kh][rows, :], kn[kh][rows, :])
            for e in range(rep):
                vh = rep * kh + e
                dec = jnp.exp(jnp.where(causal, gc_col[:, vh:vh + 1] - gc_row[vh:vh + 1, :], -jnp.inf))
                a_ref[cc, vh] = jnp.where(strict, beta[rows, vh:vh + 1] * kk * dec, 0.0)


def _gdn_solve_body(a_ref, t_ref, t_scr):
    L, _, PB = a_ref.shape
    UN = 8
    sub = lax.broadcasted_iota(I32, (L, PB), 0)
    t_scr[...] = jnp.zeros_like(t_scr)
    t_scr[0] = jnp.where(sub == 0, 1.0, 0.0)

    def row_body(i, carry):
        def col_block(jb, acc):
            j0 = pl.multiple_of(jb * UN, UN)
            a_blk = a_ref[i, pl.ds(j0, UN), :]
            for jj in range(UN):
                acc = acc + a_blk[jj:jj + 1, :] * t_scr[j0 + jj]
            return acc

        acc = lax.fori_loop(0, (i + UN - 1) // UN, col_block, jnp.zeros((L, PB), F32))
        t_scr[i] = jnp.where(sub == i, 1.0, 0.0) - acc
        return carry

    lax.fori_loop(1, L, row_body, 0)
    t_ref[...] = t_scr[...].astype(t_ref.dtype)


def _gdn_scan_body(qn_ref, kn_ref, vn_ref, z_ref, t_ref, gcc_ref, gcr_ref, beta_ref, gain_ref, y_ref, s_scr):
    L = qn_ref.shape[0]
    D = GDN_HEAD_DIM
    nv = GDN_V_HEADS // GDN_GROUPS
    rep = GDN_V_HEADS // GDN_QK_HEADS

    @pl.when(pl.program_id(2) == 0)
    def _():
        s_scr[...] = jnp.zeros_like(s_scr)

    ri = lax.broadcasted_iota(I32, (L, L), 0)
    ci = lax.broadcasted_iota(I32, (L, L), 1)
    causal = ci <= ri
    gc_col = gcc_ref[...]
    gc_row = gcr_ref[...]
    beta = beta_ref[...]
    z = z_ref[...]
    for kh in range(nv // rep):
        qh = qn_ref[:, D * kh:D * (kh + 1)]
        kf = kn_ref[:, D * kh:D * (kh + 1)].astype(F32)
        qk0 = _bdot_nt(qh, kf)
        for e in range(rep):
            vh = rep * kh + e
            gcol = gc_col[:, vh:vh + 1]
            dec = jnp.exp(jnp.where(causal, gcol - gc_row[vh:vh + 1, :], -jnp.inf))
            bcol = beta[:, vh:vh + 1]
            eg = jnp.exp(gcol)
            vf = vn_ref[:, D * vh:D * (vh + 1)].astype(F32)
            rhs = jnp.concatenate([vf * bcol, kf * (bcol * eg)], axis=1)
            sol = _bdot(t_ref[vh], rhs)
            s_st = s_scr[vh]
            u = sol[:, :D] - _bdot(sol[:, D:], s_st)
            o = _bdot(qh, s_st) * eg + _bdot(qk0 * dec, u)
            g_last = gcol[L - 1:L, :]
            s_scr[vh] = jnp.exp(g_last) * s_st + _bdot_tn(kf * jnp.exp(g_last - gcol), u)
            o = o * lax.rsqrt(jnp.mean(o * o, axis=-1, keepdims=True) + EPS) * gain_ref[...]
            y_ref[:, D * vh:D * (vh + 1)] = (o * _silu(z[:, D * vh:D * (vh + 1)])).astype(y_ref.dtype)


def _gdn(proj, b_col, a_col, a_row, conv_w, dt_bias, a_log, gain, bsz, t):
    L = GDN_CHUNK
    nc = t // L
    G = GDN_GROUPS
    nv = GDN_V_HEADS // G
    qw = GDN_QK_HEADS // G * GDN_HEAD_DIM
    vw = nv * GDN_HEAD_DIM
    m = bsz * t
    z_blk, q_blk, k_blk = 4096 // vw, 8192 // qw, 10240 // qw
    f = lambda a: a.astype(F32)
    cw = f(conv_w)
    pc_spec = pl.BlockSpec((None, 1, nv), lambda b, g, c: (g, 0, 0))
    pr_spec = pl.BlockSpec((None, nv, 1), lambda b, g, c: (g, 0, 0))
    wq_spec = pl.BlockSpec((CONV_WIDTH, qw), lambda b, g, c: (0, g))
    wk_spec = pl.BlockSpec((CONV_WIDTH, qw), lambda b, g, c: (0, G + g))
    wv_spec = pl.BlockSpec((CONV_WIDTH, vw), lambda b, g, c: (0, (2 * G * qw) // vw + g))

    LB = min(GDN_PREP_ROWS, t)
    ncb = LB // L
    nb = t // LB
    rowb = lambda b, g, c: b * nb + c
    prev8 = lambda b, g, c: jnp.maximum((b * t + c * LB) // 8 - 1, 0)
    qn, kn, vn, a_mat, gcc, gcr, beta = pl.pallas_call(
        _gdn_prep_body,
        grid=(bsz, G, nb),
        in_specs=[
            pl.BlockSpec((LB, vw), lambda b, g, c: (rowb(b, g, c), g)),
            pl.BlockSpec((LB, qw), lambda b, g, c: (rowb(b, g, c), q_blk + g)),
            pl.BlockSpec((LB, qw), lambda b, g, c: (rowb(b, g, c), k_blk + g)),
            pl.BlockSpec((8, vw), lambda b, g, c: (prev8(b, g, c), g)),
            pl.BlockSpec((8, qw), lambda b, g, c: (prev8(b, g, c), q_blk + g)),
            pl.BlockSpec((8, qw), lambda b, g, c: (prev8(b, g, c), k_blk + g)),
            pl.BlockSpec((None, None, LB, nv), lambda b, g, c: (b, g, c, 0)),
            pl.BlockSpec((None, None, LB, nv), lambda b, g, c: (b, g, c, 0)),
            pl.BlockSpec((None, None, ncb, nv, L), lambda b, g, c: (b, g, c, 0, 0)),
            wq_spec, wk_spec, wv_spec, pc_spec, pr_spec, pc_spec, pr_spec,
        ],
        out_specs=[
            pl.BlockSpec((LB, qw), lambda b, g, c: (rowb(b, g, c), g)),
            pl.BlockSpec((LB, qw), lambda b, g, c: (rowb(b, g, c), g)),
            pl.BlockSpec((LB, vw), lambda b, g, c: (rowb(b, g, c), g)),
            pl.BlockSpec((None, None, ncb, nv, L, L), lambda b, g, c: (b, g, c, 0, 0, 0)),
            pl.BlockSpec((None, None, LB, nv), lambda b, g, c: (b, g, c, 0)),
            pl.BlockSpec((None, None, ncb, nv, L), lambda b, g, c: (b, g, c, 0, 0)),
            pl.BlockSpec((None, None, LB, nv), lambda b, g, c: (b, g, c, 0)),
        ],
        out_shape=[
            jax.ShapeDtypeStruct((m, G * qw), BF16),
            jax.ShapeDtypeStruct((m, G * qw), BF16),
            jax.ShapeDtypeStruct((m, G * vw), BF16),
            jax.ShapeDtypeStruct((bsz, G, nc, nv, L, L), F32),
            jax.ShapeDtypeStruct((bsz, G, t, nv), F32),
            jax.ShapeDtypeStruct((bsz, G, nc, nv, L), F32),
            jax.ShapeDtypeStruct((bsz, G, t, nv), F32),
        ],
        compiler_params=_cparams(("parallel", "parallel", "parallel"), 48),
        name="gdn_prep",
    )(proj, proj, proj, proj, proj, proj, b_col, a_col, a_row, cw, cw, cw,
      f(dt_bias).reshape(G, 1, nv), f(dt_bias).reshape(G, nv, 1), f(a_log).reshape(G, 1, nv),
      f(a_log).reshape(G, nv, 1))

    n_prob = bsz * G * nc * nv
    pb = min(GDN_SOLVE_LANES, n_prob)
    a_t = jnp.transpose(a_mat.reshape(n_prob, L, L), (1, 2, 0))
    t_t = pl.pallas_call(
        _gdn_solve_body,
        grid=(n_prob // pb,),
        in_specs=[pl.BlockSpec((L, L, pb), lambda p: (0, 0, p))],
        out_specs=pl.BlockSpec((L, L, pb), lambda p: (0, 0, p)),
        out_shape=jax.ShapeDtypeStruct((L, L, n_prob), BF16),
        scratch_shapes=[pltpu.VMEM((L, L, pb), F32)],
        compiler_params=_cparams(("parallel",), 40),
        name="gdn_solve",
    )(a_t)
    t_mat = jnp.transpose(t_t, (2, 0, 1)).reshape(bsz, G, nc, nv, L, L)

    row = lambda b, g, c: b * nc + c
    return pl.pallas_call(
        _gdn_scan_body,
        grid=(bsz, G, nc),
        in_specs=[
            pl.BlockSpec((L, qw), lambda b, g, c: (row(b, g, c), g)),
            pl.BlockSpec((L, qw), lambda b, g, c: (row(b, g, c), g)),
            pl.BlockSpec((L, vw), lambda b, g, c: (row(b, g, c), g)),
            pl.BlockSpec((L, vw), lambda b, g, c: (row(b, g, c), z_blk + g)),
            pl.BlockSpec((None, None, None, nv, L, L), lambda b, g, c: (b, g, c, 0, 0, 0)),
            pl.BlockSpec((None, None, L, nv), lambda b, g, c: (b, g, c, 0)),
            pl.BlockSpec((None, None, None, nv, L), lambda b, g, c: (b, g, c, 0, 0)),
            pl.BlockSpec((None, None, L, nv), lambda b, g, c: (b, g, c, 0)),
            pl.BlockSpec((1, GDN_HEAD_DIM), lambda b, g, c: (0, 0)),
        ],
        out_specs=pl.BlockSpec((L, vw), lambda b, g, c: (row(b, g, c), g)),
        out_shape=jax.ShapeDtypeStruct((m, GDN_V_HEADS * GDN_HEAD_DIM), BF16),
        scratch_shapes=[pltpu.VMEM((nv, GDN_HEAD_DIM, GDN_HEAD_DIM), F32)],
        compiler_params=_cparams(("parallel", "parallel", "arbitrary"), 40),
        name="gdn_scan",
    )(qn, kn, vn, proj, t_mat, gcc, gcr, beta, f(gain).reshape(1, GDN_HEAD_DIM))


def _layer_cd(x2, bsz, t, norm_g, w_in, kv_norm, w_uk, w_uv, gdn_conv_w, gdn_dt_bias, gdn_a_log, gdn_norm,
              w_out, rel_bias, final_g, final_norm):
    m = bsz * t
    small = jnp.concatenate([w_in[:, 5632:5712], w_in[:, 18000:18064]], axis=1)
    w_main = jnp.concatenate(
        [w_in[:, 9808:13904], w_in[:, 13904:18000], w_in[:, 5712:9808], w_in[:, 4608:5632], w_in[:, 2048:2560],
         jnp.pad(small, ((0, 0), (0, 512 - small.shape[1])))], axis=1).astype(BF16)
    proj = _norm_matmul(x2, 0, D_MODEL, norm_g, w_main, out_dtype=F32, name="proj_cd")
    w_attn = jnp.concatenate([w_in[:, :2048] * (ATT_HEAD_DIM ** -0.5), w_in[:, 2560:4608]], axis=1).astype(BF16)
    qz = _norm_matmul(x2, 0, D_MODEL, norm_g, w_attn, out_dtype=BF16, blocked_width=2 * ATT_HEAD_DIM,
                      name="proj_cd_attn")
    w_kv = jnp.concatenate([w_uk.reshape(KV_RANK, -1), w_uv.reshape(KV_RANK, -1)], axis=1).astype(BF16)
    kv = _norm_matmul(proj, 13312 // KV_RANK, KV_RANK, kv_norm, w_kv, out_dtype=BF16,
                      blocked_width=2 * ATT_HEAD_DIM, name="dsa_kv")

    gate_cols = proj[:, 13824:13824 + 144]
    n_kt = t // IDX_KT
    ik_tiles = jnp.transpose(gate_cols[:, :IDX_DIM].reshape(bsz, n_kt, IDX_KT, IDX_DIM), (0, 1, 3, 2)).astype(BF16)
    iw = gate_cols[:, IDX_DIM:IDX_DIM + IDX_HEADS]
    mask = _indexer(proj, 12288 // (IDX_HEADS * IDX_DIM), iw, ik_tiles, bsz, t)
    y_c = _attention(qz, kv, mask, _rel_bias_blocks(rel_bias), bsz, t)

    nv = GDN_V_HEADS // GDN_GROUPS
    b_pre = gate_cols[:, 80:112].reshape(bsz, t, GDN_GROUPS, nv)
    a_pre = gate_cols[:, 112:144].reshape(bsz, t, GDN_GROUPS, nv)
    col = lambda a: jnp.transpose(a, (0, 2, 1, 3))
    rowf = lambda a: jnp.transpose(a.reshape(bsz, t // GDN_CHUNK, GDN_CHUNK, GDN_GROUPS, nv), (0, 3, 1, 4, 2))
    y_d = _gdn(proj, col(b_pre), col(a_pre), rowf(a_pre), gdn_conv_w, gdn_dt_bias, gdn_a_log, gdn_norm, bsz, t)
    return _out_proj([y_c, y_d], w_out.astype(BF16), x2, final_g, final_norm=final_norm, name="out_cd")


def kernel(x, rel_bias, ab_norm, ab_w_in, ab_ml_b_i, ab_ml_b_f, ab_ml_norm, ab_ssm_conv_w, ab_ssm_conv_b, ab_ssm_dt_bias, ab_ssm_a_log, ab_ssm_d, ab_ssm_norm, ab_w_out, cd_norm, cd_w_in, cd_kv_norm, cd_w_uk, cd_w_uv, cd_gdn_conv_w, cd_gdn_dt_bias, cd_gdn_a_log, cd_gdn_norm, cd_w_out, final_norm):
    bsz, t, d = x.shape
    x2 = x.reshape(bsz * t, d)
    x2 = _layer_ab(x2, bsz, t, ab_norm[0], ab_w_in[0], ab_ml_b_i[0], ab_ml_b_f[0], ab_ml_norm[0],
                   ab_ssm_conv_w[0], ab_ssm_conv_b[0], ab_ssm_dt_bias[0], ab_ssm_a_log[0], ab_ssm_d[0],
                   ab_ssm_norm[0], ab_w_out[0])
    x2 = _layer_cd(x2, bsz, t, cd_norm[0], cd_w_in[0], cd_kv_norm[0], cd_w_uk[0], cd_w_uv[0], cd_gdn_conv_w[0],
                   cd_gdn_dt_bias[0], cd_gdn_a_log[0], cd_gdn_norm[0], cd_w_out[0], rel_bias, final_norm, True)
    return x2.reshape(bsz, t, d)
```

```python
import functools
import math

import jax
import jax.numpy as jnp
from jax import lax
from jax.experimental import pallas as pl
from jax.experimental.pallas import tpu as pltpu

F32 = jnp.float32
BF16 = jnp.bfloat16
I32 = jnp.int32
HI = lax.Precision.HIGHEST

EPS = 1e-6
CONV_WIDTH = 4
D_MODEL = 2048

ML_HEADS = 4
ML_DV = 512
ML_DK = 256

SSM_GROUPS = 4
SSM_HEADS_PER_GROUP = 8
SSM_HEAD_DIM = 64
SSM_STATE = 128
SSM_GROUP_WIDTH = SSM_HEADS_PER_GROUP * SSM_HEAD_DIM

ATT_HEADS = 16
ATT_HEAD_DIM = 128
KV_RANK = 512
IDX_HEADS = 16
IDX_DIM = 64
TOPK_MAX = 256
DSA_CHUNK = 64
REL_BUCKETS = 32
REL_MAX_DIST = 128

GDN_QK_HEADS = 16
GDN_V_HEADS = 32
GDN_HEAD_DIM = 128
GDN_GROUPS = 4
GDN_CHUNK = 64

VMEM_BYTES_V7X = 64 * 1024 * 1024
LANES = 128

ML_CHUNK = 256
SSD_CHUNK = 128
IDX_Q = 128
IDX_KT = 512
ATT_T = 512
ATT_SUB = 128
INT_MIN = -(2 ** 31)


def _cparams(sem, vmem_mib):
    return pltpu.CompilerParams(dimension_semantics=sem, vmem_limit_bytes=vmem_mib * 1024 * 1024)


def _bdot(a, b):
    return jnp.dot(a.astype(BF16), b.astype(BF16), preferred_element_type=F32)


def _bdot_nt(a, b):
    return lax.dot_general(a.astype(BF16), b.astype(BF16), (((1,), (1,)), ((), ())), preferred_element_type=F32)


def _bdot_tn(a, b):
    return lax.dot_general(a.astype(BF16), b.astype(BF16), (((0,), (0,)), ((), ())), preferred_element_type=F32)


def _fdot(a, b):
    return jnp.dot(a, b, preferred_element_type=F32, precision=HI)


def _softplus(x):
    return jnp.maximum(x, 0.0) + jnp.log1p(jnp.exp(-jnp.abs(x)))


def _sigmoid(x):
    return 1.0 / (1.0 + jnp.exp(-x))


def _silu(x):
    return x * _sigmoid(x)


def _norm_matmul_body(x_ref, g_ref, w_ref, o_ref, h_ref, *, blocked):
    @pl.when(pl.program_id(1) == 0)
    def _():
        x = x_ref[...].astype(F32)
        ms = jnp.mean(x * x, axis=-1, keepdims=True)
        h_ref[...] = (x * lax.rsqrt(ms + EPS) * g_ref[...]).astype(h_ref.dtype)

    acc = jnp.dot(h_ref[...], w_ref[...], preferred_element_type=F32)
    if blocked:
        bw = o_ref.shape[-1]
        for t in range(o_ref.shape[0]):
            o_ref[t] = acc[:, t * bw:(t + 1) * bw].astype(o_ref.dtype)
    else:
        o_ref[...] = acc.astype(o_ref.dtype)


def _norm_matmul(x, x_col_block, k_dim, gain, w, *, out_dtype, blocked_width=None, tm=1024, tn=512, name):
    m = x.shape[0]
    n = w.shape[1]
    tm = min(tm, m)
    assert m % tm == 0 and n % tn == 0 and w.shape[0] == k_dim
    grid = (m // tm, n // tn)
    if blocked_width is None:
        out_shape = jax.ShapeDtypeStruct((m, n), out_dtype)
        out_spec = pl.BlockSpec((tm, tn), lambda i, j: (i, j))
    else:
        nb = tn // blocked_width
        out_shape = jax.ShapeDtypeStruct((n // blocked_width, m, blocked_width), out_dtype)
        out_spec = pl.BlockSpec((nb, tm, blocked_width), lambda i, j: (j, i, 0))
    return pl.pallas_call(
        functools.partial(_norm_matmul_body, blocked=blocked_width is not None),
        grid=grid,
        in_specs=[
            pl.BlockSpec((tm, k_dim), lambda i, j: (i, x_col_block)),
            pl.BlockSpec((1, k_dim), lambda i, j: (0, 0)),
            pl.BlockSpec((k_dim, tn), lambda i, j: (0, j)),
        ],
        out_specs=out_spec,
        out_shape=out_shape,
        scratch_shapes=[pltpu.VMEM((tm, k_dim), BF16)],
        compiler_params=_cparams(("parallel", "arbitrary"), 48),
        name=name,
    )(x, gain.reshape(1, k_dim).astype(F32), w)


def _out_proj_body(*refs, n_y, k_tiles, final_norm):
    y_refs = refs[:n_y]
    w_ref, res_ref, g_ref, o_ref, acc_ref = refs[n_y:]
    k = pl.program_id(1)

    @pl.when(k == 0)
    def _():
        acc_ref[...] = jnp.zeros_like(acc_ref)

    lo = 0
    for y_ref, nk in zip(y_refs, k_tiles):
        @pl.when((k >= lo) & (k < lo + nk))
        def _(y_ref=y_ref):
            acc_ref[...] += jnp.dot(y_ref[...], w_ref[...], preferred_element_type=F32)
        lo += nk

    @pl.when(k == lo - 1)
    def _():
        xn = res_ref[...] + acc_ref[...]
        if final_norm:
            ms = jnp.mean(xn * xn, axis=-1, keepdims=True)
            xn = xn * lax.rsqrt(ms + EPS) * g_ref[...]
        o_ref[...] = xn


def _out_proj(ys, w, res, gain, *, final_norm, tm=512, tk=512, name):
    m, d = res.shape
    tm = min(tm, m)
    k_tiles = [y.shape[1] // tk for y in ys]
    assert sum(k_tiles) * tk == w.shape[0]
    y_specs = []
    lo = 0
    for nk in k_tiles:
        y_specs.append(pl.BlockSpec((tm, tk), lambda i, k, lo=lo, nk=nk: (i, jnp.clip(k - lo, 0, nk - 1))))
        lo += nk
    return pl.pallas_call(
        functools.partial(_out_proj_body, n_y=len(ys), k_tiles=k_tiles, final_norm=final_norm),
        grid=(m // tm, lo),
        in_specs=y_specs + [
            pl.BlockSpec((tk, d), lambda i, k: (k, 0)),
            pl.BlockSpec((tm, d), lambda i, k: (i, 0)),
            pl.BlockSpec((1, d), lambda i, k: (0, 0)),
        ],
        out_specs=pl.BlockSpec((tm, d), lambda i, k: (i, 0)),
        out_shape=jax.ShapeDtypeStruct((m, d), F32),
        scratch_shapes=[pltpu.VMEM((tm, d), F32)],
        compiler_params=_cparams(("parallel", "arbitrary"), 48),
        name=name,
    )(*ys, w, res, gain.reshape(1, d).astype(F32))


def _mlstm_body(bi_ref, bf_ref, q_ref, k_ref, v_ref, op_ref, z_ref, g_ref, gain_ref, y_ref,
                c_scr, n_scr, m_scr):
    L = q_ref.shape[0]
    h = pl.program_id(1)

    @pl.when(pl.program_id(2) == 0)
    def _():
        c_scr[...] = jnp.zeros_like(c_scr)
        n_scr[...] = jnp.zeros_like(n_scr)
        m_scr[...] = jnp.zeros_like(m_scr)

    gates = g_ref[...]
    ic = gates[0:1, :] + bi_ref[h]
    fpre = gates[1:2, :] + bf_ref[h]
    fc = -_softplus(-fpre)

    ri = lax.broadcasted_iota(I32, (L, L), 0)
    ci = lax.broadcasted_iota(I32, (L, L), 1)
    causal = ci <= ri
    eye = ci == ri
    bcum_col = jnp.sum(jnp.where(causal, fc, 0.0), axis=1, keepdims=True)
    fc_col = jnp.sum(jnp.where(eye, fc, 0.0), axis=1, keepdims=True)
    bcum_row = jnp.sum(jnp.where(ri <= ci, fc_col, 0.0), axis=0, keepdims=True)

    m_st = m_scr[...]
    d = jnp.where(causal, bcum_col - bcum_row + ic, -jnp.inf)
    inter = bcum_col + m_st
    m_row = jnp.maximum(inter, jnp.max(d, axis=1, keepdims=True))
    w_inter = jnp.exp(inter - m_row)

    q = q_ref[...]
    k = k_ref[...] * (ML_DK ** -0.5)
    v = v_ref[...]
    qb = q.astype(BF16)
    vb = v.astype(BF16)
    p = jnp.exp(d - m_row) * _bdot_nt(qb, k)
    c_st = c_scr[...]
    n_st = n_scr[...]
    num = w_inter * _bdot(qb, c_st) + _bdot(p, vb)
    den = w_inter * jnp.sum(q * n_st, axis=1, keepdims=True) + jnp.sum(p, axis=1, keepdims=True)
    hout = num / jnp.maximum(jnp.abs(den), jnp.exp(-m_row))

    b_last = jnp.sum(fc, axis=1, keepdims=True)
    g_row = b_last - bcum_row + ic
    m_new = jnp.maximum(b_last + m_st, jnp.max(g_row, axis=1, keepdims=True))
    w_old = jnp.exp(b_last + m_st - m_new)
    w_in_col = jnp.sum(jnp.where(eye, jnp.exp(g_row - m_new), 0.0), axis=1, keepdims=True)
    kw = k * w_in_col
    c_scr[...] = w_old * c_st + _bdot_tn(kw, vb)
    n_scr[...] = w_old * n_st + jnp.sum(kw, axis=0, keepdims=True)
    m_scr[...] = m_new

    hn = hout * lax.rsqrt(jnp.mean(hout * hout, axis=-1, keepdims=True) + EPS) * gain_ref[...]
    y_ref[...] = (hn * _sigmoid(op_ref[...]) * _silu(z_ref[...])).astype(y_ref.dtype)


def _mlstm(proj, gates, b_i, b_f, gain, bsz, t):
    L = min(ML_CHUNK, t)
    nc = t // L
    row = lambda b, h, c: b * nc + c
    return pl.pallas_call(
        _mlstm_body,
        grid=(bsz, ML_HEADS, nc),
        in_specs=[
            pl.BlockSpec(memory_space=pltpu.SMEM),
            pl.BlockSpec(memory_space=pltpu.SMEM),
            pl.BlockSpec((L, ML_DK), lambda b, h, c: (row(b, h, c), h)),
            pl.BlockSpec((L, ML_DK), lambda b, h, c: (row(b, h, c), ML_HEADS + h)),
            pl.BlockSpec((L, ML_DV), lambda b, h, c: (row(b, h, c), ML_HEADS + h)),
            pl.BlockSpec((L, ML_DV), lambda b, h, c: (row(b, h, c), 2 * ML_HEADS + h)),
            pl.BlockSpec((L, ML_DV), lambda b, h, c: (row(b, h, c), 3 * ML_HEADS + h)),
            pl.BlockSpec((None, None, None, 2, L), lambda b, h, c: (b, h, c, 0, 0)),
            pl.BlockSpec((None, 1, ML_DV), lambda b, h, c: (h, 0, 0)),
        ],
        out_specs=pl.BlockSpec((L, ML_DV), lambda b, h, c: (row(b, h, c), h)),
        out_shape=jax.ShapeDtypeStruct((bsz * t, ML_HEADS * ML_DV), BF16),
        scratch_shapes=[pltpu.VMEM((ML_DK, ML_DV), F32), pltpu.VMEM((1, ML_DK), F32), pltpu.VMEM((1, 1), F32)],
        compiler_params=_cparams(("parallel", "parallel", "arbitrary"), 40),
        name="mlstm",
    )(b_i.astype(F32), b_f.astype(F32), proj, proj, proj, proj, proj, gates,
      gain.reshape(ML_HEADS, 1, ML_DV).astype(F32))


def _causal_conv(x, tail_ref, w):
    L = x.shape[0]
    xx = jnp.concatenate([tail_ref[...], x], axis=0)
    y = w[CONV_WIDTH - 1:CONV_WIDTH, :] * x
    for j in range(CONV_WIDTH - 1):
        y = y + w[j:j + 1, :] * pltpu.roll(xx, CONV_WIDTH - 1 - j, axis=0)[8:, :]
    tail_ref[...] = x[L - 8:, :]
    return y


def _head_expand(n_heads, width):
    r = lax.broadcasted_iota(I32, (n_heads, n_heads * width), 0)
    c = lax.broadcasted_iota(I32, (n_heads, n_heads * width), 1)
    return ((c >= r * width) & (c < (r + 1) * width)).astype(F32)


def _ssd_body(x_ref, bm_ref, cm_ref, z_ref, dtc_ref, dtr_ref, wx_ref, wb_ref, wc_ref, bx_ref, bb_ref, bc_ref,
              dtbc_ref, dtbr_ref, alc_ref, alr_ref, dsk_ref, gain_ref, y_ref, st_scr, tx_scr, tb_scr, tc_scr):
    L = x_ref.shape[0]
    nh, hd = SSM_HEADS_PER_GROUP, SSM_HEAD_DIM

    @pl.when(pl.program_id(2) == 0)
    def _():
        st_scr[...] = jnp.zeros_like(st_scr)
        tx_scr[...] = jnp.zeros_like(tx_scr)
        tb_scr[...] = jnp.zeros_like(tb_scr)
        tc_scr[...] = jnp.zeros_like(tc_scr)

    xc = _silu(_causal_conv(x_ref[...], tx_scr, wx_ref[...]) + bx_ref[...])
    bc = _silu(_causal_conv(bm_ref[...], tb_scr, wb_ref[...]) + bb_ref[...])
    cc = _silu(_causal_conv(cm_ref[...], tc_scr, wc_ref[...]) + bc_ref[...])

    ri = lax.broadcasted_iota(I32, (L, L), 0)
    ci = lax.broadcasted_iota(I32, (L, L), 1)
    causal = ci <= ri
    dt_col = _softplus(dtc_ref[...] + dtbc_ref[...])
    acum = _fdot(causal.astype(F32), dt_col * (-jnp.exp(alc_ref[...])))
    dt_row = _softplus(dtr_ref[...] + dtbr_ref[...])
    acum_row = _fdot(dt_row * (-jnp.exp(alr_ref[...])), (ri <= ci).astype(F32))

    expand = _head_expand(nh, hd)
    acum_e = _fdot(acum, expand)
    xdt = xc * _fdot(dt_col, expand)
    xdt_b = xdt.astype(BF16)
    cb = _bdot_nt(cc, bc)

    lane = lax.broadcasted_iota(I32, (L, 2 * hd), 1)
    parts = []
    for hp in range(nh // 2):
        xpair = xdt_b[:, 2 * hd * hp:2 * hd * (hp + 1)]
        acc = None
        for e in range(2):
            hh = 2 * hp + e
            seg = acum[:, hh:hh + 1] - acum_row[hh:hh + 1, :]
            dec = jnp.exp(jnp.where(causal, seg, -jnp.inf))
            xsel = jnp.where((lane >= hd * e) & (lane < hd * (e + 1)), xpair, jnp.zeros_like(xpair))
            t = _bdot(cb * dec, xsel)
            acc = t if acc is None else acc + t
        parts.append(acc)
    y = jnp.concatenate(parts, axis=1)

    st = st_scr[...]
    y = y + _bdot(cc, st) * jnp.exp(acum_e)
    a_last = acum_e[L - 1:L, :]
    st_scr[...] = jnp.exp(a_last) * st + _bdot_tn(bc, xdt * jnp.exp(a_last - acum_e))

    y = y + xc * dsk_ref[...]
    y = y * _silu(z_ref[...])
    y = y * lax.rsqrt(jnp.mean(y * y, axis=-1, keepdims=True) + EPS) * gain_ref[...]
    y_ref[...] = y.astype(y_ref.dtype)


def _ssd(proj, dt_col, dt_row, conv_w, conv_b, dt_bias, a_log, d_skip, gain, bsz, t):
    L = min(SSD_CHUNK, t)
    nc = t // L
    G, nh, gw, ns = SSM_GROUPS, SSM_HEADS_PER_GROUP, SSM_GROUP_WIDTH, SSM_STATE
    row = lambda b, g, c: b * nc + c
    x_blk = 10240 // gw
    z_blk = 8192 // gw
    b_blk = 12288 // ns
    c_blk = 12800 // ns
    wb_blk = G * gw // ns
    conv_b2 = conv_b.reshape(1, -1).astype(F32)
    f = lambda a: a.astype(F32)
    return pl.pallas_call(
        _ssd_body,
        grid=(bsz, G, nc),
        in_specs=[
            pl.BlockSpec((L, gw), lambda b, g, c: (row(b, g, c), x_blk + g)),
            pl.BlockSpec((L, ns), lambda b, g, c: (row(b, g, c), b_blk + g)),
            pl.BlockSpec((L, ns), lambda b, g, c: (row(b, g, c), c_blk + g)),
            pl.BlockSpec((L, gw), lambda b, g, c: (row(b, g, c), z_blk + g)),
            pl.BlockSpec((None, None, L, nh), lambda b, g, c: (b, g, c, 0)),
            pl.BlockSpec((None, None, nh, L), lambda b, g, c: (b, g, 0, c)),
            pl.BlockSpec((CONV_WIDTH, gw), lambda b, g, c: (0, g)),
            pl.BlockSpec((CONV_WIDTH, ns), lambda b, g, c: (0, wb_blk + g)),
            pl.BlockSpec((CONV_WIDTH, ns), lambda b, g, c: (0, wb_blk + G + g)),
            pl.BlockSpec((1, gw), lambda b, g, c: (0, g)),
            pl.BlockSpec((1, ns), lambda b, g, c: (0, wb_blk + g)),
            pl.BlockSpec((1, ns), lambda b, g, c: (0, wb_blk + G + g)),
            pl.BlockSpec((None, 1, nh), lambda b, g, c: (g, 0, 0)),
            pl.BlockSpec((None, nh, 1), lambda b, g, c: (g, 0, 0)),
            pl.BlockSpec((None, 1, nh), lambda b, g, c: (g, 0, 0)),
            pl.BlockSpec((None, nh, 1), lambda b, g, c: (g, 0, 0)),
            pl.BlockSpec((None, 1, gw), lambda b, g, c: (g, 0, 0)),
            pl.BlockSpec((None, 1, gw), lambda b, g, c: (g, 0, 0)),
        ],
        out_specs=pl.BlockSpec((L, gw), lambda b, g, c: (row(b, g, c), g)),
        out_shape=jax.ShapeDtypeStruct((bsz * t, G * gw), BF16),
        scratch_shapes=[pltpu.VMEM((ns, gw), F32), pltpu.VMEM((8, gw), F32),
                        pltpu.VMEM((8, ns), F32), pltpu.VMEM((8, ns), F32)],
        compiler_params=_cparams(("parallel", "parallel", "arbitrary"), 40),
        name="ssd",
    )(proj, proj, proj, proj, dt_col, dt_row, f(conv_w), f(conv_w), f(conv_w), conv_b2, conv_b2, conv_b2,
      f(dt_bias).reshape(G, 1, nh), f(dt_bias).reshape(G, nh, 1), f(a_log).reshape(G, 1, nh),
      f(a_log).reshape(G, nh, 1), jnp.repeat(f(d_skip), SSM_HEAD_DIM).reshape(G, 1, gw), f(gain).reshape(G, 1, gw))


def _layer_ab(x2, bsz, t, norm_g, w_in, ml_b_i, ml_b_f, ml_norm, ssm_conv_w, ssm_conv_b, ssm_dt_bias,
              ssm_a_log, ssm_d, ssm_norm, w_out):
    m = bsz * t
    small = jnp.concatenate([w_in[:, 8192:8200], w_in[:, 13320:13352]], axis=1)
    w_main = jnp.concatenate(
        [w_in[:, :8192], w_in[:, 8200:13320], jnp.pad(small, ((0, 0), (0, 512 - small.shape[1])))], axis=1
    ).astype(BF16)
    proj = _norm_matmul(x2, 0, D_MODEL, norm_g, w_main, out_dtype=F32, name="proj_ab")
    gate_cols = proj[:, 13312:13352]
    ml_gates = gate_cols[:, :8].reshape(bsz, t // min(ML_CHUNK, t), min(ML_CHUNK, t), 2, ML_HEADS)
    ml_gates = jnp.transpose(ml_gates, (0, 4, 1, 3, 2))
    dt_raw = gate_cols[:, 8:40].reshape(bsz, t, SSM_GROUPS, SSM_HEADS_PER_GROUP)
    dt_col = jnp.transpose(dt_raw, (0, 2, 1, 3))
    dt_row = jnp.transpose(dt_raw, (0, 2, 3, 1))
    y_a = _mlstm(proj, ml_gates, ml_b_i, ml_b_f, ml_norm, bsz, t)
    y_b = _ssd(proj, dt_col, dt_row, ssm_conv_w, ssm_conv_b, ssm_dt_bias, ssm_a_log, ssm_d, ssm_norm, bsz, t)
    return _out_proj([y_a, y_b], w_out.astype(BF16), x2, jnp.ones((D_MODEL,), F32), final_norm=False,
                     name="out_ab")


def _indexer_body(iq_ref, iw_ref, ik_ref, mask_ref, key_scr, pos_scr, *, top_k):
    n_kt = key_scr.shape[0]
    i = pl.program_id(1)
    start = i * IDX_Q
    n_tiles = (start + IDX_Q + IDX_KT - 1) // IDX_KT
    iqb = iq_ref[...].astype(BF16)
    iw = iw_ref[...] * (IDX_HEADS ** -0.5 * IDX_DIM ** -0.5)
    qpos = start + lax.broadcasted_iota(I32, (IDX_Q, 1), 0)
    limit = (lax.shift_right_logical(qpos, int(math.log2(DSA_CHUNK))) + 1) * DSA_CHUNK
    zeros_k = jnp.zeros((IDX_DIM, IDX_KT), BF16)
    kiota = lax.broadcasted_iota(I32, (1, IDX_KT), 1)

    def score_tile(tt, carry):
        ikt = ik_ref[tt]
        ik_even = jnp.concatenate([ikt, zeros_k], axis=0)
        ik_odd = jnp.concatenate([zeros_k, ikt], axis=0)
        acc = jnp.zeros((IDX_Q, IDX_KT), F32)
        for p in range(IDX_HEADS // 2):
            xq = iqb[:, 2 * IDX_DIM * p:2 * IDX_DIM * (p + 1)]
            s0 = jnp.dot(xq, ik_even, preferred_element_type=F32)
            s1 = jnp.dot(xq, ik_odd, preferred_element_type=F32)
            acc = acc + iw[:, 2 * p:2 * p + 1] * jnp.maximum(s0, 0.0)
            acc = acc + iw[:, 2 * p + 1:2 * p + 2] * jnp.maximum(s1, 0.0)
        bits = lax.bitcast_convert_type(acc + 0.0, I32)
        key = jnp.where(bits < 0, bits ^ jnp.int32(0x7FFFFFFF), bits)
        key_scr[tt] = jnp.where(tt * IDX_KT + kiota < limit, key, jnp.int32(INT_MIN))
        return carry

    lax.fori_loop(0, n_tiles, score_tile, 0)

    def count(pred):
        def body(tt, acc):
            w = jnp.where(pred(key_scr[tt], tt * IDX_KT + kiota), 1.0, 0.0)
            for s in range(IDX_KT // LANES):
                acc = acc + w[:, s * LANES:(s + 1) * LANES]
            return acc
        acc = lax.fori_loop(0, n_tiles, body, jnp.zeros((IDX_Q, LANES), F32))
        return jnp.sum(acc, axis=1, keepdims=True)

    thr = jnp.full((IDX_Q, 1), INT_MIN, I32)
    for bit in range(31, -1, -1):
        cand = jnp.zeros_like(thr) if bit == 31 else thr + jnp.int32(1 << bit)
        thr = jnp.where(count(lambda k, pos, cand=cand: k >= cand) >= top_k, cand, thr)

    n_gt = count(lambda k, pos: k > thr)
    n_ge = count(lambda k, pos: k >= thr)
    room = top_k - n_gt
    pos_scr[...] = jnp.full((IDX_Q, 1), 2 * n_kt * IDX_KT, I32)

    @pl.when(jnp.max(jnp.where(thr > INT_MIN, n_ge, 0.0)) > top_k)
    def _():
        pcut = jnp.zeros((IDX_Q, 1), I32)
        for bit in range(int(math.log2(n_kt * IDX_KT)), -1, -1):
            cand = pcut + jnp.int32(1 << bit)
            n_tie = count(lambda k, pos, cand=cand: (k == thr) & (pos < cand))
            pcut = jnp.where(n_tie <= room, cand, pcut)
        pos_scr[...] = pcut

    pcut = pos_scr[...]

    def write_tile(tt, carry):
        k = key_scr[tt]
        pos = tt * IDX_KT + kiota
        sel = (k > thr) | ((k == thr) & (pos < pcut))
        sel = sel & (k > INT_MIN)
        mask_ref[tt] = jnp.where(sel, 1, 0).astype(jnp.int8)
        return carry

    lax.fori_loop(0, n_tiles, write_tile, 0)

    def zero_tile(tt, carry):
        mask_ref[tt] = jnp.zeros((IDX_Q, IDX_KT), jnp.int8)
        return carry

    lax.fori_loop(n_tiles, n_kt, zero_tile, 0)


def _indexer(proj, iq_blk, iw, ik_tiles, bsz, t):
    nq = t // IDX_Q
    n_kt = t // IDX_KT
    top_k = min(TOPK_MAX, t // 4)
    return pl.pallas_call(
        functools.partial(_indexer_body, top_k=top_k),
        grid=(bsz, nq),
        in_specs=[
            pl.BlockSpec((IDX_Q, IDX_HEADS * IDX_DIM), lambda b, i: (b * nq + i, iq_blk)),
            pl.BlockSpec((IDX_Q, IDX_HEADS), lambda b, i: (b * nq + i, 0)),
            pl.BlockSpec((None, n_kt, IDX_DIM, IDX_KT), lambda b, i: (b, 0, 0, 0)),
        ],
        out_specs=pl.BlockSpec((None, n_kt, IDX_Q, IDX_KT), lambda b, i: (b * nq + i, 0, 0, 0)),
        out_shape=jax.ShapeDtypeStruct((bsz * nq, n_kt, IDX_Q, IDX_KT), jnp.int8),
        scratch_shapes=[pltpu.VMEM((n_kt, IDX_Q, IDX_KT), I32), pltpu.VMEM((IDX_Q, 1), I32)],
        compiler_params=_cparams(("parallel", "arbitrary"), 40),
        name="dsa_indexer",
    )(proj, iw, ik_tiles)


NEG_BIG = -3.0e38
LOG2_E = math.log2(math.e)


def _attn_body(q_ref, k_ref, v_ref, mask_ref, z_ref, tb_ref, o_ref, acc_scr, m_scr):
    T = ATT_T
    D = ATT_HEAD_DIM
    nsub = T // ATT_SUB
    i = pl.program_id(1)
    j = pl.program_id(2)

    @pl.when(j == 0)
    def _():
        acc_scr[...] = jnp.zeros_like(acc_scr)
        m_scr[...] = jnp.full_like(m_scr, NEG_BIG)

    def tile_step(near):
        sel = jnp.concatenate([mask_ref[a, 0].astype(F32) for a in range(mask_ref.shape[0])], axis=0)
        mask_bias = (sel - 1.0) * (-NEG_BIG)
        ones = jnp.ones((T, D), BF16)

        def pair_body(p, carry):
            q2, k2, v2 = q_ref[p], k_ref[p], v_ref[p]
            m_old = [m_scr[2 * p + e] for e in range(2)]
            a_old = [acc_scr[2 * p + e] for e in range(2)]
            new = []
            logits = [_bdot_nt(q2[:, D * e:D * (e + 1)], k2[:, D * e:D * (e + 1)]) for e in range(2)]
            for e in range(2):
                s = logits[e] + mask_bias
                if near is not None:
                    tb = tb_ref[2 * p + e]
                    zero = jnp.zeros((ATT_SUB, ATT_SUB), F32)
                    rows = []
                    for a in range(nsub):
                        cols = []
                        for bb in range(nsub):
                            rel_blocks = (bb - a) if near == "diag" else (bb - a - nsub)
                            cols.append(tb[0] if rel_blocks == 0 else tb[1] if rel_blocks == -1 else zero)
                        rows.append(jnp.concatenate(cols, axis=1))
                    s = s + jnp.concatenate(rows, axis=0)
                m_new = jnp.maximum(m_old[e], jnp.max(s, axis=1, keepdims=True))
                alpha = jnp.exp2(m_old[e] - m_new)
                pr = jnp.exp2(s - jnp.concatenate([m_new] * (T // LANES), axis=1))
                v_aug = jnp.concatenate([v2[:, D * e:D * (e + 1)], ones], axis=1)
                a_new = jnp.concatenate([alpha] * (2 * D // LANES), axis=1) * a_old[e] + _bdot(pr, v_aug)
                new.append((m_new, a_new))
            for e in range(2):
                m_scr[2 * p + e] = new[e][0]
                acc_scr[2 * p + e] = new[e][1]
            return carry

        lax.fori_loop(0, ATT_HEADS // 2, pair_body, 0, unroll=4)

    @pl.when(j < i - 1)
    def _():
        tile_step(None)

    @pl.when(j == i - 1)
    def _():
        tile_step("prev")

    @pl.when(j == i)
    def _():
        tile_step("diag")
        for p in range(ATT_HEADS // 2):
            for e in range(2):
                hh = 2 * p + e
                a = acc_scr[hh]
                o = a[:, :D] / a[:, D:]
                o = o * _silu(z_ref[p, :, D * e:D * (e + 1)].astype(F32))
                o_ref[:, D * hh:D * (hh + 1)] = o.astype(o_ref.dtype)


def _attention(qz, kv, mask, tb, bsz, t):
    T = ATT_T
    nq = t // T
    npair = ATT_HEADS // 2
    pw = 2 * ATT_HEAD_DIM
    nrow = T // IDX_Q
    ncol = T // IDX_KT
    return pl.pallas_call(
        _attn_body,
        grid=(bsz, nq, nq),
        in_specs=[
            pl.BlockSpec((npair, T, pw), lambda b, i, j: (0, b * nq + i, 0)),
            pl.BlockSpec((npair, T, pw), lambda b, i, j: (0, b * nq + jnp.minimum(j, i), 0)),
            pl.BlockSpec((npair, T, pw), lambda b, i, j: (1, b * nq + jnp.minimum(j, i), 0)),
            pl.BlockSpec((nrow, ncol, IDX_Q, IDX_KT), lambda b, i, j: (b * nq + i, jnp.minimum(j, i), 0, 0)),
            pl.BlockSpec((npair, T, pw), lambda b, i, j: (1, b * nq + i, 0)),
            pl.BlockSpec((ATT_HEADS, 2, ATT_SUB, ATT_SUB), lambda b, i, j: (0, 0, 0, 0)),
        ],
        out_specs=pl.BlockSpec((T, ATT_HEADS * ATT_HEAD_DIM), lambda b, i, j: (b * nq + i, 0)),
        out_shape=jax.ShapeDtypeStruct((bsz * t, ATT_HEADS * ATT_HEAD_DIM), BF16),
        scratch_shapes=[pltpu.VMEM((ATT_HEADS, T, 2 * ATT_HEAD_DIM), F32), pltpu.VMEM((ATT_HEADS, T, LANES), F32)],
        compiler_params=_cparams(("parallel", "parallel", "arbitrary"), 56),
        name="dsa_attention",
    )(qz, kv, kv, mask, qz, tb)


def _t5_bucket(rel):
    nb = REL_BUCKETS // 2
    max_exact = nb // 2
    ret = jnp.where(rel > 0, nb, 0)
    n = jnp.abs(rel)
    nf = jnp.maximum(n, max_exact).astype(F32)
    large = max_exact + (jnp.log(nf / max_exact) / math.log(REL_MAX_DIST / max_exact)
                         * (nb - max_exact)).astype(jnp.int32)
    large = jnp.minimum(large, nb - 1)
    return ret + jnp.where(n < max_exact, n, large)


def _rel_bias_blocks(rel_bias):
    r = jnp.arange(ATT_SUB)[:, None]
    c = jnp.arange(ATT_SUB)[None, :]
    rel = jnp.stack([c - r, c - r - ATT_SUB])
    assert ATT_SUB >= REL_MAX_DIST and ATT_SUB % DSA_CHUNK == 0
    rb = rel_bias.astype(F32)
    far = rb[_t5_bucket(jnp.int32(-2 * ATT_SUB))]
    onehot = (_t5_bucket(rel)[..., None] == jnp.arange(REL_BUCKETS)).astype(F32)
    table = jnp.einsum("vrcb,bh->hvrc", onehot, rb - far, precision=HI)
    return table * LOG2_E


GDN_PREP_ROWS = 256
GDN_SOLVE_LANES = 256


def _conv_with_prev(x, prev, w):
    xx = jnp.concatenate([prev, x], axis=0)
    y = w[CONV_WIDTH - 1:CONV_WIDTH, :] * x
    for j in range(CONV_WIDTH - 1):
        y = y + w[j:j + 1, :] * pltpu.roll(xx, CONV_WIDTH - 1 - j, axis=0)[8:, :]
    return y


def _gdn_prep_body(v_ref, q_ref, k_ref, pv_ref, pq_ref, pk_ref, bc_ref, ac_ref, ar_ref, wq_ref, wk_ref, wv_ref,
                   dtbc_ref, dtbr_ref, alc_ref, alr_ref,
                   qn_ref, rhs_ref, lu_ref, a_ref, egc_ref, egl_ref):
    L = GDN_CHUNK
    D = GDN_HEAD_DIM
    nv = GDN_V_HEADS // GDN_GROUPS
    rep = GDN_V_HEADS // GDN_QK_HEADS
    seq_start = pl.program_id(2) == 0

    def conv(x_ref, p_ref, w_ref):
        prev = jnp.where(seq_start, 0.0, p_ref[...])
        return _silu(_conv_with_prev(x_ref[...], prev, w_ref[...]))

    qc = conv(q_ref, pq_ref, wq_ref)
    kc = conv(k_ref, pk_ref, wk_ref)
    vc = conv(v_ref, pv_ref, wv_ref)
    qn, kn = [], []
    for kh in range(nv // rep):
        qh = qc[:, D * kh:D * (kh + 1)]
        kk_ = kc[:, D * kh:D * (kh + 1)]
        qh = qh * lax.rsqrt(jnp.sum(qh * qh, axis=-1, keepdims=True) + EPS) * (D ** -0.5)
        kn.append(kk_ * lax.rsqrt(jnp.sum(kk_ * kk_, axis=-1, keepdims=True) + EPS))
        qn.append(qh.astype(BF16))
        qn_ref[:, D * kh:D * (kh + 1)] = qn[kh]

    beta = _sigmoid(bc_ref[...])
    g_col = -jnp.exp(alc_ref[...]) * _softplus(ac_ref[...] + dtbc_ref[...])
    ri = lax.broadcasted_iota(I32, (L, L), 0)
    ci = lax.broadcasted_iota(I32, (L, L), 1)
    causal = ci <= ri
    strict = ci < ri
    for cc in range(a_ref.shape[0]):
        rows = slice(L * cc, L * (cc + 1))
        gc_col = _fdot(causal.astype(F32), g_col[rows, :])
        g_row = -jnp.exp(alr_ref[...]) * _softplus(ar_ref[cc] + dtbr_ref[...])
        gc_row = _fdot(g_row, (ri <= ci).astype(F32))
        eg_col = jnp.exp(gc_col)
        egc_ref[rows, :] = eg_col
        for kh in range(nv // rep):
            kf = kn[kh][rows, :]
            kb = kf.astype(BF16)
            kk = _bdot_nt(kb, kb)
            qk = _bdot_nt(qn[kh][rows, :], kb)
            for e in range(rep):
                vh = rep * kh + e
                gcol = gc_col[:, vh:vh + 1]
                bcol = beta[rows, vh:vh + 1]
                dec = jnp.exp(jnp.where(causal, gcol - gc_row[vh:vh + 1, :], -jnp.inf))
                a_ref[cc, vh] = jnp.where(strict, bcol * kk * dec, 0.0)
                g_last = gcol[L - 1:L, :]
                kw_t = jnp.transpose(kf * jnp.exp(g_last - gcol))
                lu_ref[cc, vh] = jnp.concatenate([qk * dec, kw_t], axis=0).astype(lu_ref.dtype)
                egl_ref[cc, vh:vh + 1, :] = jnp.broadcast_to(jnp.exp(g_last), (1, LANES))
                rhs_ref[rows, 2 * D * vh:2 * D * (vh + 1)] = jnp.concatenate(
                    [vc[rows, D * vh:D * (vh + 1)] * bcol, kf * (bcol * eg_col[:, vh:vh + 1])], axis=1
                ).astype(rhs_ref.dtype)


def _gdn_solve_body(a_ref, t_ref, t_scr):
    L, _, PB = a_ref.shape
    UN = 8
    sub = lax.broadcasted_iota(I32, (L, PB), 0)
    t_scr[...] = jnp.zeros_like(t_scr)
    t_scr[0] = jnp.where(sub == 0, 1.0, 0.0)

    def row_body(i, carry):
        def col_block(jb, acc):
            j0 = pl.multiple_of(jb * UN, UN)
            a_blk = a_ref[i, pl.ds(j0, UN), :]
            for jj in range(UN):
                acc = acc + a_blk[jj:jj + 1, :] * t_scr[j0 + jj]
            return acc

        acc = lax.fori_loop(0, (i + UN - 1) // UN, col_block, jnp.zeros((L, PB), F32))
        t_scr[i] = jnp.where(sub == i, 1.0, 0.0) - acc
        return carry

    lax.fori_loop(1, L, row_body, 0)
    t_ref[...] = t_scr[...].astype(t_ref.dtype)


def _gdn_scan_body(qn_ref, rhs_ref, lu_ref, t_ref, z_ref, egc_ref, egl_ref, gain_ref, y_ref, s_scr):
    L = qn_ref.shape[0]
    D = GDN_HEAD_DIM
    nv = GDN_V_HEADS // GDN_GROUPS
    rep = GDN_V_HEADS // GDN_QK_HEADS

    @pl.when(pl.program_id(2) == 0)
    def _():
        s_scr[...] = jnp.zeros_like(s_scr)

    eg_col = egc_ref[...]
    z = z_ref[...]
    heads = range(nv)
    sol = [_bdot(t_ref[vh], rhs_ref[:, 2 * D * vh:2 * D * (vh + 1)]) for vh in heads]
    s_st = [s_scr[vh] for vh in heads]
    r_s = [_bdot(jnp.concatenate([sol[vh][:, D:].astype(BF16),
                                  qn_ref[:, D * (vh // rep):D * (vh // rep + 1)]], axis=0), s_st[vh])
           for vh in heads]
    u = [sol[vh][:, :D] - r_s[vh][:L] for vh in heads]
    r_u = [_bdot(lu_ref[vh], u[vh]) for vh in heads]
    for vh in heads:
        s_scr[vh] = egl_ref[vh:vh + 1, :] * s_st[vh] + r_u[vh][L:]
        o = r_s[vh][L:] * eg_col[:, vh:vh + 1] + r_u[vh][:L]
        o = o * lax.rsqrt(jnp.mean(o * o, axis=-1, keepdims=True) + EPS) * gain_ref[...]
        y_ref[:, D * vh:D * (vh + 1)] = (o * _silu(z[:, D * vh:D * (vh + 1)])).astype(y_ref.dtype)


def _gdn(proj, b_col, a_col, a_row, conv_w, dt_bias, a_log, gain, bsz, t):
    L = GDN_CHUNK
    nc = t // L
    G = GDN_GROUPS
    nv = GDN_V_HEADS // G
    qw = GDN_QK_HEADS // G * GDN_HEAD_DIM
    vw = nv * GDN_HEAD_DIM
    m = bsz * t
    z_blk, q_blk, k_blk = 4096 // vw, 8192 // qw, 10240 // qw
    f = lambda a: a.astype(F32)
    cw = f(conv_w)
    pc_spec = pl.BlockSpec((None, 1, nv), lambda b, g, c: (g, 0, 0))
    pr_spec = pl.BlockSpec((None, nv, 1), lambda b, g, c: (g, 0, 0))
    wq_spec = pl.BlockSpec((CONV_WIDTH, qw), lambda b, g, c: (0, g))
    wk_spec = pl.BlockSpec((CONV_WIDTH, qw), lambda b, g, c: (0, G + g))
    wv_spec = pl.BlockSpec((CONV_WIDTH, vw), lambda b, g, c: (0, (2 * G * qw) // vw + g))

    LB = min(GDN_PREP_ROWS, t)
    ncb = LB // L
    nb = t // LB
    rowb = lambda b, g, c: b * nb + c
    prev8 = lambda b, g, c: jnp.maximum((b * t + c * LB) // 8 - 1, 0)
    D = GDN_HEAD_DIM
    qn, rhs, lu, a_mat, egc, egl = pl.pallas_call(
        _gdn_prep_body,
        grid=(bsz, G, nb),
        in_specs=[
            pl.BlockSpec((LB, vw), lambda b, g, c: (rowb(b, g, c), g)),
            pl.BlockSpec((LB, qw), lambda b, g, c: (rowb(b, g, c), q_blk + g)),
            pl.BlockSpec((LB, qw), lambda b, g, c: (rowb(b, g, c), k_blk + g)),
            pl.BlockSpec((8, vw), lambda b, g, c: (prev8(b, g, c), g)),
            pl.BlockSpec((8, qw), lambda b, g, c: (prev8(b, g, c), q_blk + g)),
            pl.BlockSpec((8, qw), lambda b, g, c: (prev8(b, g, c), k_blk + g)),
            pl.BlockSpec((None, None, LB, nv), lambda b, g, c: (b, g, c, 0)),
            pl.BlockSpec((None, None, LB, nv), lambda b, g, c: (b, g, c, 0)),
            pl.BlockSpec((None, None, ncb, nv, L), lambda b, g, c: (b, g, c, 0, 0)),
            wq_spec, wk_spec, wv_spec, pc_spec, pr_spec, pc_spec, pr_spec,
        ],
        out_specs=[
            pl.BlockSpec((LB, qw), lambda b, g, c: (rowb(b, g, c), g)),
            pl.BlockSpec((LB, 2 * vw), lambda b, g, c: (rowb(b, g, c), g)),
            pl.BlockSpec((None, None, ncb, nv, L + D, L), lambda b, g, c: (b, g, c, 0, 0, 0)),
            pl.BlockSpec((None, None, ncb, nv, L, L), lambda b, g, c: (b, g, c, 0, 0, 0)),
            pl.BlockSpec((None, None, LB, nv), lambda b, g, c: (b, g, c, 0)),
            pl.BlockSpec((None, None, ncb, nv, LANES), lambda b, g, c: (b, g, c, 0, 0)),
        ],
        out_shape=[
            jax.ShapeDtypeStruct((m, G * qw), BF16),
            jax.ShapeDtypeStruct((m, 2 * G * vw), BF16),
            jax.ShapeDtypeStruct((bsz, G, nc, nv, L + D, L), BF16),
            jax.ShapeDtypeStruct((bsz, G, nc, nv, L, L), F32),
            jax.ShapeDtypeStruct((bsz, G, t, nv), F32),
            jax.ShapeDtypeStruct((bsz, G, nc, nv, LANES), F32),
        ],
        compiler_params=_cparams(("parallel", "parallel", "parallel"), 48),
        name="gdn_prep",
    )(proj, proj, proj, proj, proj, proj, b_col, a_col, a_row, cw, cw, cw,
      f(dt_bias).reshape(G, 1, nv), f(dt_bias).reshape(G, nv, 1), f(a_log).reshape(G, 1, nv),
      f(a_log).reshape(G, nv, 1))

    n_prob = bsz * G * nc * nv
    pb = min(GDN_SOLVE_LANES, n_prob)
    a_t = jnp.transpose(a_mat.reshape(n_prob, L, L), (1, 2, 0))
    t_t = pl.pallas_call(
        _gdn_solve_body,
        grid=(n_prob // pb,),
        in_specs=[pl.BlockSpec((L, L, pb), lambda p: (0, 0, p))],
        out_specs=pl.BlockSpec((L, L, pb), lambda p: (0, 0, p)),
        out_shape=jax.ShapeDtypeStruct((L, L, n_prob), BF16),
        scratch_shapes=[pltpu.VMEM((L, L, pb), F32)],
        compiler_params=_cparams(("parallel",), 40),
        name="gdn_solve",
    )(a_t)
    t_mat = jnp.transpose(t_t, (2, 0, 1)).reshape(bsz, G, nc, nv, L, L)

    row = lambda b, g, c: b * nc + c
    return pl.pallas_call(
        _gdn_scan_body,
        grid=(bsz, G, nc),
        in_specs=[
            pl.BlockSpec((L, qw), lambda b, g, c: (row(b, g, c), g)),
            pl.BlockSpec((L, 2 * vw), lambda b, g, c: (row(b, g, c), g)),
            pl.BlockSpec((None, None, None, nv, L + D, L), lambda b, g, c: (b, g, c, 0, 0, 0)),
            pl.BlockSpec((None, None, None, nv, L, L), lambda b, g, c: (b, g, c, 0, 0, 0)),
            pl.BlockSpec((L, vw), lambda b, g, c: (row(b, g, c), z_blk + g)),
            pl.BlockSpec((None, None, L, nv), lambda b, g, c: (b, g, c, 0)),
            pl.BlockSpec((None, None, None, nv, LANES), lambda b, g, c: (b, g, c, 0, 0)),
            pl.BlockSpec((1, GDN_HEAD_DIM), lambda b, g, c: (0, 0)),
        ],
        out_specs=pl.BlockSpec((L, vw), lambda b, g, c: (row(b, g, c), g)),
        out_shape=jax.ShapeDtypeStruct((m, GDN_V_HEADS * GDN_HEAD_DIM), BF16),
        scratch_shapes=[pltpu.VMEM((nv, GDN_HEAD_DIM, GDN_HEAD_DIM), F32)],
        compiler_params=_cparams(("parallel", "parallel", "arbitrary"), 40),
        name="gdn_scan",
    )(qn, rhs, lu, t_mat, proj, egc, egl, f(gain).reshape(1, GDN_HEAD_DIM))


def _layer_cd(x2, bsz, t, norm_g, w_in, kv_norm, w_uk, w_uv, gdn_conv_w, gdn_dt_bias, gdn_a_log, gdn_norm,
              w_out, rel_bias, final_g, final_norm):
    m = bsz * t
    small = jnp.concatenate([w_in[:, 5632:5712], w_in[:, 18000:18064]], axis=1)
    w_main = jnp.concatenate(
        [w_in[:, 9808:13904], w_in[:, 13904:18000], w_in[:, 5712:9808], w_in[:, 4608:5632], w_in[:, 2048:2560],
         jnp.pad(small, ((0, 0), (0, 512 - small.shape[1])))], axis=1).astype(BF16)
    proj = _norm_matmul(x2, 0, D_MODEL, norm_g, w_main, out_dtype=F32, name="proj_cd")
    w_attn = jnp.concatenate([w_in[:, :2048] * (ATT_HEAD_DIM ** -0.5 * LOG2_E), w_in[:, 2560:4608]],
                             axis=1).astype(BF16)
    qz = _norm_matmul(x2, 0, D_MODEL, norm_g, w_attn, out_dtype=BF16, blocked_width=2 * ATT_HEAD_DIM,
                      name="proj_cd_attn")
    w_kv = jnp.concatenate([w_uk.reshape(KV_RANK, -1), w_uv.reshape(KV_RANK, -1)], axis=1).astype(BF16)
    kv = _norm_matmul(proj, 13312 // KV_RANK, KV_RANK, kv_norm, w_kv, out_dtype=BF16,
                      blocked_width=2 * ATT_HEAD_DIM, name="dsa_kv")

    gate_cols = proj[:, 13824:13824 + 144]
    n_kt = t // IDX_KT
    ik_tiles = jnp.transpose(gate_cols[:, :IDX_DIM].reshape(bsz, n_kt, IDX_KT, IDX_DIM), (0, 1, 3, 2)).astype(BF16)
    iw = gate_cols[:, IDX_DIM:IDX_DIM + IDX_HEADS]
    mask = _indexer(proj, 12288 // (IDX_HEADS * IDX_DIM), iw, ik_tiles, bsz, t)
    y_c = _attention(qz, kv, mask, _rel_bias_blocks(rel_bias), bsz, t)

    nv = GDN_V_HEADS // GDN_GROUPS
    b_pre = gate_cols[:, 80:112].reshape(bsz, t, GDN_GROUPS, nv)
    a_pre = gate_cols[:, 112:144].reshape(bsz, t, GDN_GROUPS, nv)
    col = lambda a: jnp.transpose(a, (0, 2, 1, 3))
    rowf = lambda a: jnp.transpose(a.reshape(bsz, t // GDN_CHUNK, GDN_CHUNK, GDN_GROUPS, nv), (0, 3, 1, 4, 2))
    y_d = _gdn(proj, col(b_pre), col(a_pre), rowf(a_pre), gdn_conv_w, gdn_dt_bias, gdn_a_log, gdn_norm, bsz, t)
    return _out_proj([y_c, y_d], w_out.astype(BF16), x2, final_g, final_norm=final_norm, name="out_cd")


def kernel(x, rel_bias, ab_norm, ab_w_in, ab_ml_b_i, ab_ml_b_f, ab_ml_norm, ab_ssm_conv_w, ab_ssm_conv_b, ab_ssm_dt_bias, ab_ssm_a_log, ab_ssm_d, ab_ssm_norm, ab_w_out, cd_norm, cd_w_in, cd_kv_norm, cd_w_uk, cd_w_uv, cd_gdn_conv_w, cd_gdn_dt_bias, cd_gdn_a_log, cd_gdn_norm, cd_w_out, final_norm):
    bsz, t, d = x.shape
    x2 = x.reshape(bsz * t, d)
    x2 = _layer_ab(x2, bsz, t, ab_norm[0], ab_w_in[0], ab_ml_b_i[0], ab_ml_b_f[0], ab_ml_norm[0],
                   ab_ssm_conv_w[0], ab_ssm_conv_b[0], ab_ssm_dt_bias[0], ab_ssm_a_log[0], ab_ssm_d[0],
                   ab_ssm_norm[0], ab_w_out[0])
    x2 = _layer_cd(x2, bsz, t, cd_norm[0], cd_w_in[0], cd_kv_norm[0], cd_w_uk[0], cd_w_uv[0], cd_gdn_conv_w[0],
                   cd_gdn_dt_bias[0], cd_gdn_a_log[0], cd_gdn_norm[0], cd_w_out[0], rel_bias, final_norm, True)
    return x2.reshape(bsz, t, d)
```

```python
import functools
import math

import jax
import jax.numpy as jnp
from jax import lax
from jax.experimental import pallas as pl
from jax.experimental.pallas import tpu as pltpu

F32 = jnp.float32
BF16 = jnp.bfloat16
I32 = jnp.int32
I16 = jnp.int16
HI = lax.Precision.HIGHEST

EPS = 1e-6
CONV_WIDTH = 4
D_MODEL = 2048

ML_HEADS = 4
ML_DV = 512
ML_DK = 256

SSM_GROUPS = 4
SSM_HEADS_PER_GROUP = 8
SSM_HEAD_DIM = 64
SSM_STATE = 128
SSM_GROUP_WIDTH = SSM_HEADS_PER_GROUP * SSM_HEAD_DIM

ATT_HEADS = 16
ATT_HEAD_DIM = 128
KV_RANK = 512
IDX_HEADS = 16
IDX_DIM = 64
TOPK_MAX = 256
DSA_CHUNK = 64
REL_BUCKETS = 32
REL_MAX_DIST = 128

GDN_QK_HEADS = 16
GDN_V_HEADS = 32
GDN_HEAD_DIM = 128
GDN_GROUPS = 2
GDN_CHUNK = 64

VMEM_BYTES_V7X = 64 * 1024 * 1024
LANES = 128

ML_CHUNK = 256
SSD_CHUNK = 128
IDX_Q = 256
IDX_KT = 512
ATT_T = 512
ATT_SUB = 128
INT_MIN = -(2 ** 31)
HALF16 = 2 ** 15


def _cparams(sem, vmem_mib):
    return pltpu.CompilerParams(dimension_semantics=sem, vmem_limit_bytes=vmem_mib * 1024 * 1024)


def _bdot(a, b):
    return jnp.dot(a.astype(BF16), b.astype(BF16), preferred_element_type=F32)


def _bdot_nt(a, b):
    return lax.dot_general(a.astype(BF16), b.astype(BF16), (((1,), (1,)), ((), ())), preferred_element_type=F32)


def _bdot_tn(a, b):
    return lax.dot_general(a.astype(BF16), b.astype(BF16), (((0,), (0,)), ((), ())), preferred_element_type=F32)


def _fdot(a, b):
    return jnp.dot(a, b, preferred_element_type=F32, precision=HI)


def _softplus(x):
    return jnp.maximum(x, 0.0) + jnp.log1p(jnp.exp(-jnp.abs(x)))


def _sigmoid(x):
    return 1.0 / (1.0 + jnp.exp(-x))


def _silu(x):
    return x * _sigmoid(x)


def _norm_matmul_body(x_ref, g_ref, w_ref, o_ref, h_ref, *, blocked):
    @pl.when(pl.program_id(1) == 0)
    def _():
        x = x_ref[...].astype(F32)
        ms = jnp.mean(x * x, axis=-1, keepdims=True)
        h_ref[...] = (x * lax.rsqrt(ms + EPS) * g_ref[...]).astype(h_ref.dtype)

    acc = jnp.dot(h_ref[...], w_ref[...], preferred_element_type=F32)
    if blocked:
        bw = o_ref.shape[-1]
        for t in range(o_ref.shape[0]):
            o_ref[t] = acc[:, t * bw:(t + 1) * bw].astype(o_ref.dtype)
    else:
        o_ref[...] = acc.astype(o_ref.dtype)


def _norm_matmul(x, x_col_block, k_dim, gain, w, *, out_dtype, blocked_width=None, tm=1024, tn=512, name):
    m = x.shape[0]
    n = w.shape[1]
    tm = min(tm, m)
    assert m % tm == 0 and n % tn == 0 and w.shape[0] == k_dim
    grid = (m // tm, n // tn)
    if blocked_width is None:
        out_shape = jax.ShapeDtypeStruct((m, n), out_dtype)
        out_spec = pl.BlockSpec((tm, tn), lambda i, j: (i, j))
    else:
        nb = tn // blocked_width
        out_shape = jax.ShapeDtypeStruct((n // blocked_width, m, blocked_width), out_dtype)
        out_spec = pl.BlockSpec((nb, tm, blocked_width), lambda i, j: (j, i, 0))
    return pl.pallas_call(
        functools.partial(_norm_matmul_body, blocked=blocked_width is not None),
        grid=grid,
        in_specs=[
            pl.BlockSpec((tm, k_dim), lambda i, j: (i, x_col_block), pipeline_mode=pl.Buffered(1)),
            pl.BlockSpec((1, k_dim), lambda i, j: (0, 0)),
            pl.BlockSpec((k_dim, tn), lambda i, j: (0, j)),
        ],
        out_specs=out_spec,
        out_shape=out_shape,
        scratch_shapes=[pltpu.VMEM((tm, k_dim), BF16)],
        compiler_params=_cparams(("parallel", "arbitrary"), 48),
        name=name,
    )(x, gain.reshape(1, k_dim).astype(F32), w)


def _out_proj_body(*refs, n_y, k_tiles, final_norm):
    y_refs = refs[:n_y]
    w_ref, res_ref, g_ref, o_ref, acc_ref = refs[n_y:]
    k = pl.program_id(1)

    @pl.when(k == 0)
    def _():
        acc_ref[...] = jnp.zeros_like(acc_ref)

    lo = 0
    for y_ref, nk in zip(y_refs, k_tiles):
        @pl.when((k >= lo) & (k < lo + nk))
        def _(y_ref=y_ref):
            acc_ref[...] += jnp.dot(y_ref[...], w_ref[...], preferred_element_type=F32)
        lo += nk

    @pl.when(k == lo - 1)
    def _():
        xn = res_ref[...] + acc_ref[...]
        if final_norm:
            ms = jnp.mean(xn * xn, axis=-1, keepdims=True)
            xn = xn * lax.rsqrt(ms + EPS) * g_ref[...]
        o_ref[...] = xn


def _out_proj(ys, w, res, gain, *, final_norm, tm=512, tk=1024, name):
    m, d = res.shape
    tm = min(tm, m)
    k_tiles = [y.shape[1] // tk for y in ys]
    assert sum(k_tiles) * tk == w.shape[0]
    y_specs = []
    lo = 0
    for nk in k_tiles:
        y_specs.append(pl.BlockSpec((tm, tk), lambda i, k, lo=lo, nk=nk: (i, jnp.clip(k - lo, 0, nk - 1))))
        lo += nk
    return pl.pallas_call(
        functools.partial(_out_proj_body, n_y=len(ys), k_tiles=k_tiles, final_norm=final_norm),
        grid=(m // tm, lo),
        in_specs=y_specs + [
            pl.BlockSpec((tk, d), lambda i, k: (k, 0)),
            pl.BlockSpec((tm, d), lambda i, k: (i, 0)),
            pl.BlockSpec((1, d), lambda i, k: (0, 0)),
        ],
        out_specs=pl.BlockSpec((tm, d), lambda i, k: (i, 0)),
        out_shape=jax.ShapeDtypeStruct((m, d), F32),
        scratch_shapes=[pltpu.VMEM((tm, d), F32)],
        compiler_params=_cparams(("parallel", "arbitrary"), 48),
        name=name,
    )(*ys, w, res, gain.reshape(1, d).astype(F32))


def _mlstm_body(bi_ref, bf_ref, q_ref, k_ref, v_ref, op_ref, z_ref, g_ref, gain_ref, y_ref,
                c_scr, n_scr, m_scr):
    L = q_ref.shape[0]
    h = pl.program_id(1)

    @pl.when(pl.program_id(2) == 0)
    def _():
        c_scr[...] = jnp.zeros_like(c_scr)
        n_scr[...] = jnp.zeros_like(n_scr)
        m_scr[...] = jnp.zeros_like(m_scr)

    gates = g_ref[...]
    ic = gates[0:1, :] + bi_ref[h]
    fpre = gates[1:2, :] + bf_ref[h]
    fc = -_softplus(-fpre)

    ri = lax.broadcasted_iota(I32, (L, L), 0)
    ci = lax.broadcasted_iota(I32, (L, L), 1)
    causal = ci <= ri
    eye = ci == ri
    bcum_col = jnp.sum(jnp.where(causal, fc, 0.0), axis=1, keepdims=True)
    fc_col = jnp.sum(jnp.where(eye, fc, 0.0), axis=1, keepdims=True)
    bcum_row = jnp.sum(jnp.where(ri <= ci, fc_col, 0.0), axis=0, keepdims=True)

    m_st = m_scr[...]
    d = jnp.where(causal, bcum_col - bcum_row + ic, -jnp.inf)
    inter = bcum_col + m_st
    m_row = jnp.maximum(inter, jnp.max(d, axis=1, keepdims=True))
    w_inter = jnp.exp(inter - m_row)

    q = q_ref[...]
    k = k_ref[...] * (ML_DK ** -0.5)
    v = v_ref[...]
    qb = q.astype(BF16)
    vb = v.astype(BF16)
    p = jnp.exp(d - m_row) * _bdot_nt(qb, k)
    c_st = c_scr[...]
    n_st = n_scr[...]
    num = w_inter * _bdot(qb, c_st) + _bdot(p, vb)
    den = w_inter * jnp.sum(q * n_st, axis=1, keepdims=True) + jnp.sum(p, axis=1, keepdims=True)
    hout = num / jnp.maximum(jnp.abs(den), jnp.exp(-m_row))

    b_last = jnp.sum(fc, axis=1, keepdims=True)
    g_row = b_last - bcum_row + ic
    m_new = jnp.maximum(b_last + m_st, jnp.max(g_row, axis=1, keepdims=True))
    w_old = jnp.exp(b_last + m_st - m_new)
    w_in_col = jnp.sum(jnp.where(eye, jnp.exp(g_row - m_new), 0.0), axis=1, keepdims=True)
    kw = k * w_in_col
    c_scr[...] = w_old * c_st + _bdot_tn(kw, vb)
    n_scr[...] = w_old * n_st + jnp.sum(kw, axis=0, keepdims=True)
    m_scr[...] = m_new

    hn = hout * lax.rsqrt(jnp.mean(hout * hout, axis=-1, keepdims=True) + EPS) * gain_ref[...]
    y_ref[...] = (hn * _sigmoid(op_ref[...]) * _silu(z_ref[...])).astype(y_ref.dtype)


def _mlstm(proj, gates, b_i, b_f, gain, bsz, t):
    L = min(ML_CHUNK, t)
    nc = t // L
    row = lambda b, h, c: b * nc + c
    return pl.pallas_call(
        _mlstm_body,
        grid=(bsz, ML_HEADS, nc),
        in_specs=[
            pl.BlockSpec(memory_space=pltpu.SMEM),
            pl.BlockSpec(memory_space=pltpu.SMEM),
            pl.BlockSpec((L, ML_DK), lambda b, h, c: (row(b, h, c), h)),
            pl.BlockSpec((L, ML_DK), lambda b, h, c: (row(b, h, c), ML_HEADS + h)),
            pl.BlockSpec((L, ML_DV), lambda b, h, c: (row(b, h, c), ML_HEADS + h)),
            pl.BlockSpec((L, ML_DV), lambda b, h, c: (row(b, h, c), 2 * ML_HEADS + h)),
            pl.BlockSpec((L, ML_DV), lambda b, h, c: (row(b, h, c), 3 * ML_HEADS + h)),
            pl.BlockSpec((None, None, None, 2, L), lambda b, h, c: (b, h, c, 0, 0)),
            pl.BlockSpec((None, 1, ML_DV), lambda b, h, c: (h, 0, 0)),
        ],
        out_specs=pl.BlockSpec((L, ML_DV), lambda b, h, c: (row(b, h, c), h)),
        out_shape=jax.ShapeDtypeStruct((bsz * t, ML_HEADS * ML_DV), BF16),
        scratch_shapes=[pltpu.VMEM((ML_DK, ML_DV), F32), pltpu.VMEM((1, ML_DK), F32), pltpu.VMEM((1, 1), F32)],
        compiler_params=_cparams(("parallel", "parallel", "arbitrary"), 40),
        name="mlstm",
    )(b_i.astype(F32), b_f.astype(F32), proj, proj, proj, proj, proj, gates,
      gain.reshape(ML_HEADS, 1, ML_DV).astype(F32))


def _causal_conv(x, tail_ref, w):
    L = x.shape[0]
    xx = jnp.concatenate([tail_ref[...], x], axis=0)
    y = w[CONV_WIDTH - 1:CONV_WIDTH, :] * x
    for j in range(CONV_WIDTH - 1):
        y = y + w[j:j + 1, :] * pltpu.roll(xx, CONV_WIDTH - 1 - j, axis=0)[8:, :]
    tail_ref[...] = x[L - 8:, :]
    return y


def _head_expand(n_heads, width):
    r = lax.broadcasted_iota(I32, (n_heads, n_heads * width), 0)
    c = lax.broadcasted_iota(I32, (n_heads, n_heads * width), 1)
    return ((c >= r * width) & (c < (r + 1) * width)).astype(F32)


def _ssd_body(x_ref, bm_ref, cm_ref, z_ref, dtc_ref, dtr_ref, wx_ref, wb_ref, wc_ref, bx_ref, bb_ref, bc_ref,
              dtbc_ref, dtbr_ref, alc_ref, alr_ref, dsk_ref, gain_ref, y_ref, st_scr, tx_scr, tb_scr, tc_scr):
    L = x_ref.shape[0]
    nh, hd = SSM_HEADS_PER_GROUP, SSM_HEAD_DIM

    @pl.when(pl.program_id(2) == 0)
    def _():
        st_scr[...] = jnp.zeros_like(st_scr)
        tx_scr[...] = jnp.zeros_like(tx_scr)
        tb_scr[...] = jnp.zeros_like(tb_scr)
        tc_scr[...] = jnp.zeros_like(tc_scr)

    xc = _silu(_causal_conv(x_ref[...], tx_scr, wx_ref[...]) + bx_ref[...])
    bc = _silu(_causal_conv(bm_ref[...], tb_scr, wb_ref[...]) + bb_ref[...])
    cc = _silu(_causal_conv(cm_ref[...], tc_scr, wc_ref[...]) + bc_ref[...])

    ri = lax.broadcasted_iota(I32, (L, L), 0)
    ci = lax.broadcasted_iota(I32, (L, L), 1)
    causal = ci <= ri
    dt_col = _softplus(dtc_ref[...] + dtbc_ref[...])
    acum = _fdot(causal.astype(F32), dt_col * (-jnp.exp(alc_ref[...])))
    dt_row = _softplus(dtr_ref[...] + dtbr_ref[...])
    acum_row = _fdot(dt_row * (-jnp.exp(alr_ref[...])), (ri <= ci).astype(F32))

    expand = _head_expand(nh, hd)
    acum_e = _fdot(acum, expand)
    xdt = xc * _fdot(dt_col, expand)
    xdt_b = xdt.astype(BF16)
    cb = _bdot_nt(cc, bc)

    lane = lax.broadcasted_iota(I32, (L, 2 * hd), 1)
    parts = []
    for hp in range(nh // 2):
        xpair = xdt_b[:, 2 * hd * hp:2 * hd * (hp + 1)]
        acc = None
        for e in range(2):
            hh = 2 * hp + e
            seg = acum[:, hh:hh + 1] - acum_row[hh:hh + 1, :]
            dec = jnp.exp(jnp.where(causal, seg, -jnp.inf))
            xsel = jnp.where((lane >= hd * e) & (lane < hd * (e + 1)), xpair, jnp.zeros_like(xpair))
            t = _bdot(cb * dec, xsel)
            acc = t if acc is None else acc + t
        parts.append(acc)
    y = jnp.concatenate(parts, axis=1)

    st = st_scr[...]
    y = y + _bdot(cc, st) * jnp.exp(acum_e)
    a_last = acum_e[L - 1:L, :]
    st_scr[...] = jnp.exp(a_last) * st + _bdot_tn(bc, xdt * jnp.exp(a_last - acum_e))

    y = y + xc * dsk_ref[...]
    y = y * _silu(z_ref[...])
    y = y * lax.rsqrt(jnp.mean(y * y, axis=-1, keepdims=True) + EPS) * gain_ref[...]
    y_ref[...] = y.astype(y_ref.dtype)


def _ssd(proj, dt_col, dt_row, conv_w, conv_b, dt_bias, a_log, d_skip, gain, bsz, t):
    L = min(SSD_CHUNK, t)
    nc = t // L
    G, nh, gw, ns = SSM_GROUPS, SSM_HEADS_PER_GROUP, SSM_GROUP_WIDTH, SSM_STATE
    row = lambda b, g, c: b * nc + c
    x_blk = 10240 // gw
    z_blk = 8192 // gw
    b_blk = 12288 // ns
    c_blk = 12800 // ns
    wb_blk = G * gw // ns
    conv_b2 = conv_b.reshape(1, -1).astype(F32)
    f = lambda a: a.astype(F32)
    return pl.pallas_call(
        _ssd_body,
        grid=(bsz, G, nc),
        in_specs=[
            pl.BlockSpec((L, gw), lambda b, g, c: (row(b, g, c), x_blk + g)),
            pl.BlockSpec((L, ns), lambda b, g, c: (row(b, g, c), b_blk + g)),
            pl.BlockSpec((L, ns), lambda b, g, c: (row(b, g, c), c_blk + g)),
            pl.BlockSpec((L, gw), lambda b, g, c: (row(b, g, c), z_blk + g)),
            pl.BlockSpec((None, None, L, nh), lambda b, g, c: (b, g, c, 0)),
            pl.BlockSpec((None, None, nh, L), lambda b, g, c: (b, g, 0, c)),
            pl.BlockSpec((CONV_WIDTH, gw), lambda b, g, c: (0, g)),
            pl.BlockSpec((CONV_WIDTH, ns), lambda b, g, c: (0, wb_blk + g)),
            pl.BlockSpec((CONV_WIDTH, ns), lambda b, g, c: (0, wb_blk + G + g)),
            pl.BlockSpec((1, gw), lambda b, g, c: (0, g)),
            pl.BlockSpec((1, ns), lambda b, g, c: (0, wb_blk + g)),
            pl.BlockSpec((1, ns), lambda b, g, c: (0, wb_blk + G + g)),
            pl.BlockSpec((None, 1, nh), lambda b, g, c: (g, 0, 0)),
            pl.BlockSpec((None, nh, 1), lambda b, g, c: (g, 0, 0)),
            pl.BlockSpec((None, 1, nh), lambda b, g, c: (g, 0, 0)),
            pl.BlockSpec((None, nh, 1), lambda b, g, c: (g, 0, 0)),
            pl.BlockSpec((None, 1, gw), lambda b, g, c: (g, 0, 0)),
            pl.BlockSpec((None, 1, gw), lambda b, g, c: (g, 0, 0)),
        ],
        out_specs=pl.BlockSpec((L, gw), lambda b, g, c: (row(b, g, c), g)),
        out_shape=jax.ShapeDtypeStruct((bsz * t, G * gw), BF16),
        scratch_shapes=[pltpu.VMEM((ns, gw), F32), pltpu.VMEM((8, gw), F32),
                        pltpu.VMEM((8, ns), F32), pltpu.VMEM((8, ns), F32)],
        compiler_params=_cparams(("parallel", "parallel", "arbitrary"), 40),
        name="ssd",
    )(proj, proj, proj, proj, dt_col, dt_row, f(conv_w), f(conv_w), f(conv_w), conv_b2, conv_b2, conv_b2,
      f(dt_bias).reshape(G, 1, nh), f(dt_bias).reshape(G, nh, 1), f(a_log).reshape(G, 1, nh),
      f(a_log).reshape(G, nh, 1), jnp.repeat(f(d_skip), SSM_HEAD_DIM).reshape(G, 1, gw), f(gain).reshape(G, 1, gw))


def _layer_ab(x2, bsz, t, norm_g, w_in, ml_b_i, ml_b_f, ml_norm, ssm_conv_w, ssm_conv_b, ssm_dt_bias,
              ssm_a_log, ssm_d, ssm_norm, w_out):
    m = bsz * t
    small = jnp.concatenate([w_in[:, 8192:8200], w_in[:, 13320:13352]], axis=1)
    w_main = jnp.concatenate(
        [w_in[:, :8192], w_in[:, 8200:13320], jnp.pad(small, ((0, 0), (0, 512 - small.shape[1])))], axis=1
    ).astype(BF16)
    proj = _norm_matmul(x2, 0, D_MODEL, norm_g, w_main, out_dtype=F32, tn=1536, name="proj_ab")
    gate_cols = proj[:, 13312:13352]
    ml_gates = gate_cols[:, :8].reshape(bsz, t // min(ML_CHUNK, t), min(ML_CHUNK, t), 2, ML_HEADS)
    ml_gates = jnp.transpose(ml_gates, (0, 4, 1, 3, 2))
    dt_raw = gate_cols[:, 8:40].reshape(bsz, t, SSM_GROUPS, SSM_HEADS_PER_GROUP)
    dt_col = jnp.transpose(dt_raw, (0, 2, 1, 3))
    dt_row = jnp.transpose(dt_raw, (0, 2, 3, 1))
    y_a = _mlstm(proj, ml_gates, ml_b_i, ml_b_f, ml_norm, bsz, t)
    y_b = _ssd(proj, dt_col, dt_row, ssm_conv_w, ssm_conv_b, ssm_dt_bias, ssm_a_log, ssm_d, ssm_norm, bsz, t)
    return _out_proj([y_a, y_b], w_out.astype(BF16), x2, jnp.ones((D_MODEL,), F32), final_norm=False,
                     name="out_ab")


def _indexer_body(iq_ref, iw_ref, ik_ref, mask_ref, key_scr, hi_scr, lo_scr, pos_scr, *, top_k):
    n_kt = key_scr.shape[0]
    i = pl.program_id(1)
    start = i * IDX_Q
    n_tiles = (start + IDX_Q + IDX_KT - 1) // IDX_KT
    iqb = iq_ref[...].astype(BF16)
    iw = iw_ref[...] * (IDX_HEADS ** -0.5 * IDX_DIM ** -0.5)
    qpos = start + lax.broadcasted_iota(I32, (IDX_Q, 1), 0)
    limit = (lax.shift_right_logical(qpos, int(math.log2(DSA_CHUNK))) + 1) * DSA_CHUNK
    zeros_k = jnp.zeros((IDX_DIM, IDX_KT), BF16)
    kiota = lax.broadcasted_iota(I32, (1, IDX_KT), 1)

    def score_tile(tt, carry):
        ikt = ik_ref[tt]
        ik_even = jnp.concatenate([ikt, zeros_k], axis=0)
        ik_odd = jnp.concatenate([zeros_k, ikt], axis=0)
        acc = jnp.zeros((IDX_Q, IDX_KT), F32)
        for p in range(IDX_HEADS // 2):
            xq = iqb[:, 2 * IDX_DIM * p:2 * IDX_DIM * (p + 1)]
            s0 = jnp.dot(xq, ik_even, preferred_element_type=F32)
            s1 = jnp.dot(xq, ik_odd, preferred_element_type=F32)
            acc = acc + iw[:, 2 * p:2 * p + 1] * jnp.maximum(s0, 0.0)
            acc = acc + iw[:, 2 * p + 1:2 * p + 2] * jnp.maximum(s1, 0.0)
        bits = lax.bitcast_convert_type(acc + 0.0, I32)
        key = jnp.where(bits < 0, bits ^ jnp.int32(0x7FFFFFFF), bits)
        key = jnp.where(tt * IDX_KT + kiota < limit, key, jnp.int32(INT_MIN))
        key_scr[tt] = key
        hi_scr[tt] = lax.shift_right_arithmetic(key, 16).astype(I16)
        lo_scr[tt] = ((key & jnp.int32(0xFFFF)) - jnp.int32(HALF16)).astype(I16)
        return carry

    lax.fori_loop(0, n_tiles, score_tile, 0)

    def count(pred):
        def body(tt, acc):
            w = jnp.where(pred(key_scr[tt], tt * IDX_KT + kiota), 1.0, 0.0)
            for s in range(IDX_KT // LANES):
                acc = acc + w[:, s * LANES:(s + 1) * LANES]
            return acc
        acc = lax.fori_loop(0, n_tiles, body, jnp.zeros((IDX_Q, LANES), F32))
        return jnp.sum(acc, axis=1, keepdims=True)

    def count16(ref, pred):
        def body(tt, acc):
            w = jnp.where(pred(ref[tt]), jnp.int16(1), jnp.int16(0))
            for s in range(IDX_KT // LANES):
                acc = acc + w[:, s * LANES:(s + 1) * LANES]
            return acc
        acc = lax.fori_loop(0, n_tiles, body, jnp.zeros((IDX_Q, LANES), I16))
        return jnp.sum(acc.astype(F32), axis=1, keepdims=True)

    def tile16(v):
        return jnp.concatenate([jnp.broadcast_to(v, (IDX_Q, LANES)).astype(I16)] * (IDX_KT // LANES), axis=1)

    def bisect16(ref, n_above):
        t16 = jnp.full((IDX_Q, 1), -HALF16, I32)
        for bit in range(15, -1, -1):
            cand = jnp.zeros_like(t16) if bit == 15 else t16 + jnp.int32(1 << bit)
            ct = tile16(cand)
            t16 = jnp.where(n_above + count16(ref, lambda v, ct=ct: v >= ct) >= top_k, cand, t16)
        return t16

    thr_hi = bisect16(hi_scr, 0.0)
    th = tile16(thr_hi)
    n_hi_gt = count16(hi_scr, lambda v: v > th)

    def keep_lo(tt, carry):
        lo_scr[tt] = jnp.where(hi_scr[tt] == th, lo_scr[tt], jnp.int16(-HALF16))
        return carry

    lax.fori_loop(0, n_tiles, keep_lo, 0)
    thr = thr_hi * jnp.int32(2 * HALF16) + (bisect16(lo_scr, n_hi_gt) + jnp.int32(HALF16))

    n_gt = count(lambda k, pos: k > thr)
    n_ge = count(lambda k, pos: k >= thr)
    room = top_k - n_gt
    pos_scr[...] = jnp.full((IDX_Q, 1), 2 * n_kt * IDX_KT, I32)

    @pl.when(jnp.max(jnp.where(thr > INT_MIN, n_ge, 0.0)) > top_k)
    def _():
        pcut = jnp.zeros((IDX_Q, 1), I32)
        for bit in range(int(math.log2(n_kt * IDX_KT)), -1, -1):
            cand = pcut + jnp.int32(1 << bit)
            n_tie = count(lambda k, pos, cand=cand: (k == thr) & (pos < cand))
            pcut = jnp.where(n_tie <= room, cand, pcut)
        pos_scr[...] = pcut

    pcut = pos_scr[...]

    def write_tile(tt, carry):
        k = key_scr[tt]
        pos = tt * IDX_KT + kiota
        sel = (k > thr) | ((k == thr) & (pos < pcut))
        sel = sel & (k > INT_MIN)
        mask_ref[tt] = jnp.where(sel, 1, 0).astype(jnp.int8)
        return carry

    lax.fori_loop(0, n_tiles, write_tile, 0)

    def zero_tile(tt, carry):
        mask_ref[tt] = jnp.zeros((IDX_Q, IDX_KT), jnp.int8)
        return carry

    lax.fori_loop(n_tiles, n_kt, zero_tile, 0)


def _indexer(proj, iq_blk, iw, ik_tiles, bsz, t):
    nq = t // IDX_Q
    n_kt = t // IDX_KT
    top_k = min(TOPK_MAX, t // 4)
    return pl.pallas_call(
        functools.partial(_indexer_body, top_k=top_k),
        grid=(bsz, nq),
        in_specs=[
            pl.BlockSpec((IDX_Q, IDX_HEADS * IDX_DIM), lambda b, i: (b * nq + i, iq_blk)),
            pl.BlockSpec((IDX_Q, IDX_HEADS), lambda b, i: (b * nq + i, 0)),
            pl.BlockSpec((None, n_kt, IDX_DIM, IDX_KT), lambda b, i: (b, 0, 0, 0)),
        ],
        out_specs=pl.BlockSpec((None, n_kt, IDX_Q, IDX_KT), lambda b, i: (b * nq + i, 0, 0, 0)),
        out_shape=jax.ShapeDtypeStruct((bsz * nq, n_kt, IDX_Q, IDX_KT), jnp.int8),
        scratch_shapes=[pltpu.VMEM((n_kt, IDX_Q, IDX_KT), I32), pltpu.VMEM((n_kt, IDX_Q, IDX_KT), I16),
                        pltpu.VMEM((n_kt, IDX_Q, IDX_KT), I16), pltpu.VMEM((IDX_Q, 1), I32)],
        compiler_params=_cparams(("parallel", "arbitrary"), 40),
        name="dsa_indexer",
    )(proj, iw, ik_tiles)


NEG_BIG = -3.0e38
LOG2_E = math.log2(math.e)


def _attn_body(q_ref, k_ref, v_ref, mask_ref, z_ref, tb_ref, o_ref, acc_scr, m_scr):
    T = ATT_T
    D = ATT_HEAD_DIM
    nsub = T // ATT_SUB
    i = pl.program_id(1)
    j = pl.program_id(2)

    @pl.when(j == 0)
    def _():
        acc_scr[...] = jnp.zeros_like(acc_scr)
        m_scr[...] = jnp.full_like(m_scr, NEG_BIG)

    def tile_step(near):
        sel = jnp.concatenate([mask_ref[a, 0].astype(F32) for a in range(mask_ref.shape[0])], axis=0)
        mask_bias = (sel - 1.0) * (-NEG_BIG)
        ones = jnp.ones((T, D), BF16)

        def pair_body(p, carry):
            q2, k2, v2 = q_ref[p], k_ref[p], v_ref[p]
            m_old = [m_scr[2 * p + e] for e in range(2)]
            a_old = [acc_scr[2 * p + e] for e in range(2)]
            new = []
            logits = [_bdot_nt(q2[:, D * e:D * (e + 1)], k2[:, D * e:D * (e + 1)]) for e in range(2)]
            for e in range(2):
                s = logits[e] + mask_bias
                if near is not None:
                    tb = tb_ref[2 * p + e]
                    zero = jnp.zeros((ATT_SUB, ATT_SUB), F32)
                    rows = []
                    for a in range(nsub):
                        cols = []
                        for bb in range(nsub):
                            rel_blocks = (bb - a) if near == "diag" else (bb - a - nsub)
                            cols.append(tb[0] if rel_blocks == 0 else tb[1] if rel_blocks == -1 else zero)
                        rows.append(jnp.concatenate(cols, axis=1))
                    s = s + jnp.concatenate(rows, axis=0)
                m_new = jnp.maximum(m_old[e], jnp.max(s, axis=1, keepdims=True))
                alpha = jnp.exp2(m_old[e] - m_new)
                pr = jnp.exp2(s - jnp.concatenate([m_new] * (T // LANES), axis=1))
                v_aug = jnp.concatenate([v2[:, D * e:D * (e + 1)], ones], axis=1)
                a_new = jnp.concatenate([alpha] * (2 * D // LANES), axis=1) * a_old[e] + _bdot(pr, v_aug)
                new.append((m_new, a_new))
            for e in range(2):
                m_scr[2 * p + e] = new[e][0]
                acc_scr[2 * p + e] = new[e][1]
            return carry

        lax.fori_loop(0, ATT_HEADS // 2, pair_body, 0, unroll=4)

    @pl.when(j < i - 1)
    def _():
        tile_step(None)

    @pl.when(j == i - 1)
    def _():
        tile_step("prev")

    @pl.when(j == i)
    def _():
        tile_step("diag")
        for p in range(ATT_HEADS // 2):
            for e in range(2):
                hh = 2 * p + e
                a = acc_scr[hh]
                o = a[:, :D] / a[:, D:]
                o = o * _silu(z_ref[p, :, D * e:D * (e + 1)].astype(F32))
                o_ref[:, D * hh:D * (hh + 1)] = o.astype(o_ref.dtype)


def _attention(qz, kv, mask, tb, bsz, t):
    T = ATT_T
    nq = t // T
    npair = ATT_HEADS // 2
    pw = 2 * ATT_HEAD_DIM
    nrow = T // IDX_Q
    ncol = T // IDX_KT
    return pl.pallas_call(
        _attn_body,
        grid=(bsz, nq, nq),
        in_specs=[
            pl.BlockSpec((npair, T, pw), lambda b, i, j: (0, b * nq + i, 0)),
            pl.BlockSpec((npair, T, pw), lambda b, i, j: (0, b * nq + jnp.minimum(j, i), 0)),
            pl.BlockSpec((npair, T, pw), lambda b, i, j: (1, b * nq + jnp.minimum(j, i), 0)),
            pl.BlockSpec((nrow, ncol, IDX_Q, IDX_KT), lambda b, i, j: (b * nq + i, jnp.minimum(j, i), 0, 0)),
            pl.BlockSpec((npair, T, pw), lambda b, i, j: (1, b * nq + i, 0)),
            pl.BlockSpec((ATT_HEADS, 2, ATT_SUB, ATT_SUB), lambda b, i, j: (0, 0, 0, 0)),
        ],
        out_specs=pl.BlockSpec((T, ATT_HEADS * ATT_HEAD_DIM), lambda b, i, j: (b * nq + i, 0)),
        out_shape=jax.ShapeDtypeStruct((bsz * t, ATT_HEADS * ATT_HEAD_DIM), BF16),
        scratch_shapes=[pltpu.VMEM((ATT_HEADS, T, 2 * ATT_HEAD_DIM), F32), pltpu.VMEM((ATT_HEADS, T, LANES), F32)],
        compiler_params=_cparams(("parallel", "parallel", "arbitrary"), 56),
        name="dsa_attention",
    )(qz, kv, kv, mask, qz, tb)


def _t5_bucket(rel):
    nb = REL_BUCKETS // 2
    max_exact = nb // 2
    ret = jnp.where(rel > 0, nb, 0)
    n = jnp.abs(rel)
    nf = jnp.maximum(n, max_exact).astype(F32)
    large = max_exact + (jnp.log(nf / max_exact) / math.log(REL_MAX_DIST / max_exact)
                         * (nb - max_exact)).astype(jnp.int32)
    large = jnp.minimum(large, nb - 1)
    return ret + jnp.where(n < max_exact, n, large)


def _rel_bias_blocks(rel_bias):
    r = jnp.arange(ATT_SUB)[:, None]
    c = jnp.arange(ATT_SUB)[None, :]
    rel = jnp.stack([c - r, c - r - ATT_SUB])
    assert ATT_SUB >= REL_MAX_DIST and ATT_SUB % DSA_CHUNK == 0
    rb = rel_bias.astype(F32)
    far = rb[_t5_bucket(jnp.int32(-2 * ATT_SUB))]
    onehot = (_t5_bucket(rel)[..., None] == jnp.arange(REL_BUCKETS)).astype(F32)
    table = jnp.einsum("vrcb,bh->hvrc", onehot, rb - far, precision=HI)
    return table * LOG2_E


GDN_PREP_ROWS = 256
GDN_SOLVE_LANES = 256


def _conv_with_prev(x, prev, w):
    xx = jnp.concatenate([prev, x], axis=0)
    y = w[CONV_WIDTH - 1:CONV_WIDTH, :] * x
    for j in range(CONV_WIDTH - 1):
        y = y + w[j:j + 1, :] * pltpu.roll(xx, CONV_WIDTH - 1 - j, axis=0)[8:, :]
    return y


def _gdn_prep_body(v_ref, q_ref, k_ref, pv_ref, pq_ref, pk_ref, bc_ref, ac_ref, ar_ref, wq_ref, wk_ref, wv_ref,
                   dtbc_ref, dtbr_ref, alc_ref, alr_ref,
                   qn_ref, rhs_ref, lu_ref, a_ref, egc_ref, egl_ref):
    L = GDN_CHUNK
    D = GDN_HEAD_DIM
    nv = GDN_V_HEADS // GDN_GROUPS
    rep = GDN_V_HEADS // GDN_QK_HEADS
    seq_start = pl.program_id(2) == 0

    def conv(x_ref, p_ref, w_ref):
        prev = jnp.where(seq_start, 0.0, p_ref[...])
        return _silu(_conv_with_prev(x_ref[...], prev, w_ref[...]))

    qc = conv(q_ref, pq_ref, wq_ref)
    kc = conv(k_ref, pk_ref, wk_ref)
    vc = conv(v_ref, pv_ref, wv_ref)
    qn, kn = [], []
    for kh in range(nv // rep):
        qh = qc[:, D * kh:D * (kh + 1)]
        kk_ = kc[:, D * kh:D * (kh + 1)]
        qh = qh * lax.rsqrt(jnp.sum(qh * qh, axis=-1, keepdims=True) + EPS) * (D ** -0.5)
        kn.append(kk_ * lax.rsqrt(jnp.sum(kk_ * kk_, axis=-1, keepdims=True) + EPS))
        qn.append(qh.astype(BF16))
        qn_ref[:, D * kh:D * (kh + 1)] = qn[kh]

    beta = _sigmoid(bc_ref[...])
    g_col = -jnp.exp(alc_ref[...]) * _softplus(ac_ref[...] + dtbc_ref[...])
    ri = lax.broadcasted_iota(I32, (L, L), 0)
    ci = lax.broadcasted_iota(I32, (L, L), 1)
    causal = ci <= ri
    strict = ci < ri
    for cc in range(a_ref.shape[0]):
        rows = slice(L * cc, L * (cc + 1))
        gc_col = _fdot(causal.astype(F32), g_col[rows, :])
        g_row = -jnp.exp(alr_ref[...]) * _softplus(ar_ref[cc] + dtbr_ref[...])
        gc_row = _fdot(g_row, (ri <= ci).astype(F32))
        eg_col = jnp.exp(gc_col)
        egc_ref[rows, :] = eg_col
        for kh in range(nv // rep):
            kf = kn[kh][rows, :]
            kb = kf.astype(BF16)
            kk = _bdot_nt(kb, kb)
            qk = _bdot_nt(qn[kh][rows, :], kb)
            for e in range(rep):
                vh = rep * kh + e
                gcol = gc_col[:, vh:vh + 1]
                bcol = beta[rows, vh:vh + 1]
                dec = jnp.exp(jnp.where(causal, gcol - gc_row[vh:vh + 1, :], -jnp.inf))
                a_ref[cc, vh] = jnp.where(strict, bcol * kk * dec, 0.0)
                g_last = gcol[L - 1:L, :]
                kw_t = jnp.transpose(kf * jnp.exp(g_last - gcol))
                lu_ref[cc, vh] = jnp.concatenate([qk * dec, kw_t], axis=0).astype(lu_ref.dtype)
                egl_ref[cc, vh:vh + 1, :] = jnp.broadcast_to(jnp.exp(g_last), (1, LANES))
                rhs_ref[rows, 2 * D * vh:2 * D * (vh + 1)] = jnp.concatenate(
                    [vc[rows, D * vh:D * (vh + 1)] * bcol, kf * (bcol * eg_col[:, vh:vh + 1])], axis=1
                ).astype(rhs_ref.dtype)


def _gdn_solve_body(a_ref, t_ref, t_scr):
    L, _, PB = a_ref.shape
    UN = 8
    sub = lax.broadcasted_iota(I32, (L, PB), 0)
    t_scr[...] = jnp.zeros_like(t_scr)
    t_scr[0] = jnp.where(sub == 0, 1.0, 0.0)

    def row_body(i, carry):
        def col_block(jb, acc):
            j0 = pl.multiple_of(jb * UN, UN)
            a_blk = a_ref[i, pl.ds(j0, UN), :]
            for jj in range(UN):
                acc = acc + a_blk[jj:jj + 1, :] * t_scr[j0 + jj]
            return acc

        acc = lax.fori_loop(0, (i + UN - 1) // UN, col_block, jnp.zeros((L, PB), F32))
        t_scr[i] = jnp.where(sub == i, 1.0, 0.0) - acc
        return carry

    lax.fori_loop(1, L, row_body, 0)
    t_ref[...] = t_scr[...].astype(t_ref.dtype)


def _gdn_scan_body(qn_ref, rhs_ref, lu_ref, t_ref, z_ref, egc_ref, egl_ref, gain_ref, y_ref, s_scr):
    L = qn_ref.shape[0]
    D = GDN_HEAD_DIM
    nv = GDN_V_HEADS // GDN_GROUPS
    rep = GDN_V_HEADS // GDN_QK_HEADS

    @pl.when(pl.program_id(2) == 0)
    def _():
        s_scr[...] = jnp.zeros_like(s_scr)

    eg_col = egc_ref[...]
    z = z_ref[...]
    heads = range(nv)
    sol = [_bdot(t_ref[vh], rhs_ref[:, 2 * D * vh:2 * D * (vh + 1)]) for vh in heads]
    s_st = [s_scr[vh] for vh in heads]
    r_s = [_bdot(jnp.concatenate([sol[vh][:, D:].astype(BF16),
                                  qn_ref[:, D * (vh // rep):D * (vh // rep + 1)]], axis=0), s_st[vh])
           for vh in heads]
    u = [sol[vh][:, :D] - r_s[vh][:L] for vh in heads]
    r_u = [_bdot(lu_ref[vh], u[vh]) for vh in heads]
    for vh in heads:
        s_scr[vh] = egl_ref[vh:vh + 1, :] * s_st[vh] + r_u[vh][L:]
        o = r_s[vh][L:] * eg_col[:, vh:vh + 1] + r_u[vh][:L]
        o = o * lax.rsqrt(jnp.mean(o * o, axis=-1, keepdims=True) + EPS) * gain_ref[...]
        y_ref[:, D * vh:D * (vh + 1)] = (o * _silu(z[:, D * vh:D * (vh + 1)])).astype(y_ref.dtype)


def _gdn(proj, b_col, a_col, a_row, conv_w, dt_bias, a_log, gain, bsz, t):
    L = GDN_CHUNK
    nc = t // L
    G = GDN_GROUPS
    nv = GDN_V_HEADS // G
    qw = GDN_QK_HEADS // G * GDN_HEAD_DIM
    vw = nv * GDN_HEAD_DIM
    m = bsz * t
    z_blk, q_blk, k_blk = 4096 // vw, 8192 // qw, 10240 // qw
    f = lambda a: a.astype(F32)
    cw = f(conv_w)
    pc_spec = pl.BlockSpec((None, 1, nv), lambda b, g, c: (g, 0, 0))
    pr_spec = pl.BlockSpec((None, nv, 1), lambda b, g, c: (g, 0, 0))
    wq_spec = pl.BlockSpec((CONV_WIDTH, qw), lambda b, g, c: (0, g))
    wk_spec = pl.BlockSpec((CONV_WIDTH, qw), lambda b, g, c: (0, G + g))
    wv_spec = pl.BlockSpec((CONV_WIDTH, vw), lambda b, g, c: (0, (2 * G * qw) // vw + g))

    LB = min(GDN_PREP_ROWS, t)
    ncb = LB // L
    nb = t // LB
    rowb = lambda b, g, c: b * nb + c
    prev8 = lambda b, g, c: jnp.maximum((b * t + c * LB) // 8 - 1, 0)
    D = GDN_HEAD_DIM
    qn, rhs, lu, a_mat, egc, egl = pl.pallas_call(
        _gdn_prep_body,
        grid=(bsz, G, nb),
        in_specs=[
            pl.BlockSpec((LB, vw), lambda b, g, c: (rowb(b, g, c), g)),
            pl.BlockSpec((LB, qw), lambda b, g, c: (rowb(b, g, c), q_blk + g)),
            pl.BlockSpec((LB, qw), lambda b, g, c: (rowb(b, g, c), k_blk + g)),
            pl.BlockSpec((8, vw), lambda b, g, c: (prev8(b, g, c), g)),
            pl.BlockSpec((8, qw), lambda b, g, c: (prev8(b, g, c), q_blk + g)),
            pl.BlockSpec((8, qw), lambda b, g, c: (prev8(b, g, c), k_blk + g)),
            pl.BlockSpec((None, None, LB, nv), lambda b, g, c: (b, g, c, 0)),
            pl.BlockSpec((None, None, LB, nv), lambda b, g, c: (b, g, c, 0)),
            pl.BlockSpec((None, None, ncb, nv, L), lambda b, g, c: (b, g, c, 0, 0)),
            wq_spec, wk_spec, wv_spec, pc_spec, pr_spec, pc_spec, pr_spec,
        ],
        out_specs=[
            pl.BlockSpec((LB, qw), lambda b, g, c: (rowb(b, g, c), g)),
            pl.BlockSpec((LB, 2 * vw), lambda b, g, c: (rowb(b, g, c), g)),
            pl.BlockSpec((None, None, ncb, nv, L + D, L), lambda b, g, c: (b, g, c, 0, 0, 0)),
            pl.BlockSpec((None, None, ncb, nv, L, L), lambda b, g, c: (b, g, c, 0, 0, 0)),
            pl.BlockSpec((None, None, LB, nv), lambda b, g, c: (b, g, c, 0)),
            pl.BlockSpec((None, None, ncb, nv, LANES), lambda b, g, c: (b, g, c, 0, 0)),
        ],
        out_shape=[
            jax.ShapeDtypeStruct((m, G * qw), BF16),
            jax.ShapeDtypeStruct((m, 2 * G * vw), BF16),
            jax.ShapeDtypeStruct((bsz, G, nc, nv, L + D, L), BF16),
            jax.ShapeDtypeStruct((bsz, G, nc, nv, L, L), F32),
            jax.ShapeDtypeStruct((bsz, G, t, nv), F32),
            jax.ShapeDtypeStruct((bsz, G, nc, nv, LANES), F32),
        ],
        compiler_params=_cparams(("parallel", "parallel", "parallel"), 48),
        name="gdn_prep",
    )(proj, proj, proj, proj, proj, proj, b_col, a_col, a_row, cw, cw, cw,
      f(dt_bias).reshape(G, 1, nv), f(dt_bias).reshape(G, nv, 1), f(a_log).reshape(G, 1, nv),
      f(a_log).reshape(G, nv, 1))

    n_prob = bsz * G * nc * nv
    pb = min(GDN_SOLVE_LANES, n_prob)
    a_t = jnp.transpose(a_mat.reshape(n_prob, L, L), (1, 2, 0))
    t_t = pl.pallas_call(
        _gdn_solve_body,
        grid=(n_prob // pb,),
        in_specs=[pl.BlockSpec((L, L, pb), lambda p: (0, 0, p))],
        out_specs=pl.BlockSpec((L, L, pb), lambda p: (0, 0, p)),
        out_shape=jax.ShapeDtypeStruct((L, L, n_prob), BF16),
        scratch_shapes=[pltpu.VMEM((L, L, pb), F32)],
        compiler_params=_cparams(("parallel",), 40),
        name="gdn_solve",
    )(a_t)
    t_mat = jnp.transpose(t_t, (2, 0, 1)).reshape(bsz, G, nc, nv, L, L)

    row = lambda b, g, c: b * nc + c
    return pl.pallas_call(
        _gdn_scan_body,
        grid=(bsz, G, nc),
        in_specs=[
            pl.BlockSpec((L, qw), lambda b, g, c: (row(b, g, c), g)),
            pl.BlockSpec((L, 2 * vw), lambda b, g, c: (row(b, g, c), g)),
            pl.BlockSpec((None, None, None, nv, L + D, L), lambda b, g, c: (b, g, c, 0, 0, 0)),
            pl.BlockSpec((None, None, None, nv, L, L), lambda b, g, c: (b, g, c, 0, 0, 0)),
            pl.BlockSpec((L, vw), lambda b, g, c: (row(b, g, c), z_blk + g)),
            pl.BlockSpec((None, None, L, nv), lambda b, g, c: (b, g, c, 0)),
            pl.BlockSpec((None, None, None, nv, LANES), lambda b, g, c: (b, g, c, 0, 0)),
            pl.BlockSpec((1, GDN_HEAD_DIM), lambda b, g, c: (0, 0)),
        ],
        out_specs=pl.BlockSpec((L, vw), lambda b, g, c: (row(b, g, c), g)),
        out_shape=jax.ShapeDtypeStruct((m, GDN_V_HEADS * GDN_HEAD_DIM), BF16),
        scratch_shapes=[pltpu.VMEM((nv, GDN_HEAD_DIM, GDN_HEAD_DIM), F32)],
        compiler_params=_cparams(("parallel", "parallel", "arbitrary"), 40),
        name="gdn_scan",
    )(qn, rhs, lu, t_mat, proj, egc, egl, f(gain).reshape(1, GDN_HEAD_DIM))


def _layer_cd(x2, bsz, t, norm_g, w_in, kv_norm, w_uk, w_uv, gdn_conv_w, gdn_dt_bias, gdn_a_log, gdn_norm,
              w_out, rel_bias, final_g, final_norm):
    m = bsz * t
    small = jnp.concatenate([w_in[:, 5632:5712], w_in[:, 18000:18064]], axis=1)
    w_main = jnp.concatenate(
        [w_in[:, 9808:13904], w_in[:, 13904:18000], w_in[:, 5712:9808], w_in[:, 4608:5632], w_in[:, 2048:2560],
         jnp.pad(small, ((0, 0), (0, 512 - small.shape[1])))], axis=1).astype(BF16)
    proj = _norm_matmul(x2, 0, D_MODEL, norm_g, w_main, out_dtype=F32, tn=1024, name="proj_cd")
    w_attn = jnp.concatenate([w_in[:, :2048] * (ATT_HEAD_DIM ** -0.5 * LOG2_E), w_in[:, 2560:4608]],
                             axis=1).astype(BF16)
    qz = _norm_matmul(x2, 0, D_MODEL, norm_g, w_attn, out_dtype=BF16, blocked_width=2 * ATT_HEAD_DIM,
                      tn=1024, name="proj_cd_attn")
    w_kv = jnp.concatenate([w_uk.reshape(KV_RANK, -1), w_uv.reshape(KV_RANK, -1)], axis=1).astype(BF16)
    kv = _norm_matmul(proj, 13312 // KV_RANK, KV_RANK, kv_norm, w_kv, out_dtype=BF16,
                      blocked_width=2 * ATT_HEAD_DIM, tn=1024, name="dsa_kv")

    gate_cols = proj[:, 13824:13824 + 144]
    n_kt = t // IDX_KT
    ik_tiles = jnp.transpose(gate_cols[:, :IDX_DIM].reshape(bsz, n_kt, IDX_KT, IDX_DIM), (0, 1, 3, 2)).astype(BF16)
    iw = gate_cols[:, IDX_DIM:IDX_DIM + IDX_HEADS]
    mask = _indexer(proj, 12288 // (IDX_HEADS * IDX_DIM), iw, ik_tiles, bsz, t)
    y_c = _attention(qz, kv, mask, _rel_bias_blocks(rel_bias), bsz, t)

    nv = GDN_V_HEADS // GDN_GROUPS
    b_pre = gate_cols[:, 80:112].reshape(bsz, t, GDN_GROUPS, nv)
    a_pre = gate_cols[:, 112:144].reshape(bsz, t, GDN_GROUPS, nv)
    col = lambda a: jnp.transpose(a, (0, 2, 1, 3))
    rowf = lambda a: jnp.transpose(a.reshape(bsz, t // GDN_CHUNK, GDN_CHUNK, GDN_GROUPS, nv), (0, 3, 1, 4, 2))
    y_d = _gdn(proj, col(b_pre), col(a_pre), rowf(a_pre), gdn_conv_w, gdn_dt_bias, gdn_a_log, gdn_norm, bsz, t)
    return _out_proj([y_c, y_d], w_out.astype(BF16), x2, final_g, final_norm=final_norm, name="out_cd")


def kernel(x, rel_bias, ab_norm, ab_w_in, ab_ml_b_i, ab_ml_b_f, ab_ml_norm, ab_ssm_conv_w, ab_ssm_conv_b, ab_ssm_dt_bias, ab_ssm_a_log, ab_ssm_d, ab_ssm_norm, ab_w_out, cd_norm, cd_w_in, cd_kv_norm, cd_w_uk, cd_w_uv, cd_gdn_conv_w, cd_gdn_dt_bias, cd_gdn_a_log, cd_gdn_norm, cd_w_out, final_norm):
    bsz, t, d = x.shape
    x2 = x.reshape(bsz * t, d)
    x2 = _layer_ab(x2, bsz, t, ab_norm[0], ab_w_in[0], ab_ml_b_i[0], ab_ml_b_f[0], ab_ml_norm[0],
                   ab_ssm_conv_w[0], ab_ssm_conv_b[0], ab_ssm_dt_bias[0], ab_ssm_a_log[0], ab_ssm_d[0],
                   ab_ssm_norm[0], ab_w_out[0])
    x2 = _layer_cd(x2, bsz, t, cd_norm[0], cd_w_in[0], cd_kv_norm[0], cd_w_uk[0], cd_w_uv[0], cd_gdn_conv_w[0],
                   cd_gdn_dt_bias[0], cd_gdn_a_log[0], cd_gdn_norm[0], cd_w_out[0], rel_bias, final_norm, True)
    return x2.reshape(bsz, t, d)
```

```python
import functools
import math

import jax
import jax.numpy as jnp
from jax import lax
from jax.experimental import pallas as pl
from jax.experimental.pallas import tpu as pltpu

F32 = jnp.float32
BF16 = jnp.bfloat16
I32 = jnp.int32
I16 = jnp.int16
HI = lax.Precision.HIGHEST

EPS = 1e-6
CONV_WIDTH = 4
D_MODEL = 2048

ML_HEADS = 4
ML_DV = 512
ML_DK = 256

SSM_GROUPS = 4
SSM_HEADS_PER_GROUP = 8
SSM_HEAD_DIM = 64
SSM_STATE = 128
SSM_GROUP_WIDTH = SSM_HEADS_PER_GROUP * SSM_HEAD_DIM

ATT_HEADS = 16
ATT_HEAD_DIM = 128
KV_RANK = 512
IDX_HEADS = 16
IDX_DIM = 64
TOPK_MAX = 256
DSA_CHUNK = 64
REL_BUCKETS = 32
REL_MAX_DIST = 128

GDN_QK_HEADS = 16
GDN_V_HEADS = 32
GDN_HEAD_DIM = 128
GDN_GROUPS = 2
GDN_CHUNK = 64

VMEM_BYTES_V7X = 64 * 1024 * 1024
LANES = 128

ML_CHUNK = 256
SSD_CHUNK = 256
IDX_Q = 256
IDX_KT = 512
IDX_COUNT_UNROLL = 4
ATT_T = 512
ATT_SUB = 128
INT_MIN = -(2 ** 31)
HALF16 = 2 ** 15


def _cparams(sem, vmem_mib):
    return pltpu.CompilerParams(dimension_semantics=sem, vmem_limit_bytes=vmem_mib * 1024 * 1024)


def _bdot(a, b):
    return jnp.dot(a.astype(BF16), b.astype(BF16), preferred_element_type=F32)


def _bdot_nt(a, b):
    return lax.dot_general(a.astype(BF16), b.astype(BF16), (((1,), (1,)), ((), ())), preferred_element_type=F32)


def _bdot_tn(a, b):
    return lax.dot_general(a.astype(BF16), b.astype(BF16), (((0,), (0,)), ((), ())), preferred_element_type=F32)


def _split3(x):
    hi = x.astype(BF16)
    r = x - hi.astype(F32)
    mid = r.astype(BF16)
    return hi, mid, (r - mid.astype(F32)).astype(BF16)


def _as_bf16_01(sel):
    return jnp.where(sel, 1.0, 0.0).astype(BF16) if sel.dtype == jnp.bool_ else sel.astype(BF16)


def _dot_sel_lhs(sel, x):
    s = _as_bf16_01(sel)
    return sum(jnp.dot(s, p, preferred_element_type=F32) for p in _split3(x))


def _dot_sel_rhs(x, sel):
    s = _as_bf16_01(sel)
    return sum(jnp.dot(p, s, preferred_element_type=F32) for p in _split3(x))


def _softplus(x):
    return jnp.maximum(x, 0.0) + jnp.log1p(jnp.exp(-jnp.abs(x)))


def _sigmoid(x):
    return 1.0 / (1.0 + jnp.exp(-x))


def _silu(x):
    return x * _sigmoid(x)


def _norm_matmul_body(x_ref, g_ref, w_ref, o_ref, h_ref, *, blocked):
    @pl.when(pl.program_id(1) == 0)
    def _():
        x = x_ref[...].astype(F32)
        ms = jnp.mean(x * x, axis=-1, keepdims=True)
        h_ref[...] = (x * lax.rsqrt(ms + EPS) * g_ref[...]).astype(h_ref.dtype)

    acc = jnp.dot(h_ref[...], w_ref[...], preferred_element_type=F32)
    if blocked:
        bw = o_ref.shape[-1]
        for t in range(o_ref.shape[0]):
            o_ref[t] = acc[:, t * bw:(t + 1) * bw].astype(o_ref.dtype)
    else:
        o_ref[...] = acc.astype(o_ref.dtype)


def _norm_matmul(x, x_col_block, k_dim, gain, w, *, out_dtype, blocked_width=None, tm=1024, tn=512, name):
    m = x.shape[0]
    n = w.shape[1]
    tm = min(tm, m)
    assert m % tm == 0 and n % tn == 0 and w.shape[0] == k_dim
    grid = (m // tm, n // tn)
    if blocked_width is None:
        out_shape = jax.ShapeDtypeStruct((m, n), out_dtype)
        out_spec = pl.BlockSpec((tm, tn), lambda i, j: (i, j))
    else:
        nb = tn // blocked_width
        out_shape = jax.ShapeDtypeStruct((n // blocked_width, m, blocked_width), out_dtype)
        out_spec = pl.BlockSpec((nb, tm, blocked_width), lambda i, j: (j, i, 0))
    return pl.pallas_call(
        functools.partial(_norm_matmul_body, blocked=blocked_width is not None),
        grid=grid,
        in_specs=[
            pl.BlockSpec((tm, k_dim), lambda i, j: (i, x_col_block), pipeline_mode=pl.Buffered(1)),
            pl.BlockSpec((1, k_dim), lambda i, j: (0, 0)),
            pl.BlockSpec((k_dim, tn), lambda i, j: (0, j)),
        ],
        out_specs=out_spec,
        out_shape=out_shape,
        scratch_shapes=[pltpu.VMEM((tm, k_dim), BF16)],
        compiler_params=_cparams(("parallel", "arbitrary"), 48),
        name=name,
    )(x, gain.reshape(1, k_dim).astype(F32), w)


def _out_proj_body(*refs, n_y, k_tiles, final_norm):
    y_refs = refs[:n_y]
    w_ref, res_ref, g_ref, o_ref, acc_ref = refs[n_y:]
    k = pl.program_id(1)

    @pl.when(k == 0)
    def _():
        acc_ref[...] = jnp.zeros_like(acc_ref)

    lo = 0
    for y_ref, nk in zip(y_refs, k_tiles):
        @pl.when((k >= lo) & (k < lo + nk))
        def _(y_ref=y_ref):
            acc_ref[...] += jnp.dot(y_ref[...], w_ref[...], preferred_element_type=F32)
        lo += nk

    @pl.when(k == lo - 1)
    def _():
        xn = res_ref[...] + acc_ref[...]
        if final_norm:
            ms = jnp.mean(xn * xn, axis=-1, keepdims=True)
            xn = xn * lax.rsqrt(ms + EPS) * g_ref[...]
        o_ref[...] = xn


def _out_proj(ys, w, res, gain, *, final_norm, tm=512, tk=1024, name):
    m, d = res.shape
    tm = min(tm, m)
    k_tiles = [y.shape[1] // tk for y in ys]
    assert sum(k_tiles) * tk == w.shape[0]
    y_specs = []
    lo = 0
    for nk in k_tiles:
        y_specs.append(pl.BlockSpec((tm, tk), lambda i, k, lo=lo, nk=nk: (i, jnp.clip(k - lo, 0, nk - 1))))
        lo += nk
    return pl.pallas_call(
        functools.partial(_out_proj_body, n_y=len(ys), k_tiles=k_tiles, final_norm=final_norm),
        grid=(m // tm, lo),
        in_specs=y_specs + [
            pl.BlockSpec((tk, d), lambda i, k: (k, 0)),
            pl.BlockSpec((tm, d), lambda i, k: (i, 0)),
            pl.BlockSpec((1, d), lambda i, k: (0, 0)),
        ],
        out_specs=pl.BlockSpec((tm, d), lambda i, k: (i, 0)),
        out_shape=jax.ShapeDtypeStruct((m, d), F32),
        scratch_shapes=[pltpu.VMEM((tm, d), F32)],
        compiler_params=_cparams(("parallel", "arbitrary"), 48),
        name=name,
    )(*ys, w, res, gain.reshape(1, d).astype(F32))


def _mlstm_body(bi_ref, bf_ref, q_ref, k_ref, v_ref, op_ref, z_ref, g_ref, gain_ref, y_ref,
                c_scr, n_scr, m_scr):
    L = q_ref.shape[0]
    h = pl.program_id(1)

    @pl.when(pl.program_id(2) == 0)
    def _():
        c_scr[...] = jnp.zeros_like(c_scr)
        n_scr[...] = jnp.zeros_like(n_scr)
        m_scr[...] = jnp.zeros_like(m_scr)

    gates = g_ref[...]
    ic = gates[0:1, :] + bi_ref[h]
    fpre = gates[1:2, :] + bf_ref[h]
    fc = -_softplus(-fpre)

    ri = lax.broadcasted_iota(I32, (L, L), 0)
    ci = lax.broadcasted_iota(I32, (L, L), 1)
    causal = ci <= ri
    eye = ci == ri
    bcum_col = jnp.sum(jnp.where(causal, fc, 0.0), axis=1, keepdims=True)
    fc_col = jnp.sum(jnp.where(eye, fc, 0.0), axis=1, keepdims=True)
    bcum_row = jnp.sum(jnp.where(ri <= ci, fc_col, 0.0), axis=0, keepdims=True)

    m_st = m_scr[...]
    d = jnp.where(causal, bcum_col - bcum_row + ic, -jnp.inf)
    inter = bcum_col + m_st
    m_row = jnp.maximum(inter, jnp.max(d, axis=1, keepdims=True))
    w_inter = jnp.exp(inter - m_row)

    q = q_ref[...]
    k = k_ref[...] * (ML_DK ** -0.5)
    v = v_ref[...]
    qb = q.astype(BF16)
    vb = v.astype(BF16)
    p = jnp.exp(d - m_row) * _bdot_nt(qb, k)
    c_st = c_scr[...]
    n_st = n_scr[...]
    num = w_inter * _bdot(qb, c_st) + _bdot(p, vb)
    den = w_inter * jnp.sum(q * n_st, axis=1, keepdims=True) + jnp.sum(p, axis=1, keepdims=True)
    hout = num / jnp.maximum(jnp.abs(den), jnp.exp(-m_row))

    b_last = jnp.sum(fc, axis=1, keepdims=True)
    g_row = b_last - bcum_row + ic
    m_new = jnp.maximum(b_last + m_st, jnp.max(g_row, axis=1, keepdims=True))
    w_old = jnp.exp(b_last + m_st - m_new)
    w_in_col = jnp.sum(jnp.where(eye, jnp.exp(g_row - m_new), 0.0), axis=1, keepdims=True)
    kw = k * w_in_col
    c_scr[...] = w_old * c_st + _bdot_tn(kw, vb)
    n_scr[...] = w_old * n_st + jnp.sum(kw, axis=0, keepdims=True)
    m_scr[...] = m_new

    hn = hout * lax.rsqrt(jnp.mean(hout * hout, axis=-1, keepdims=True) + EPS) * gain_ref[...]
    y_ref[...] = (hn * _sigmoid(op_ref[...]) * _silu(z_ref[...])).astype(y_ref.dtype)


def _mlstm(proj, gates, b_i, b_f, gain, bsz, t):
    L = min(ML_CHUNK, t)
    nc = t // L
    row = lambda b, h, c: b * nc + c
    return pl.pallas_call(
        _mlstm_body,
        grid=(bsz, ML_HEADS, nc),
        in_specs=[
            pl.BlockSpec(memory_space=pltpu.SMEM),
            pl.BlockSpec(memory_space=pltpu.SMEM),
            pl.BlockSpec((L, ML_DK), lambda b, h, c: (row(b, h, c), h)),
            pl.BlockSpec((L, ML_DK), lambda b, h, c: (row(b, h, c), ML_HEADS + h)),
            pl.BlockSpec((L, ML_DV), lambda b, h, c: (row(b, h, c), ML_HEADS + h)),
            pl.BlockSpec((L, ML_DV), lambda b, h, c: (row(b, h, c), 2 * ML_HEADS + h)),
            pl.BlockSpec((L, ML_DV), lambda b, h, c: (row(b, h, c), 3 * ML_HEADS + h)),
            pl.BlockSpec((None, None, None, 2, L), lambda b, h, c: (b, h, c, 0, 0)),
            pl.BlockSpec((None, 1, ML_DV), lambda b, h, c: (h, 0, 0)),
        ],
        out_specs=pl.BlockSpec((L, ML_DV), lambda b, h, c: (row(b, h, c), h)),
        out_shape=jax.ShapeDtypeStruct((bsz * t, ML_HEADS * ML_DV), BF16),
        scratch_shapes=[pltpu.VMEM((ML_DK, ML_DV), F32), pltpu.VMEM((1, ML_DK), F32), pltpu.VMEM((1, 1), F32)],
        compiler_params=_cparams(("parallel", "parallel", "arbitrary"), 40),
        name="mlstm",
    )(b_i.astype(F32), b_f.astype(F32), proj, proj, proj, proj, proj, gates,
      gain.reshape(ML_HEADS, 1, ML_DV).astype(F32))


def _causal_conv(x, tail_ref, w):
    L = x.shape[0]
    xx = jnp.concatenate([tail_ref[...], x], axis=0)
    y = w[CONV_WIDTH - 1:CONV_WIDTH, :] * x
    for j in range(CONV_WIDTH - 1):
        y = y + w[j:j + 1, :] * pltpu.roll(xx, CONV_WIDTH - 1 - j, axis=0)[8:, :]
    tail_ref[...] = x[L - 8:, :]
    return y


def _head_expand(n_heads, width):
    r = lax.broadcasted_iota(I32, (n_heads, n_heads * width), 0)
    c = lax.broadcasted_iota(I32, (n_heads, n_heads * width), 1)
    return ((c >= r * width) & (c < (r + 1) * width)).astype(F32)


def _ssd_body(x_ref, bm_ref, cm_ref, z_ref, dtc_ref, dtr_ref, wx_ref, wb_ref, wc_ref, bx_ref, bb_ref, bc_ref,
              dtbc_ref, dtbr_ref, alc_ref, alr_ref, dsk_ref, gain_ref, y_ref, st_scr, tx_scr, tb_scr, tc_scr):
    L = x_ref.shape[0]
    nh, hd = SSM_HEADS_PER_GROUP, SSM_HEAD_DIM

    @pl.when(pl.program_id(2) == 0)
    def _():
        st_scr[...] = jnp.zeros_like(st_scr)
        tx_scr[...] = jnp.zeros_like(tx_scr)
        tb_scr[...] = jnp.zeros_like(tb_scr)
        tc_scr[...] = jnp.zeros_like(tc_scr)

    xc = _silu(_causal_conv(x_ref[...], tx_scr, wx_ref[...]) + bx_ref[...])
    bc = _silu(_causal_conv(bm_ref[...], tb_scr, wb_ref[...]) + bb_ref[...])
    cc = _silu(_causal_conv(cm_ref[...], tc_scr, wc_ref[...]) + bc_ref[...])

    ri = lax.broadcasted_iota(I32, (L, L), 0)
    ci = lax.broadcasted_iota(I32, (L, L), 1)
    causal = ci <= ri
    dt_col = _softplus(dtc_ref[...] + dtbc_ref[...])
    acum = _dot_sel_lhs(causal, dt_col * (-jnp.exp(alc_ref[...])))
    dt_row = _softplus(dtr_ref[...] + dtbr_ref[...])
    acum_row = _dot_sel_rhs(dt_row * (-jnp.exp(alr_ref[...])), ri <= ci)

    expand = _head_expand(nh, hd)
    acum_e = _dot_sel_rhs(acum, expand)
    xdt = xc * _dot_sel_rhs(dt_col, expand)
    xdt_b = xdt.astype(BF16)
    cb = _bdot_nt(cc, bc)

    lane = lax.broadcasted_iota(I32, (L, 2 * hd), 1)
    parts = []
    for hp in range(nh // 2):
        xpair = xdt_b[:, 2 * hd * hp:2 * hd * (hp + 1)]
        acc = None
        for e in range(2):
            hh = 2 * hp + e
            seg = acum[:, hh:hh + 1] - acum_row[hh:hh + 1, :]
            dec = jnp.exp(jnp.where(causal, seg, -jnp.inf))
            xsel = jnp.where((lane >= hd * e) & (lane < hd * (e + 1)), xpair, jnp.zeros_like(xpair))
            t = _bdot(cb * dec, xsel)
            acc = t if acc is None else acc + t
        parts.append(acc)
    y = jnp.concatenate(parts, axis=1)

    st = st_scr[...]
    y = y + _bdot(cc, st) * jnp.exp(acum_e)
    a_last = acum_e[L - 1:L, :]
    st_scr[...] = jnp.exp(a_last) * st + _bdot_tn(bc, xdt * jnp.exp(a_last - acum_e))

    y = y + xc * dsk_ref[...]
    y = y * _silu(z_ref[...])
    y = y * lax.rsqrt(jnp.mean(y * y, axis=-1, keepdims=True) + EPS) * gain_ref[...]
    y_ref[...] = y.astype(y_ref.dtype)


def _ssd(proj, dt_col, dt_row, conv_w, conv_b, dt_bias, a_log, d_skip, gain, bsz, t):
    L = min(SSD_CHUNK, t)
    nc = t // L
    G, nh, gw, ns = SSM_GROUPS, SSM_HEADS_PER_GROUP, SSM_GROUP_WIDTH, SSM_STATE
    row = lambda b, g, c: b * nc + c
    x_blk = 10240 // gw
    z_blk = 8192 // gw
    b_blk = 12288 // ns
    c_blk = 12800 // ns
    wb_blk = G * gw // ns
    conv_b2 = conv_b.reshape(1, -1).astype(F32)
    f = lambda a: a.astype(F32)
    return pl.pallas_call(
        _ssd_body,
        grid=(bsz, G, nc),
        in_specs=[
            pl.BlockSpec((L, gw), lambda b, g, c: (row(b, g, c), x_blk + g)),
            pl.BlockSpec((L, ns), lambda b, g, c: (row(b, g, c), b_blk + g)),
            pl.BlockSpec((L, ns), lambda b, g, c: (row(b, g, c), c_blk + g)),
            pl.BlockSpec((L, gw), lambda b, g, c: (row(b, g, c), z_blk + g)),
            pl.BlockSpec((None, None, L, nh), lambda b, g, c: (b, g, c, 0)),
            pl.BlockSpec((None, None, nh, L), lambda b, g, c: (b, g, 0, c)),
            pl.BlockSpec((CONV_WIDTH, gw), lambda b, g, c: (0, g)),
            pl.BlockSpec((CONV_WIDTH, ns), lambda b, g, c: (0, wb_blk + g)),
            pl.BlockSpec((CONV_WIDTH, ns), lambda b, g, c: (0, wb_blk + G + g)),
            pl.BlockSpec((1, gw), lambda b, g, c: (0, g)),
            pl.BlockSpec((1, ns), lambda b, g, c: (0, wb_blk + g)),
            pl.BlockSpec((1, ns), lambda b, g, c: (0, wb_blk + G + g)),
            pl.BlockSpec((None, 1, nh), lambda b, g, c: (g, 0, 0)),
            pl.BlockSpec((None, nh, 1), lambda b, g, c: (g, 0, 0)),
            pl.BlockSpec((None, 1, nh), lambda b, g, c: (g, 0, 0)),
            pl.BlockSpec((None, nh, 1), lambda b, g, c: (g, 0, 0)),
            pl.BlockSpec((None, 1, gw), lambda b, g, c: (g, 0, 0)),
            pl.BlockSpec((None, 1, gw), lambda b, g, c: (g, 0, 0)),
        ],
        out_specs=pl.BlockSpec((L, gw), lambda b, g, c: (row(b, g, c), g)),
        out_shape=jax.ShapeDtypeStruct((bsz * t, G * gw), BF16),
        scratch_shapes=[pltpu.VMEM((ns, gw), F32), pltpu.VMEM((8, gw), F32),
                        pltpu.VMEM((8, ns), F32), pltpu.VMEM((8, ns), F32)],
        compiler_params=_cparams(("parallel", "parallel", "arbitrary"), 40),
        name="ssd",
    )(proj, proj, proj, proj, dt_col, dt_row, f(conv_w), f(conv_w), f(conv_w), conv_b2, conv_b2, conv_b2,
      f(dt_bias).reshape(G, 1, nh), f(dt_bias).reshape(G, nh, 1), f(a_log).reshape(G, 1, nh),
      f(a_log).reshape(G, nh, 1), jnp.repeat(f(d_skip), SSM_HEAD_DIM).reshape(G, 1, gw), f(gain).reshape(G, 1, gw))


def _layer_ab(x2, bsz, t, norm_g, w_in, ml_b_i, ml_b_f, ml_norm, ssm_conv_w, ssm_conv_b, ssm_dt_bias,
              ssm_a_log, ssm_d, ssm_norm, w_out):
    m = bsz * t
    small = jnp.concatenate([w_in[:, 8192:8200], w_in[:, 13320:13352]], axis=1)
    w_main = jnp.concatenate(
        [w_in[:, :8192], w_in[:, 8200:13320], jnp.pad(small, ((0, 0), (0, 512 - small.shape[1])))], axis=1
    ).astype(BF16)
    proj = _norm_matmul(x2, 0, D_MODEL, norm_g, w_main, out_dtype=F32, tn=1536, name="proj_ab")
    gate_cols = proj[:, 13312:13352]
    ml_gates = gate_cols[:, :8].reshape(bsz, t // min(ML_CHUNK, t), min(ML_CHUNK, t), 2, ML_HEADS)
    ml_gates = jnp.transpose(ml_gates, (0, 4, 1, 3, 2))
    dt_raw = gate_cols[:, 8:40].reshape(bsz, t, SSM_GROUPS, SSM_HEADS_PER_GROUP)
    dt_col = jnp.transpose(dt_raw, (0, 2, 1, 3))
    dt_row = jnp.transpose(dt_raw, (0, 2, 3, 1))
    y_a = _mlstm(proj, ml_gates, ml_b_i, ml_b_f, ml_norm, bsz, t)
    y_b = _ssd(proj, dt_col, dt_row, ssm_conv_w, ssm_conv_b, ssm_dt_bias, ssm_a_log, ssm_d, ssm_norm, bsz, t)
    return _out_proj([y_a, y_b], w_out.astype(BF16), x2, jnp.ones((D_MODEL,), F32), final_norm=False,
                     name="out_ab")


def _indexer_body(iq_ref, iw_ref, ik_ref, mask_ref, key_scr, hi_scr, lo_scr, pos_scr, *, top_k):
    n_kt = key_scr.shape[0]
    i = pl.program_id(1)
    start = i * IDX_Q
    n_tiles = (start + IDX_Q + IDX_KT - 1) // IDX_KT
    iqb = iq_ref[...].astype(BF16)
    iw = iw_ref[...] * (IDX_HEADS ** -0.5 * IDX_DIM ** -0.5)
    qpos = start + lax.broadcasted_iota(I32, (IDX_Q, 1), 0)
    limit = (lax.shift_right_logical(qpos, int(math.log2(DSA_CHUNK))) + 1) * DSA_CHUNK
    zeros_k = jnp.zeros((IDX_DIM, IDX_KT), BF16)
    kiota = lax.broadcasted_iota(I32, (1, IDX_KT), 1)

    def score_tile(tt, carry):
        ikt = ik_ref[tt]
        ik_even = jnp.concatenate([ikt, zeros_k], axis=0)
        ik_odd = jnp.concatenate([zeros_k, ikt], axis=0)
        acc = jnp.zeros((IDX_Q, IDX_KT), F32)
        for p in range(IDX_HEADS // 2):
            xq = iqb[:, 2 * IDX_DIM * p:2 * IDX_DIM * (p + 1)]
            s0 = jnp.dot(xq, ik_even, preferred_element_type=F32)
            s1 = jnp.dot(xq, ik_odd, preferred_element_type=F32)
            acc = acc + iw[:, 2 * p:2 * p + 1] * jnp.maximum(s0, 0.0)
            acc = acc + iw[:, 2 * p + 1:2 * p + 2] * jnp.maximum(s1, 0.0)
        bits = lax.bitcast_convert_type(acc + 0.0, I32)
        key = jnp.where(bits < 0, bits ^ jnp.int32(0x7FFFFFFF), bits)
        key = jnp.where(tt * IDX_KT + kiota < limit, key, jnp.int32(INT_MIN))
        key_scr[tt] = key
        hi_scr[tt] = lax.shift_right_arithmetic(key, 16).astype(I16)
        lo_scr[tt] = ((key & jnp.int32(0xFFFF)) - jnp.int32(HALF16)).astype(I16)
        return carry

    lax.fori_loop(0, n_tiles, score_tile, 0)

    cu = min(IDX_COUNT_UNROLL, n_kt)
    n_groups = (n_tiles + cu - 1) // cu

    def neutral_tile(tt, carry):
        hi_scr[tt] = jnp.full((IDX_Q, IDX_KT), -HALF16, I16)
        lo_scr[tt] = jnp.full((IDX_Q, IDX_KT), -HALF16, I16)
        return carry

    lax.fori_loop(n_tiles, n_groups * cu, neutral_tile, 0)

    def count(pred):
        def body(tt, acc):
            w = jnp.where(pred(key_scr[tt], tt * IDX_KT + kiota), 1.0, 0.0)
            for s in range(IDX_KT // LANES):
                acc = acc + w[:, s * LANES:(s + 1) * LANES]
            return acc
        acc = lax.fori_loop(0, n_tiles, body, jnp.zeros((IDX_Q, LANES), F32))
        return jnp.sum(acc, axis=1, keepdims=True)

    def count16(ref, pred):
        def body(g, acc):
            for u in range(cu):
                for s in range(IDX_KT // LANES):
                    v = ref[g * cu + u, :, s * LANES:(s + 1) * LANES]
                    acc = acc + jnp.where(pred(v), jnp.int16(1), jnp.int16(0))
            return acc
        acc = lax.fori_loop(0, n_groups, body, jnp.zeros((IDX_Q, LANES), I16))
        return jnp.sum(acc.astype(F32), axis=1, keepdims=True)

    def lanes16(v):
        return jnp.broadcast_to(v, (IDX_Q, LANES)).astype(I16)

    def bisect16(ref, n_above):
        t16 = jnp.full((IDX_Q, 1), -HALF16, I32)
        for bit in range(15, -1, -1):
            cand = jnp.zeros_like(t16) if bit == 15 else t16 + jnp.int32(1 << bit)
            ct = lanes16(cand)
            t16 = jnp.where(n_above + count16(ref, lambda v, ct=ct: v >= ct) >= top_k, cand, t16)
        return t16

    thr_hi = bisect16(hi_scr, 0.0)
    th = lanes16(thr_hi)
    n_hi_gt = count16(hi_scr, lambda v: v > th)

    def keep_lo(tt, carry):
        for s in range(IDX_KT // LANES):
            cols = slice(s * LANES, (s + 1) * LANES)
            lo_scr[tt, :, cols] = jnp.where(hi_scr[tt, :, cols] == th, lo_scr[tt, :, cols], jnp.int16(-HALF16))
        return carry

    lax.fori_loop(0, n_groups * cu, keep_lo, 0)
    thr = thr_hi * jnp.int32(2 * HALF16) + (bisect16(lo_scr, n_hi_gt) + jnp.int32(HALF16))

    n_gt = count(lambda k, pos: k > thr)
    n_ge = count(lambda k, pos: k >= thr)
    room = top_k - n_gt
    pos_scr[...] = jnp.full((IDX_Q, 1), 2 * n_kt * IDX_KT, I32)

    @pl.when(jnp.max(jnp.where(thr > INT_MIN, n_ge, 0.0)) > top_k)
    def _():
        pcut = jnp.zeros((IDX_Q, 1), I32)
        for bit in range(int(math.log2(n_kt * IDX_KT)), -1, -1):
            cand = pcut + jnp.int32(1 << bit)
            n_tie = count(lambda k, pos, cand=cand: (k == thr) & (pos < cand))
            pcut = jnp.where(n_tie <= room, cand, pcut)
        pos_scr[...] = pcut

    pcut = pos_scr[...]

    def write_tile(tt, carry):
        k = key_scr[tt]
        pos = tt * IDX_KT + kiota
        sel = (k > thr) | ((k == thr) & (pos < pcut))
        sel = sel & (k > INT_MIN)
        mask_ref[tt] = jnp.where(sel, 1, 0).astype(jnp.int8)
        return carry

    lax.fori_loop(0, n_tiles, write_tile, 0)

    def zero_tile(tt, carry):
        mask_ref[tt] = jnp.zeros((IDX_Q, IDX_KT), jnp.int8)
        return carry

    lax.fori_loop(n_tiles, n_kt, zero_tile, 0)


def _indexer(proj, iq_blk, iw, ik_tiles, bsz, t):
    nq = t // IDX_Q
    n_kt = t // IDX_KT
    top_k = min(TOPK_MAX, t // 4)
    return pl.pallas_call(
        functools.partial(_indexer_body, top_k=top_k),
        grid=(bsz, nq),
        in_specs=[
            pl.BlockSpec((IDX_Q, IDX_HEADS * IDX_DIM), lambda b, i: (b * nq + i, iq_blk)),
            pl.BlockSpec((IDX_Q, IDX_HEADS), lambda b, i: (b * nq + i, 0)),
            pl.BlockSpec((None, n_kt, IDX_DIM, IDX_KT), lambda b, i: (b, 0, 0, 0)),
        ],
        out_specs=pl.BlockSpec((None, n_kt, IDX_Q, IDX_KT), lambda b, i: (b * nq + i, 0, 0, 0)),
        out_shape=jax.ShapeDtypeStruct((bsz * nq, n_kt, IDX_Q, IDX_KT), jnp.int8),
        scratch_shapes=[pltpu.VMEM((n_kt, IDX_Q, IDX_KT), I32), pltpu.VMEM((n_kt, IDX_Q, IDX_KT), I16),
                        pltpu.VMEM((n_kt, IDX_Q, IDX_KT), I16), pltpu.VMEM((IDX_Q, 1), I32)],
        compiler_params=_cparams(("parallel", "arbitrary"), 40),
        name="dsa_indexer",
    )(proj, iw, ik_tiles)


NEG_BIG = -3.0e38
LOG2_E = math.log2(math.e)


def _attn_body(q_ref, k_ref, v_ref, mask_ref, z_ref, tb_ref, o_ref, acc_scr, m_scr):
    T = ATT_T
    D = ATT_HEAD_DIM
    nsub = T // ATT_SUB
    i = pl.program_id(1)
    j = pl.program_id(2)

    @pl.when(j == 0)
    def _():
        acc_scr[...] = jnp.zeros_like(acc_scr)
        m_scr[...] = jnp.full_like(m_scr, NEG_BIG)

    def tile_step(near):
        sel = jnp.concatenate([mask_ref[a, 0].astype(F32) for a in range(mask_ref.shape[0])], axis=0)
        mask_bias = (sel - 1.0) * (-NEG_BIG)
        ones = jnp.ones((T, D), BF16)

        def pair_body(p, carry):
            q2, k2, v2 = q_ref[p], k_ref[p], v_ref[p]
            m_old = [m_scr[2 * p + e] for e in range(2)]
            a_old = [acc_scr[2 * p + e] for e in range(2)]
            new = []
            logits = [_bdot_nt(q2[:, D * e:D * (e + 1)], k2[:, D * e:D * (e + 1)]) for e in range(2)]
            for e in range(2):
                s = logits[e] + mask_bias
                if near is not None:
                    tb = tb_ref[2 * p + e]
                    zero = jnp.zeros((ATT_SUB, ATT_SUB), F32)
                    rows = []
                    for a in range(nsub):
                        cols = []
                        for bb in range(nsub):
                            rel_blocks = (bb - a) if near == "diag" else (bb - a - nsub)
                            cols.append(tb[0] if rel_blocks == 0 else tb[1] if rel_blocks == -1 else zero)
                        rows.append(jnp.concatenate(cols, axis=1))
                    s = s + jnp.concatenate(rows, axis=0)
                m_new = jnp.maximum(m_old[e], jnp.max(s, axis=1, keepdims=True))
                alpha = jnp.exp2(m_old[e] - m_new)
                pr = jnp.exp2(s - jnp.concatenate([m_new] * (T // LANES), axis=1))
                v_aug = jnp.concatenate([v2[:, D * e:D * (e + 1)], ones], axis=1)
                a_new = jnp.concatenate([alpha] * (2 * D // LANES), axis=1) * a_old[e] + _bdot(pr, v_aug)
                new.append((m_new, a_new))
            for e in range(2):
                m_scr[2 * p + e] = new[e][0]
                acc_scr[2 * p + e] = new[e][1]
            return carry

        lax.fori_loop(0, ATT_HEADS // 2, pair_body, 0, unroll=4)

    @pl.when(j < i - 1)
    def _():
        tile_step(None)

    @pl.when(j == i - 1)
    def _():
        tile_step("prev")

    @pl.when(j == i)
    def _():
        tile_step("diag")
        for p in range(ATT_HEADS // 2):
            for e in range(2):
                hh = 2 * p + e
                a = acc_scr[hh]
                o = a[:, :D] / a[:, D:]
                o = o * _silu(z_ref[p, :, D * e:D * (e + 1)].astype(F32))
                o_ref[:, D * hh:D * (hh + 1)] = o.astype(o_ref.dtype)


def _attention(qz, kv, mask, tb, bsz, t):
    T = ATT_T
    nq = t // T
    npair = ATT_HEADS // 2
    pw = 2 * ATT_HEAD_DIM
    nrow = T // IDX_Q
    ncol = T // IDX_KT
    return pl.pallas_call(
        _attn_body,
        grid=(bsz, nq, nq),
        in_specs=[
            pl.BlockSpec((npair, T, pw), lambda b, i, j: (0, b * nq + i, 0)),
            pl.BlockSpec((npair, T, pw), lambda b, i, j: (0, b * nq + jnp.minimum(j, i), 0)),
            pl.BlockSpec((npair, T, pw), lambda b, i, j: (1, b * nq + jnp.minimum(j, i), 0)),
            pl.BlockSpec((nrow, ncol, IDX_Q, IDX_KT), lambda b, i, j: (b * nq + i, jnp.minimum(j, i), 0, 0)),
            pl.BlockSpec((npair, T, pw), lambda b, i, j: (1, b * nq + i, 0)),
            pl.BlockSpec((ATT_HEADS, 2, ATT_SUB, ATT_SUB), lambda b, i, j: (0, 0, 0, 0)),
        ],
        out_specs=pl.BlockSpec((T, ATT_HEADS * ATT_HEAD_DIM), lambda b, i, j: (b * nq + i, 0)),
        out_shape=jax.ShapeDtypeStruct((bsz * t, ATT_HEADS * ATT_HEAD_DIM), BF16),
        scratch_shapes=[pltpu.VMEM((ATT_HEADS, T, 2 * ATT_HEAD_DIM), F32), pltpu.VMEM((ATT_HEADS, T, LANES), F32)],
        compiler_params=_cparams(("parallel", "parallel", "arbitrary"), 56),
        name="dsa_attention",
    )(qz, kv, kv, mask, qz, tb)


def _t5_bucket(rel):
    nb = REL_BUCKETS // 2
    max_exact = nb // 2
    ret = jnp.where(rel > 0, nb, 0)
    n = jnp.abs(rel)
    nf = jnp.maximum(n, max_exact).astype(F32)
    large = max_exact + (jnp.log(nf / max_exact) / math.log(REL_MAX_DIST / max_exact)
                         * (nb - max_exact)).astype(jnp.int32)
    large = jnp.minimum(large, nb - 1)
    return ret + jnp.where(n < max_exact, n, large)


def _rel_bias_blocks(rel_bias):
    r = jnp.arange(ATT_SUB)[:, None]
    c = jnp.arange(ATT_SUB)[None, :]
    rel = jnp.stack([c - r, c - r - ATT_SUB])
    assert ATT_SUB >= REL_MAX_DIST and ATT_SUB % DSA_CHUNK == 0
    rb = rel_bias.astype(F32)
    far = rb[_t5_bucket(jnp.int32(-2 * ATT_SUB))]
    onehot = (_t5_bucket(rel)[..., None] == jnp.arange(REL_BUCKETS)).astype(F32)
    table = jnp.einsum("vrcb,bh->hvrc", onehot, rb - far, precision=HI)
    return table * LOG2_E


GDN_PREP_ROWS = 256
GDN_SOLVE_LANES = 256


def _conv_with_prev(x, prev, w):
    xx = jnp.concatenate([prev, x], axis=0)
    y = w[CONV_WIDTH - 1:CONV_WIDTH, :] * x
    for j in range(CONV_WIDTH - 1):
        y = y + w[j:j + 1, :] * pltpu.roll(xx, CONV_WIDTH - 1 - j, axis=0)[8:, :]
    return y


def _gdn_prep_body(v_ref, q_ref, k_ref, pv_ref, pq_ref, pk_ref, bc_ref, ac_ref, ar_ref, wq_ref, wk_ref, wv_ref,
                   dtbc_ref, dtbr_ref, alc_ref, alr_ref,
                   qn_ref, rhs_ref, lu_ref, a_ref, egc_ref, egl_ref):
    L = GDN_CHUNK
    D = GDN_HEAD_DIM
    nv = GDN_V_HEADS // GDN_GROUPS
    rep = GDN_V_HEADS // GDN_QK_HEADS
    seq_start = pl.program_id(2) == 0

    def conv(x_ref, p_ref, w_ref):
        prev = jnp.where(seq_start, 0.0, p_ref[...])
        return _silu(_conv_with_prev(x_ref[...], prev, w_ref[...]))

    qc = conv(q_ref, pq_ref, wq_ref)
    kc = conv(k_ref, pk_ref, wk_ref)
    vc = conv(v_ref, pv_ref, wv_ref)
    qn, kn = [], []
    for kh in range(nv // rep):
        qh = qc[:, D * kh:D * (kh + 1)]
        kk_ = kc[:, D * kh:D * (kh + 1)]
        qh = qh * lax.rsqrt(jnp.sum(qh * qh, axis=-1, keepdims=True) + EPS) * (D ** -0.5)
        kn.append(kk_ * lax.rsqrt(jnp.sum(kk_ * kk_, axis=-1, keepdims=True) + EPS))
        qn.append(qh.astype(BF16))
        qn_ref[:, D * kh:D * (kh + 1)] = qn[kh]

    beta = _sigmoid(bc_ref[...])
    g_col = -jnp.exp(alc_ref[...]) * _softplus(ac_ref[...] + dtbc_ref[...])
    ri = lax.broadcasted_iota(I32, (L, L), 0)
    ci = lax.broadcasted_iota(I32, (L, L), 1)
    causal = ci <= ri
    strict = ci < ri
    for cc in range(a_ref.shape[0]):
        rows = slice(L * cc, L * (cc + 1))
        gc_col = _dot_sel_lhs(causal, g_col[rows, :])
        g_row = -jnp.exp(alr_ref[...]) * _softplus(ar_ref[cc] + dtbr_ref[...])
        gc_row = _dot_sel_rhs(g_row, ri <= ci)
        eg_col = jnp.exp(gc_col)
        egc_ref[rows, :] = eg_col
        for kh in range(nv // rep):
            kf = kn[kh][rows, :]
            kb = kf.astype(BF16)
            kk = _bdot_nt(kb, kb)
            qk = _bdot_nt(qn[kh][rows, :], kb)
            for e in range(rep):
                vh = rep * kh + e
                gcol = gc_col[:, vh:vh + 1]
                bcol = beta[rows, vh:vh + 1]
                dec = jnp.exp(jnp.where(causal, gcol - gc_row[vh:vh + 1, :], -jnp.inf))
                a_ref[cc, vh] = jnp.where(strict, bcol * kk * dec, 0.0)
                g_last = gcol[L - 1:L, :]
                kw_t = jnp.transpose(kf * jnp.exp(g_last - gcol))
                lu_ref[cc, vh] = jnp.concatenate([qk * dec, kw_t], axis=0).astype(lu_ref.dtype)
                egl_ref[cc, vh:vh + 1, :] = jnp.broadcast_to(jnp.exp(g_last), (1, LANES))
                rhs_ref[rows, 2 * D * vh:2 * D * (vh + 1)] = jnp.concatenate(
                    [vc[rows, D * vh:D * (vh + 1)] * bcol, kf * (bcol * eg_col[:, vh:vh + 1])], axis=1
                ).astype(rhs_ref.dtype)


def _gdn_solve_body(a_ref, t_ref, t_scr):
    L, _, PB = a_ref.shape
    UN = 8
    sub = lax.broadcasted_iota(I32, (L, PB), 0)
    t_scr[...] = jnp.zeros_like(t_scr)
    t_scr[0] = jnp.where(sub == 0, 1.0, 0.0)

    def row_body(i, carry):
        def col_block(jb, acc):
            j0 = pl.multiple_of(jb * UN, UN)
            a_blk = a_ref[i, pl.ds(j0, UN), :]
            for jj in range(UN):
                acc = acc + a_blk[jj:jj + 1, :] * t_scr[j0 + jj]
            return acc

        acc = lax.fori_loop(0, (i + UN - 1) // UN, col_block, jnp.zeros((L, PB), F32))
        t_scr[i] = jnp.where(sub == i, 1.0, 0.0) - acc
        return carry

    lax.fori_loop(1, L, row_body, 0)
    t_ref[...] = t_scr[...].astype(t_ref.dtype)


def _gdn_scan_body(qn_ref, rhs_ref, lu_ref, t_ref, z_ref, egc_ref, egl_ref, gain_ref, y_ref, s_scr):
    L = qn_ref.shape[0]
    D = GDN_HEAD_DIM
    nv = GDN_V_HEADS // GDN_GROUPS
    rep = GDN_V_HEADS // GDN_QK_HEADS

    @pl.when(pl.program_id(2) == 0)
    def _():
        s_scr[...] = jnp.zeros_like(s_scr)

    eg_col = egc_ref[...]
    z = z_ref[...]
    heads = range(nv)
    sol = [_bdot(t_ref[vh], rhs_ref[:, 2 * D * vh:2 * D * (vh + 1)]) for vh in heads]
    s_st = [s_scr[vh] for vh in heads]
    r_s = [_bdot(jnp.concatenate([sol[vh][:, D:].astype(BF16),
                                  qn_ref[:, D * (vh // rep):D * (vh // rep + 1)]], axis=0), s_st[vh])
           for vh in heads]
    u = [sol[vh][:, :D] - r_s[vh][:L] for vh in heads]
    r_u = [_bdot(lu_ref[vh], u[vh]) for vh in heads]
    for vh in heads:
        s_scr[vh] = egl_ref[vh:vh + 1, :] * s_st[vh] + r_u[vh][L:]
        o = r_s[vh][L:] * eg_col[:, vh:vh + 1] + r_u[vh][:L]
        o = o * lax.rsqrt(jnp.mean(o * o, axis=-1, keepdims=True) + EPS) * gain_ref[...]
        y_ref[:, D * vh:D * (vh + 1)] = (o * _silu(z[:, D * vh:D * (vh + 1)])).astype(y_ref.dtype)


def _gdn(proj, b_col, a_col, a_row, conv_w, dt_bias, a_log, gain, bsz, t):
    L = GDN_CHUNK
    nc = t // L
    G = GDN_GROUPS
    nv = GDN_V_HEADS // G
    qw = GDN_QK_HEADS // G * GDN_HEAD_DIM
    vw = nv * GDN_HEAD_DIM
    m = bsz * t
    z_blk, q_blk, k_blk = 4096 // vw, 8192 // qw, 10240 // qw
    f = lambda a: a.astype(F32)
    cw = f(conv_w)
    pc_spec = pl.BlockSpec((None, 1, nv), lambda b, g, c: (g, 0, 0))
    pr_spec = pl.BlockSpec((None, nv, 1), lambda b, g, c: (g, 0, 0))
    wq_spec = pl.BlockSpec((CONV_WIDTH, qw), lambda b, g, c: (0, g))
    wk_spec = pl.BlockSpec((CONV_WIDTH, qw), lambda b, g, c: (0, G + g))
    wv_spec = pl.BlockSpec((CONV_WIDTH, vw), lambda b, g, c: (0, (2 * G * qw) // vw + g))

    LB = min(GDN_PREP_ROWS, t)
    ncb = LB // L
    nb = t // LB
    rowb = lambda b, g, c: b * nb + c
    prev8 = lambda b, g, c: jnp.maximum((b * t + c * LB) // 8 - 1, 0)
    D = GDN_HEAD_DIM
    qn, rhs, lu, a_mat, egc, egl = pl.pallas_call(
        _gdn_prep_body,
        grid=(bsz, G, nb),
        in_specs=[
            pl.BlockSpec((LB, vw), lambda b, g, c: (rowb(b, g, c), g)),
            pl.BlockSpec((LB, qw), lambda b, g, c: (rowb(b, g, c), q_blk + g)),
            pl.BlockSpec((LB, qw), lambda b, g, c: (rowb(b, g, c), k_blk + g)),
            pl.BlockSpec((8, vw), lambda b, g, c: (prev8(b, g, c), g)),
            pl.BlockSpec((8, qw), lambda b, g, c: (prev8(b, g, c), q_blk + g)),
            pl.BlockSpec((8, qw), lambda b, g, c: (prev8(b, g, c), k_blk + g)),
            pl.BlockSpec((None, None, LB, nv), lambda b, g, c: (b, g, c, 0)),
            pl.BlockSpec((None, None, LB, nv), lambda b, g, c: (b, g, c, 0)),
            pl.BlockSpec((None, None, ncb, nv, L), lambda b, g, c: (b, g, c, 0, 0)),
            wq_spec, wk_spec, wv_spec, pc_spec, pr_spec, pc_spec, pr_spec,
        ],
        out_specs=[
            pl.BlockSpec((LB, qw), lambda b, g, c: (rowb(b, g, c), g)),
            pl.BlockSpec((LB, 2 * vw), lambda b, g, c: (rowb(b, g, c), g)),
            pl.BlockSpec((None, None, ncb, nv, L + D, L), lambda b, g, c: (b, g, c, 0, 0, 0)),
            pl.BlockSpec((None, None, ncb, nv, L, L), lambda b, g, c: (b, g, c, 0, 0, 0)),
            pl.BlockSpec((None, None, LB, nv), lambda b, g, c: (b, g, c, 0)),
            pl.BlockSpec((None, None, ncb, nv, LANES), lambda b, g, c: (b, g, c, 0, 0)),
        ],
        out_shape=[
            jax.ShapeDtypeStruct((m, G * qw), BF16),
            jax.ShapeDtypeStruct((m, 2 * G * vw), BF16),
            jax.ShapeDtypeStruct((bsz, G, nc, nv, L + D, L), BF16),
            jax.ShapeDtypeStruct((bsz, G, nc, nv, L, L), F32),
            jax.ShapeDtypeStruct((bsz, G, t, nv), F32),
            jax.ShapeDtypeStruct((bsz, G, nc, nv, LANES), F32),
        ],
        compiler_params=_cparams(("parallel", "parallel", "parallel"), 48),
        name="gdn_prep",
    )(proj, proj, proj, proj, proj, proj, b_col, a_col, a_row, cw, cw, cw,
      f(dt_bias).reshape(G, 1, nv), f(dt_bias).reshape(G, nv, 1), f(a_log).reshape(G, 1, nv),
      f(a_log).reshape(G, nv, 1))

    n_prob = bsz * G * nc * nv
    pb = min(GDN_SOLVE_LANES, n_prob)
    a_t = jnp.transpose(a_mat.reshape(n_prob, L, L), (1, 2, 0))
    t_t = pl.pallas_call(
        _gdn_solve_body,
        grid=(n_prob // pb,),
        in_specs=[pl.BlockSpec((L, L, pb), lambda p: (0, 0, p))],
        out_specs=pl.BlockSpec((L, L, pb), lambda p: (0, 0, p)),
        out_shape=jax.ShapeDtypeStruct((L, L, n_prob), BF16),
        scratch_shapes=[pltpu.VMEM((L, L, pb), F32)],
        compiler_params=_cparams(("parallel",), 40),
        name="gdn_solve",
    )(a_t)
    t_mat = jnp.transpose(t_t, (2, 0, 1)).reshape(bsz, G, nc, nv, L, L)

    row = lambda b, g, c: b * nc + c
    return pl.pallas_call(
        _gdn_scan_body,
        grid=(bsz, G, nc),
        in_specs=[
            pl.BlockSpec((L, qw), lambda b, g, c: (row(b, g, c), g)),
            pl.BlockSpec((L, 2 * vw), lambda b, g, c: (row(b, g, c), g)),
            pl.BlockSpec((None, None, None, nv, L + D, L), lambda b, g, c: (b, g, c, 0, 0, 0)),
            pl.BlockSpec((None, None, None, nv, L, L), lambda b, g, c: (b, g, c, 0, 0, 0)),
            pl.BlockSpec((L, vw), lambda b, g, c: (row(b, g, c), z_blk + g)),
            pl.BlockSpec((None, None, L, nv), lambda b, g, c: (b, g, c, 0)),
            pl.BlockSpec((None, None, None, nv, LANES), lambda b, g, c: (b, g, c, 0, 0)),
            pl.BlockSpec((1, GDN_HEAD_DIM), lambda b, g, c: (0, 0)),
        ],
        out_specs=pl.BlockSpec((L, vw), lambda b, g, c: (row(b, g, c), g)),
        out_shape=jax.ShapeDtypeStruct((m, GDN_V_HEADS * GDN_HEAD_DIM), BF16),
        scratch_shapes=[pltpu.VMEM((nv, GDN_HEAD_DIM, GDN_HEAD_DIM), F32)],
        compiler_params=_cparams(("parallel", "parallel", "arbitrary"), 40),
        name="gdn_scan",
    )(qn, rhs, lu, t_mat, proj, egc, egl, f(gain).reshape(1, GDN_HEAD_DIM))


def _layer_cd(x2, bsz, t, norm_g, w_in, kv_norm, w_uk, w_uv, gdn_conv_w, gdn_dt_bias, gdn_a_log, gdn_norm,
              w_out, rel_bias, final_g, final_norm):
    m = bsz * t
    small = jnp.concatenate([w_in[:, 5632:5712], w_in[:, 18000:18064]], axis=1)
    w_main = jnp.concatenate(
        [w_in[:, 9808:13904], w_in[:, 13904:18000], w_in[:, 5712:9808], w_in[:, 4608:5632], w_in[:, 2048:2560],
         jnp.pad(small, ((0, 0), (0, 512 - small.shape[1])))], axis=1).astype(BF16)
    proj = _norm_matmul(x2, 0, D_MODEL, norm_g, w_main, out_dtype=F32, tn=1024, name="proj_cd")
    w_attn = jnp.concatenate([w_in[:, :2048] * (ATT_HEAD_DIM ** -0.5 * LOG2_E), w_in[:, 2560:4608]],
                             axis=1).astype(BF16)
    qz = _norm_matmul(x2, 0, D_MODEL, norm_g, w_attn, out_dtype=BF16, blocked_width=2 * ATT_HEAD_DIM,
                      tn=1024, name="proj_cd_attn")
    w_kv = jnp.concatenate([w_uk.reshape(KV_RANK, -1), w_uv.reshape(KV_RANK, -1)], axis=1).astype(BF16)
    kv = _norm_matmul(proj, 13312 // KV_RANK, KV_RANK, kv_norm, w_kv, out_dtype=BF16,
                      blocked_width=2 * ATT_HEAD_DIM, tn=1024, name="dsa_kv")

    gate_cols = proj[:, 13824:13824 + 144]
    n_kt = t // IDX_KT
    ik_tiles = jnp.transpose(gate_cols[:, :IDX_DIM].reshape(bsz, n_kt, IDX_KT, IDX_DIM), (0, 1, 3, 2)).astype(BF16)
    iw = gate_cols[:, IDX_DIM:IDX_DIM + IDX_HEADS]
    mask = _indexer(proj, 12288 // (IDX_HEADS * IDX_DIM), iw, ik_tiles, bsz, t)
    y_c = _attention(qz, kv, mask, _rel_bias_blocks(rel_bias), bsz, t)

    nv = GDN_V_HEADS // GDN_GROUPS
    b_pre = gate_cols[:, 80:112].reshape(bsz, t, GDN_GROUPS, nv)
    a_pre = gate_cols[:, 112:144].reshape(bsz, t, GDN_GROUPS, nv)
    col = lambda a: jnp.transpose(a, (0, 2, 1, 3))
    rowf = lambda a: jnp.transpose(a.reshape(bsz, t // GDN_CHUNK, GDN_CHUNK, GDN_GROUPS, nv), (0, 3, 1, 4, 2))
    y_d = _gdn(proj, col(b_pre), col(a_pre), rowf(a_pre), gdn_conv_w, gdn_dt_bias, gdn_a_log, gdn_norm, bsz, t)
    return _out_proj([y_c, y_d], w_out.astype(BF16), x2, final_g, final_norm=final_norm, name="out_cd")


def kernel(x, rel_bias, ab_norm, ab_w_in, ab_ml_b_i, ab_ml_b_f, ab_ml_norm, ab_ssm_conv_w, ab_ssm_conv_b, ab_ssm_dt_bias, ab_ssm_a_log, ab_ssm_d, ab_ssm_norm, ab_w_out, cd_norm, cd_w_in, cd_kv_norm, cd_w_uk, cd_w_uv, cd_gdn_conv_w, cd_gdn_dt_bias, cd_gdn_a_log, cd_gdn_norm, cd_w_out, final_norm):
    bsz, t, d = x.shape
    x2 = x.reshape(bsz * t, d)
    x2 = _layer_ab(x2, bsz, t, ab_norm[0], ab_w_in[0], ab_ml_b_i[0], ab_ml_b_f[0], ab_ml_norm[0],
                   ab_ssm_conv_w[0], ab_ssm_conv_b[0], ab_ssm_dt_bias[0], ab_ssm_a_log[0], ab_ssm_d[0],
                   ab_ssm_norm[0], ab_w_out[0])
    x2 = _layer_cd(x2, bsz, t, cd_norm[0], cd_w_in[0], cd_kv_norm[0], cd_w_uk[0], cd_w_uv[0], cd_gdn_conv_w[0],
                   cd_gdn_dt_bias[0], cd_gdn_a_log[0], cd_gdn_norm[0], cd_w_out[0], rel_bias, final_norm, True)
    return x2.reshape(bsz, t, d)
```

```python
import functools
import math

import jax
import jax.numpy as jnp
from jax import lax
from jax.experimental import pallas as pl
from jax.experimental.pallas import tpu as pltpu

F32 = jnp.float32
BF16 = jnp.bfloat16
I32 = jnp.int32
I16 = jnp.int16
HI = lax.Precision.HIGHEST

EPS = 1e-6
CONV_WIDTH = 4
D_MODEL = 2048

ML_HEADS = 4
ML_DV = 512
ML_DK = 256

SSM_GROUPS = 4
SSM_HEADS_PER_GROUP = 8
SSM_HEAD_DIM = 64
SSM_STATE = 128
SSM_GROUP_WIDTH = SSM_HEADS_PER_GROUP * SSM_HEAD_DIM

ATT_HEADS = 16
ATT_HEAD_DIM = 128
KV_RANK = 512
IDX_HEADS = 16
IDX_DIM = 64
TOPK_MAX = 256
DSA_CHUNK = 64
REL_BUCKETS = 32
REL_MAX_DIST = 128

GDN_QK_HEADS = 16
GDN_V_HEADS = 32
GDN_HEAD_DIM = 128
GDN_GROUPS = 2
GDN_CHUNK = 64

VMEM_BYTES_V7X = 64 * 1024 * 1024
LANES = 128

ML_CHUNK = 256
SSD_CHUNK = 256
IDX_Q = 256
IDX_KT = 512
ATT_T = 512
ATT_SUB = 128
INT_MIN = -(2 ** 31)
HALF16 = 2 ** 15


def _cparams(sem, vmem_mib):
    return pltpu.CompilerParams(dimension_semantics=sem, vmem_limit_bytes=vmem_mib * 1024 * 1024)


def _bdot(a, b):
    return jnp.dot(a.astype(BF16), b.astype(BF16), preferred_element_type=F32)


def _bdot_nt(a, b):
    return lax.dot_general(a.astype(BF16), b.astype(BF16), (((1,), (1,)), ((), ())), preferred_element_type=F32)


def _bdot_tn(a, b):
    return lax.dot_general(a.astype(BF16), b.astype(BF16), (((0,), (0,)), ((), ())), preferred_element_type=F32)


def _split3(x):
    hi = x.astype(BF16)
    r = x - hi.astype(F32)
    mid = r.astype(BF16)
    return hi, mid, (r - mid.astype(F32)).astype(BF16)


def _as_bf16_01(sel):
    return jnp.where(sel, 1.0, 0.0).astype(BF16) if sel.dtype == jnp.bool_ else sel.astype(BF16)


def _dot_sel_lhs(sel, x):
    s = _as_bf16_01(sel)
    return sum(jnp.dot(s, p, preferred_element_type=F32) for p in _split3(x))


def _dot_sel_rhs(x, sel):
    s = _as_bf16_01(sel)
    return sum(jnp.dot(p, s, preferred_element_type=F32) for p in _split3(x))


def _softplus(x):
    return jnp.maximum(x, 0.0) + jnp.log1p(jnp.exp(-jnp.abs(x)))


def _sigmoid(x):
    return 1.0 / (1.0 + jnp.exp(-x))


def _silu(x):
    return x * _sigmoid(x)


def _norm_matmul_body(x_ref, g_ref, w_ref, o_ref, h_ref, *, blocked):
    @pl.when(pl.program_id(1) == 0)
    def _():
        x = x_ref[...].astype(F32)
        ms = jnp.mean(x * x, axis=-1, keepdims=True)
        h_ref[...] = (x * lax.rsqrt(ms + EPS) * g_ref[...]).astype(h_ref.dtype)

    acc = jnp.dot(h_ref[...], w_ref[...], preferred_element_type=F32)
    if blocked:
        bw = o_ref.shape[-1]
        for t in range(o_ref.shape[0]):
            o_ref[t] = acc[:, t * bw:(t + 1) * bw].astype(o_ref.dtype)
    else:
        o_ref[...] = acc.astype(o_ref.dtype)


def _norm_matmul(x, x_col_block, k_dim, gain, w, *, out_dtype, blocked_width=None, tm=1024, tn=512, name):
    m = x.shape[0]
    n = w.shape[1]
    tm = min(tm, m)
    assert m % tm == 0 and n % tn == 0 and w.shape[0] == k_dim
    grid = (m // tm, n // tn)
    if blocked_width is None:
        out_shape = jax.ShapeDtypeStruct((m, n), out_dtype)
        out_spec = pl.BlockSpec((tm, tn), lambda i, j: (i, j))
    else:
        nb = tn // blocked_width
        out_shape = jax.ShapeDtypeStruct((n // blocked_width, m, blocked_width), out_dtype)
        out_spec = pl.BlockSpec((nb, tm, blocked_width), lambda i, j: (j, i, 0))
    return pl.pallas_call(
        functools.partial(_norm_matmul_body, blocked=blocked_width is not None),
        grid=grid,
        in_specs=[
            pl.BlockSpec((tm, k_dim), lambda i, j: (i, x_col_block), pipeline_mode=pl.Buffered(1)),
            pl.BlockSpec((1, k_dim), lambda i, j: (0, 0)),
            pl.BlockSpec((k_dim, tn), lambda i, j: (0, j)),
        ],
        out_specs=out_spec,
        out_shape=out_shape,
        scratch_shapes=[pltpu.VMEM((tm, k_dim), BF16)],
        compiler_params=_cparams(("parallel", "arbitrary"), 48),
        name=name,
    )(x, gain.reshape(1, k_dim).astype(F32), w)


def _out_proj_body(*refs, n_y, k_tiles, final_norm):
    y_refs = refs[:n_y]
    w_ref, res_ref, g_ref, o_ref, acc_ref = refs[n_y:]
    k = pl.program_id(1)

    @pl.when(k == 0)
    def _():
        acc_ref[...] = jnp.zeros_like(acc_ref)

    lo = 0
    for y_ref, nk in zip(y_refs, k_tiles):
        @pl.when((k >= lo) & (k < lo + nk))
        def _(y_ref=y_ref):
            acc_ref[...] += jnp.dot(y_ref[...], w_ref[...], preferred_element_type=F32)
        lo += nk

    @pl.when(k == lo - 1)
    def _():
        xn = res_ref[...] + acc_ref[...]
        if final_norm:
            ms = jnp.mean(xn * xn, axis=-1, keepdims=True)
            xn = xn * lax.rsqrt(ms + EPS) * g_ref[...]
        o_ref[...] = xn


def _out_proj(ys, w, res, gain, *, final_norm, tm=512, tk=1024, name):
    m, d = res.shape
    tm = min(tm, m)
    k_tiles = [y.shape[1] // tk for y in ys]
    assert sum(k_tiles) * tk == w.shape[0]
    y_specs = []
    lo = 0
    for nk in k_tiles:
        y_specs.append(pl.BlockSpec((tm, tk), lambda i, k, lo=lo, nk=nk: (i, jnp.clip(k - lo, 0, nk - 1))))
        lo += nk
    return pl.pallas_call(
        functools.partial(_out_proj_body, n_y=len(ys), k_tiles=k_tiles, final_norm=final_norm),
        grid=(m // tm, lo),
        in_specs=y_specs + [
            pl.BlockSpec((tk, d), lambda i, k: (k, 0)),
            pl.BlockSpec((tm, d), lambda i, k: (i, 0)),
            pl.BlockSpec((1, d), lambda i, k: (0, 0)),
        ],
        out_specs=pl.BlockSpec((tm, d), lambda i, k: (i, 0)),
        out_shape=jax.ShapeDtypeStruct((m, d), F32),
        scratch_shapes=[pltpu.VMEM((tm, d), F32)],
        compiler_params=_cparams(("parallel", "arbitrary"), 48),
        name=name,
    )(*ys, w, res, gain.reshape(1, d).astype(F32))


def _mlstm_body(bi_ref, bf_ref, q_ref, k_ref, v_ref, op_ref, z_ref, g_ref, gain_ref, y_ref,
                c_scr, n_scr, m_scr):
    L = q_ref.shape[0]
    h = pl.program_id(1)

    @pl.when(pl.program_id(2) == 0)
    def _():
        c_scr[...] = jnp.zeros_like(c_scr)
        n_scr[...] = jnp.zeros_like(n_scr)
        m_scr[...] = jnp.zeros_like(m_scr)

    gates = g_ref[...]
    ic = gates[0:1, :] + bi_ref[h]
    fpre = gates[1:2, :] + bf_ref[h]
    fc = -_softplus(-fpre)

    ri = lax.broadcasted_iota(I32, (L, L), 0)
    ci = lax.broadcasted_iota(I32, (L, L), 1)
    causal = ci <= ri
    eye = ci == ri
    bcum_col = jnp.sum(jnp.where(causal, fc, 0.0), axis=1, keepdims=True)
    fc_col = jnp.sum(jnp.where(eye, fc, 0.0), axis=1, keepdims=True)
    bcum_row = jnp.sum(jnp.where(ri <= ci, fc_col, 0.0), axis=0, keepdims=True)

    m_st = m_scr[...]
    d = jnp.where(causal, bcum_col - bcum_row + ic, -jnp.inf)
    inter = bcum_col + m_st
    m_row = jnp.maximum(inter, jnp.max(d, axis=1, keepdims=True))
    w_inter = jnp.exp(inter - m_row)

    q = q_ref[...]
    k = k_ref[...] * (ML_DK ** -0.5)
    v = v_ref[...]
    qb = q.astype(BF16)
    vb = v.astype(BF16)
    p = jnp.exp(d - m_row) * _bdot_nt(qb, k)
    c_st = c_scr[...]
    n_st = n_scr[...]
    num = w_inter * _bdot(qb, c_st) + _bdot(p, vb)
    den = w_inter * jnp.sum(q * n_st, axis=1, keepdims=True) + jnp.sum(p, axis=1, keepdims=True)
    hout = num / jnp.maximum(jnp.abs(den), jnp.exp(-m_row))

    b_last = jnp.sum(fc, axis=1, keepdims=True)
    g_row = b_last - bcum_row + ic
    m_new = jnp.maximum(b_last + m_st, jnp.max(g_row, axis=1, keepdims=True))
    w_old = jnp.exp(b_last + m_st - m_new)
    w_in_col = jnp.sum(jnp.where(eye, jnp.exp(g_row - m_new), 0.0), axis=1, keepdims=True)
    kw = k * w_in_col
    c_scr[...] = w_old * c_st + _bdot_tn(kw, vb)
    n_scr[...] = w_old * n_st + jnp.sum(kw, axis=0, keepdims=True)
    m_scr[...] = m_new

    hn = hout * lax.rsqrt(jnp.mean(hout * hout, axis=-1, keepdims=True) + EPS) * gain_ref[...]
    y_ref[...] = (hn * _sigmoid(op_ref[...]) * _silu(z_ref[...])).astype(y_ref.dtype)


def _mlstm(proj, gates, b_i, b_f, gain, bsz, t):
    L = min(ML_CHUNK, t)
    nc = t // L
    row = lambda b, h, c: b * nc + c
    return pl.pallas_call(
        _mlstm_body,
        grid=(bsz, ML_HEADS, nc),
        in_specs=[
            pl.BlockSpec(memory_space=pltpu.SMEM),
            pl.BlockSpec(memory_space=pltpu.SMEM),
            pl.BlockSpec((L, ML_DK), lambda b, h, c: (row(b, h, c), h)),
            pl.BlockSpec((L, ML_DK), lambda b, h, c: (row(b, h, c), ML_HEADS + h)),
            pl.BlockSpec((L, ML_DV), lambda b, h, c: (row(b, h, c), ML_HEADS + h)),
            pl.BlockSpec((L, ML_DV), lambda b, h, c: (row(b, h, c), 2 * ML_HEADS + h)),
            pl.BlockSpec((L, ML_DV), lambda b, h, c: (row(b, h, c), 3 * ML_HEADS + h)),
            pl.BlockSpec((None, None, None, 2, L), lambda b, h, c: (b, h, c, 0, 0)),
            pl.BlockSpec((None, 1, ML_DV), lambda b, h, c: (h, 0, 0)),
        ],
        out_specs=pl.BlockSpec((L, ML_DV), lambda b, h, c: (row(b, h, c), h)),
        out_shape=jax.ShapeDtypeStruct((bsz * t, ML_HEADS * ML_DV), BF16),
        scratch_shapes=[pltpu.VMEM((ML_DK, ML_DV), F32), pltpu.VMEM((1, ML_DK), F32), pltpu.VMEM((1, 1), F32)],
        compiler_params=_cparams(("parallel", "parallel", "arbitrary"), 40),
        name="mlstm",
    )(b_i.astype(F32), b_f.astype(F32), proj, proj, proj, proj, proj, gates,
      gain.reshape(ML_HEADS, 1, ML_DV).astype(F32))


def _causal_conv(x, tail_ref, w):
    L = x.shape[0]
    xx = jnp.concatenate([tail_ref[...], x], axis=0)
    y = w[CONV_WIDTH - 1:CONV_WIDTH, :] * x
    for j in range(CONV_WIDTH - 1):
        y = y + w[j:j + 1, :] * pltpu.roll(xx, CONV_WIDTH - 1 - j, axis=0)[8:, :]
    tail_ref[...] = x[L - 8:, :]
    return y


def _head_expand(n_heads, width):
    r = lax.broadcasted_iota(I32, (n_heads, n_heads * width), 0)
    c = lax.broadcasted_iota(I32, (n_heads, n_heads * width), 1)
    return ((c >= r * width) & (c < (r + 1) * width)).astype(F32)


def _ssd_body(x_ref, bm_ref, cm_ref, z_ref, dtc_ref, dtr_ref, wx_ref, wb_ref, wc_ref, bx_ref, bb_ref, bc_ref,
              dtbc_ref, dtbr_ref, alc_ref, alr_ref, dsk_ref, gain_ref, y_ref, st_scr, tx_scr, tb_scr, tc_scr):
    L = x_ref.shape[0]
    nh, hd = SSM_HEADS_PER_GROUP, SSM_HEAD_DIM

    @pl.when(pl.program_id(2) == 0)
    def _():
        st_scr[...] = jnp.zeros_like(st_scr)
        tx_scr[...] = jnp.zeros_like(tx_scr)
        tb_scr[...] = jnp.zeros_like(tb_scr)
        tc_scr[...] = jnp.zeros_like(tc_scr)

    xc = _silu(_causal_conv(x_ref[...], tx_scr, wx_ref[...]) + bx_ref[...])
    bc = _silu(_causal_conv(bm_ref[...], tb_scr, wb_ref[...]) + bb_ref[...])
    cc = _silu(_causal_conv(cm_ref[...], tc_scr, wc_ref[...]) + bc_ref[...])

    ri = lax.broadcasted_iota(I32, (L, L), 0)
    ci = lax.broadcasted_iota(I32, (L, L), 1)
    causal = ci <= ri
    dt_col = _softplus(dtc_ref[...] + dtbc_ref[...])
    acum = _dot_sel_lhs(causal, dt_col * (-jnp.exp(alc_ref[...])))
    dt_row = _softplus(dtr_ref[...] + dtbr_ref[...])
    acum_row = _dot_sel_rhs(dt_row * (-jnp.exp(alr_ref[...])), ri <= ci)

    expand = _head_expand(nh, hd)
    acum_e = _dot_sel_rhs(acum, expand)
    xdt = xc * _dot_sel_rhs(dt_col, expand)
    xdt_b = xdt.astype(BF16)
    cb = _bdot_nt(cc, bc)

    lane = lax.broadcasted_iota(I32, (L, 2 * hd), 1)
    parts = []
    for hp in range(nh // 2):
        xpair = xdt_b[:, 2 * hd * hp:2 * hd * (hp + 1)]
        acc = None
        for e in range(2):
            hh = 2 * hp + e
            seg = acum[:, hh:hh + 1] - acum_row[hh:hh + 1, :]
            dec = jnp.exp(jnp.where(causal, seg, -jnp.inf))
            xsel = jnp.where((lane >= hd * e) & (lane < hd * (e + 1)), xpair, jnp.zeros_like(xpair))
            t = _bdot(cb * dec, xsel)
            acc = t if acc is None else acc + t
        parts.append(acc)
    y = jnp.concatenate(parts, axis=1)

    st = st_scr[...]
    y = y + _bdot(cc, st) * jnp.exp(acum_e)
    a_last = acum_e[L - 1:L, :]
    st_scr[...] = jnp.exp(a_last) * st + _bdot_tn(bc, xdt * jnp.exp(a_last - acum_e))

    y = y + xc * dsk_ref[...]
    y = y * _silu(z_ref[...])
    y = y * lax.rsqrt(jnp.mean(y * y, axis=-1, keepdims=True) + EPS) * gain_ref[...]
    y_ref[...] = y.astype(y_ref.dtype)


def _ssd(proj, dt_col, dt_row, conv_w, conv_b, dt_bias, a_log, d_skip, gain, bsz, t):
    L = min(SSD_CHUNK, t)
    nc = t // L
    G, nh, gw, ns = SSM_GROUPS, SSM_HEADS_PER_GROUP, SSM_GROUP_WIDTH, SSM_STATE
    row = lambda b, g, c: b * nc + c
    x_blk = 10240 // gw
    z_blk = 8192 // gw
    b_blk = 12288 // ns
    c_blk = 12800 // ns
    wb_blk = G * gw // ns
    conv_b2 = conv_b.reshape(1, -1).astype(F32)
    f = lambda a: a.astype(F32)
    return pl.pallas_call(
        _ssd_body,
        grid=(bsz, G, nc),
        in_specs=[
            pl.BlockSpec((L, gw), lambda b, g, c: (row(b, g, c), x_blk + g)),
            pl.BlockSpec((L, ns), lambda b, g, c: (row(b, g, c), b_blk + g)),
            pl.BlockSpec((L, ns), lambda b, g, c: (row(b, g, c), c_blk + g)),
            pl.BlockSpec((L, gw), lambda b, g, c: (row(b, g, c), z_blk + g)),
            pl.BlockSpec((None, None, L, nh), lambda b, g, c: (b, g, c, 0)),
            pl.BlockSpec((None, None, nh, L), lambda b, g, c: (b, g, 0, c)),
            pl.BlockSpec((CONV_WIDTH, gw), lambda b, g, c: (0, g)),
            pl.BlockSpec((CONV_WIDTH, ns), lambda b, g, c: (0, wb_blk + g)),
            pl.BlockSpec((CONV_WIDTH, ns), lambda b, g, c: (0, wb_blk + G + g)),
            pl.BlockSpec((1, gw), lambda b, g, c: (0, g)),
            pl.BlockSpec((1, ns), lambda b, g, c: (0, wb_blk + g)),
            pl.BlockSpec((1, ns), lambda b, g, c: (0, wb_blk + G + g)),
            pl.BlockSpec((None, 1, nh), lambda b, g, c: (g, 0, 0)),
            pl.BlockSpec((None, nh, 1), lambda b, g, c: (g, 0, 0)),
            pl.BlockSpec((None, 1, nh), lambda b, g, c: (g, 0, 0)),
            pl.BlockSpec((None, nh, 1), lambda b, g, c: (g, 0, 0)),
            pl.BlockSpec((None, 1, gw), lambda b, g, c: (g, 0, 0)),
            pl.BlockSpec((None, 1, gw), lambda b, g, c: (g, 0, 0)),
        ],
        out_specs=pl.BlockSpec((L, gw), lambda b, g, c: (row(b, g, c), g)),
        out_shape=jax.ShapeDtypeStruct((bsz * t, G * gw), BF16),
        scratch_shapes=[pltpu.VMEM((ns, gw), F32), pltpu.VMEM((8, gw), F32),
                        pltpu.VMEM((8, ns), F32), pltpu.VMEM((8, ns), F32)],
        compiler_params=_cparams(("parallel", "parallel", "arbitrary"), 40),
        name="ssd",
    )(proj, proj, proj, proj, dt_col, dt_row, f(conv_w), f(conv_w), f(conv_w), conv_b2, conv_b2, conv_b2,
      f(dt_bias).reshape(G, 1, nh), f(dt_bias).reshape(G, nh, 1), f(a_log).reshape(G, 1, nh),
      f(a_log).reshape(G, nh, 1), jnp.repeat(f(d_skip), SSM_HEAD_DIM).reshape(G, 1, gw), f(gain).reshape(G, 1, gw))


def _layer_ab(x2, bsz, t, norm_g, w_in, ml_b_i, ml_b_f, ml_norm, ssm_conv_w, ssm_conv_b, ssm_dt_bias,
              ssm_a_log, ssm_d, ssm_norm, w_out):
    m = bsz * t
    small = jnp.concatenate([w_in[:, 8192:8200], w_in[:, 13320:13352]], axis=1)
    w_main = jnp.concatenate(
        [w_in[:, :8192], w_in[:, 8200:13320], jnp.pad(small, ((0, 0), (0, 512 - small.shape[1])))], axis=1
    ).astype(BF16)
    proj = _norm_matmul(x2, 0, D_MODEL, norm_g, w_main, out_dtype=F32, tn=1536, name="proj_ab")
    gate_cols = proj[:, 13312:13352]
    ml_gates = gate_cols[:, :8].reshape(bsz, t // min(ML_CHUNK, t), min(ML_CHUNK, t), 2, ML_HEADS)
    ml_gates = jnp.transpose(ml_gates, (0, 4, 1, 3, 2))
    dt_raw = gate_cols[:, 8:40].reshape(bsz, t, SSM_GROUPS, SSM_HEADS_PER_GROUP)
    dt_col = jnp.transpose(dt_raw, (0, 2, 1, 3))
    dt_row = jnp.transpose(dt_raw, (0, 2, 3, 1))
    y_a = _mlstm(proj, ml_gates, ml_b_i, ml_b_f, ml_norm, bsz, t)
    y_b = _ssd(proj, dt_col, dt_row, ssm_conv_w, ssm_conv_b, ssm_dt_bias, ssm_a_log, ssm_d, ssm_norm, bsz, t)
    return _out_proj([y_a, y_b], w_out.astype(BF16), x2, jnp.ones((D_MODEL,), F32), final_norm=False,
                     name="out_ab")


def _indexer_body(iq_ref, iw_ref, ik_ref, mask_ref, key_scr, hi_scr, lo_scr, pos_scr, *, top_k):
    n_kt = key_scr.shape[0]
    i = pl.program_id(1)
    start = i * IDX_Q
    n_tiles = (start + IDX_Q + IDX_KT - 1) // IDX_KT
    iqb = iq_ref[...].astype(BF16)
    iw = iw_ref[...] * (IDX_HEADS ** -0.5 * IDX_DIM ** -0.5)
    qpos = start + lax.broadcasted_iota(I32, (IDX_Q, 1), 0)
    limit = (lax.shift_right_logical(qpos, int(math.log2(DSA_CHUNK))) + 1) * DSA_CHUNK
    zeros_k = jnp.zeros((IDX_DIM, IDX_KT), BF16)
    kiota = lax.broadcasted_iota(I32, (1, IDX_KT), 1)

    def score_tile(tt, carry):
        ikt = ik_ref[tt]
        ik_even = jnp.concatenate([ikt, zeros_k], axis=0)
        ik_odd = jnp.concatenate([zeros_k, ikt], axis=0)
        acc = jnp.zeros((IDX_Q, IDX_KT), F32)
        for p in range(IDX_HEADS // 2):
            xq = iqb[:, 2 * IDX_DIM * p:2 * IDX_DIM * (p + 1)]
            s0 = jnp.dot(xq, ik_even, preferred_element_type=F32)
            s1 = jnp.dot(xq, ik_odd, preferred_element_type=F32)
            acc = acc + iw[:, 2 * p:2 * p + 1] * jnp.maximum(s0, 0.0)
            acc = acc + iw[:, 2 * p + 1:2 * p + 2] * jnp.maximum(s1, 0.0)
        bits = lax.bitcast_convert_type(acc + 0.0, I32)
        key = jnp.where(bits < 0, bits ^ jnp.int32(0x7FFFFFFF), bits)
        key = jnp.where(tt * IDX_KT + kiota < limit, key, jnp.int32(INT_MIN))
        key_scr[tt] = key
        hi_scr[tt] = lax.shift_right_arithmetic(key, 16).astype(I16)
        lo_scr[tt] = ((key & jnp.int32(0xFFFF)) - jnp.int32(HALF16)).astype(I16)
        return carry

    lax.fori_loop(0, n_tiles, score_tile, 0)

    def count(pred):
        def body(tt, acc):
            w = jnp.where(pred(key_scr[tt], tt * IDX_KT + kiota), 1.0, 0.0)
            for s in range(IDX_KT // LANES):
                acc = acc + w[:, s * LANES:(s + 1) * LANES]
            return acc
        acc = lax.fori_loop(0, n_tiles, body, jnp.zeros((IDX_Q, LANES), F32))
        return jnp.sum(acc, axis=1, keepdims=True)

    def count16(ref, pred):
        def body(tt, acc):
            w = jnp.where(pred(ref[tt]), jnp.int16(1), jnp.int16(0))
            for s in range(IDX_KT // LANES):
                acc = acc + w[:, s * LANES:(s + 1) * LANES]
            return acc
        acc = lax.fori_loop(0, n_tiles, body, jnp.zeros((IDX_Q, LANES), I16))
        return jnp.sum(acc.astype(F32), axis=1, keepdims=True)

    def tile16(v):
        return jnp.concatenate([jnp.broadcast_to(v, (IDX_Q, LANES)).astype(I16)] * (IDX_KT // LANES), axis=1)

    def bisect16(ref, n_above):
        t16 = jnp.full((IDX_Q, 1), -HALF16, I32)
        for bit in range(15, -1, -1):
            cand = jnp.zeros_like(t16) if bit == 15 else t16 + jnp.int32(1 << bit)
            ct = tile16(cand)
            t16 = jnp.where(n_above + count16(ref, lambda v, ct=ct: v >= ct) >= top_k, cand, t16)
        return t16

    thr_hi = bisect16(hi_scr, 0.0)
    th = tile16(thr_hi)
    n_hi_gt = count16(hi_scr, lambda v: v > th)

    def keep_lo(tt, carry):
        lo_scr[tt] = jnp.where(hi_scr[tt] == th, lo_scr[tt], jnp.int16(-HALF16))
        return carry

    lax.fori_loop(0, n_tiles, keep_lo, 0)
    thr = thr_hi * jnp.int32(2 * HALF16) + (bisect16(lo_scr, n_hi_gt) + jnp.int32(HALF16))

    n_gt = count(lambda k, pos: k > thr)
    n_ge = count(lambda k, pos: k >= thr)
    room = top_k - n_gt
    pos_scr[...] = jnp.full((IDX_Q, 1), 2 * n_kt * IDX_KT, I32)

    @pl.when(jnp.max(jnp.where(thr > INT_MIN, n_ge, 0.0)) > top_k)
    def _():
        pcut = jnp.zeros((IDX_Q, 1), I32)
        for bit in range(int(math.log2(n_kt * IDX_KT)), -1, -1):
            cand = pcut + jnp.int32(1 << bit)
            n_tie = count(lambda k, pos, cand=cand: (k == thr) & (pos < cand))
            pcut = jnp.where(n_tie <= room, cand, pcut)
        pos_scr[...] = pcut

    pcut = pos_scr[...]

    def write_tile(tt, carry):
        k = key_scr[tt]
        pos = tt * IDX_KT + kiota
        sel = (k > thr) | ((k == thr) & (pos < pcut))
        sel = sel & (k > INT_MIN)
        mask_ref[tt] = jnp.where(sel, 1, 0).astype(jnp.int8)
        return carry

    lax.fori_loop(0, n_tiles, write_tile, 0)

    def zero_tile(tt, carry):
        mask_ref[tt] = jnp.zeros((IDX_Q, IDX_KT), jnp.int8)
        return carry

    lax.fori_loop(n_tiles, n_kt, zero_tile, 0)


def _indexer(proj, iq_blk, iw, ik_tiles, bsz, t):
    nq = t // IDX_Q
    n_kt = t // IDX_KT
    top_k = min(TOPK_MAX, t // 4)
    return pl.pallas_call(
        functools.partial(_indexer_body, top_k=top_k),
        grid=(bsz, nq),
        in_specs=[
            pl.BlockSpec((IDX_Q, IDX_HEADS * IDX_DIM), lambda b, i: (b * nq + i, iq_blk)),
            pl.BlockSpec((IDX_Q, IDX_HEADS), lambda b, i: (b * nq + i, 0)),
            pl.BlockSpec((None, n_kt, IDX_DIM, IDX_KT), lambda b, i: (b, 0, 0, 0)),
        ],
        out_specs=pl.BlockSpec((None, n_kt, IDX_Q, IDX_KT), lambda b, i: (b * nq + i, 0, 0, 0)),
        out_shape=jax.ShapeDtypeStruct((bsz * nq, n_kt, IDX_Q, IDX_KT), jnp.int8),
        scratch_shapes=[pltpu.VMEM((n_kt, IDX_Q, IDX_KT), I32), pltpu.VMEM((n_kt, IDX_Q, IDX_KT), I16),
                        pltpu.VMEM((n_kt, IDX_Q, IDX_KT), I16), pltpu.VMEM((IDX_Q, 1), I32)],
        compiler_params=_cparams(("parallel", "arbitrary"), 40),
        name="dsa_indexer",
    )(proj, iw, ik_tiles)


NEG_BIG = -3.0e38
LOG2_E = math.log2(math.e)


def _attn_body(q_ref, k_ref, v_ref, mask_ref, z_ref, tb_ref, o_ref, acc_scr, m_scr):
    T = ATT_T
    D = ATT_HEAD_DIM
    nsub = T // ATT_SUB
    i = pl.program_id(1)
    j = pl.program_id(2)

    @pl.when(j == 0)
    def _():
        acc_scr[...] = jnp.zeros_like(acc_scr)
        m_scr[...] = jnp.full_like(m_scr, NEG_BIG)

    def tile_step(near):
        sel = jnp.concatenate([mask_ref[a, 0].astype(F32) for a in range(mask_ref.shape[0])], axis=0)
        mask_bias = (sel - 1.0) * (-NEG_BIG)
        ones = jnp.ones((T, D), BF16)

        def pair_body(p, carry):
            q2, k2, v2 = q_ref[p], k_ref[p], v_ref[p]
            m_old = [m_scr[2 * p + e] for e in range(2)]
            a_old = [acc_scr[2 * p + e] for e in range(2)]
            new = []
            logits = [_bdot_nt(q2[:, D * e:D * (e + 1)], k2[:, D * e:D * (e + 1)]) for e in range(2)]
            for e in range(2):
                s = logits[e] + mask_bias
                if near is not None:
                    tb = tb_ref[2 * p + e]
                    zero = jnp.zeros((ATT_SUB, ATT_SUB), F32)
                    rows = []
                    for a in range(nsub):
                        cols = []
                        for bb in range(nsub):
                            rel_blocks = (bb - a) if near == "diag" else (bb - a - nsub)
                            cols.append(tb[0] if rel_blocks == 0 else tb[1] if rel_blocks == -1 else zero)
                        rows.append(jnp.concatenate(cols, axis=1))
                    s = s + jnp.concatenate(rows, axis=0)
                m_new = jnp.maximum(m_old[e], jnp.max(s, axis=1, keepdims=True))
                alpha = jnp.exp2(m_old[e] - m_new)
                pr = jnp.exp2(s - jnp.concatenate([m_new] * (T // LANES), axis=1))
                v_aug = jnp.concatenate([v2[:, D * e:D * (e + 1)], ones], axis=1)
                a_new = jnp.concatenate([alpha] * (2 * D // LANES), axis=1) * a_old[e] + _bdot(pr, v_aug)
                new.append((m_new, a_new))
            for e in range(2):
                m_scr[2 * p + e] = new[e][0]
                acc_scr[2 * p + e] = new[e][1]
            return carry

        lax.fori_loop(0, ATT_HEADS // 2, pair_body, 0, unroll=8)

    @pl.when(j < i - 1)
    def _():
        tile_step(None)

    @pl.when(j == i - 1)
    def _():
        tile_step("prev")

    @pl.when(j == i)
    def _():
        tile_step("diag")
        for p in range(ATT_HEADS // 2):
            for e in range(2):
                hh = 2 * p + e
                a = acc_scr[hh]
                o = a[:, :D] / a[:, D:]
                o = o * _silu(z_ref[p, :, D * e:D * (e + 1)].astype(F32))
                o_ref[:, D * hh:D * (hh + 1)] = o.astype(o_ref.dtype)


def _attention(qz, kv, mask, tb, bsz, t):
    T = ATT_T
    nq = t // T
    npair = ATT_HEADS // 2
    pw = 2 * ATT_HEAD_DIM
    nrow = T // IDX_Q
    ncol = T // IDX_KT
    return pl.pallas_call(
        _attn_body,
        grid=(bsz, nq, nq),
        in_specs=[
            pl.BlockSpec((npair, T, pw), lambda b, i, j: (0, b * nq + i, 0)),
            pl.BlockSpec((npair, T, pw), lambda b, i, j: (0, b * nq + jnp.minimum(j, i), 0)),
            pl.BlockSpec((npair, T, pw), lambda b, i, j: (1, b * nq + jnp.minimum(j, i), 0)),
            pl.BlockSpec((nrow, ncol, IDX_Q, IDX_KT), lambda b, i, j: (b * nq + i, jnp.minimum(j, i), 0, 0)),
            pl.BlockSpec((npair, T, pw), lambda b, i, j: (1, b * nq + i, 0)),
            pl.BlockSpec((ATT_HEADS, 2, ATT_SUB, ATT_SUB), lambda b, i, j: (0, 0, 0, 0)),
        ],
        out_specs=pl.BlockSpec((T, ATT_HEADS * ATT_HEAD_DIM), lambda b, i, j: (b * nq + i, 0)),
        out_shape=jax.ShapeDtypeStruct((bsz * t, ATT_HEADS * ATT_HEAD_DIM), BF16),
        scratch_shapes=[pltpu.VMEM((ATT_HEADS, T, 2 * ATT_HEAD_DIM), F32), pltpu.VMEM((ATT_HEADS, T, LANES), F32)],
        compiler_params=_cparams(("parallel", "parallel", "arbitrary"), 56),
        name="dsa_attention",
    )(qz, kv, kv, mask, qz, tb)


def _t5_bucket(rel):
    nb = REL_BUCKETS // 2
    max_exact = nb // 2
    ret = jnp.where(rel > 0, nb, 0)
    n = jnp.abs(rel)
    nf = jnp.maximum(n, max_exact).astype(F32)
    large = max_exact + (jnp.log(nf / max_exact) / math.log(REL_MAX_DIST / max_exact)
                         * (nb - max_exact)).astype(jnp.int32)
    large = jnp.minimum(large, nb - 1)
    return ret + jnp.where(n < max_exact, n, large)


def _rel_bias_blocks(rel_bias):
    r = jnp.arange(ATT_SUB)[:, None]
    c = jnp.arange(ATT_SUB)[None, :]
    rel = jnp.stack([c - r, c - r - ATT_SUB])
    assert ATT_SUB >= REL_MAX_DIST and ATT_SUB % DSA_CHUNK == 0
    rb = rel_bias.astype(F32)
    far = rb[_t5_bucket(jnp.int32(-2 * ATT_SUB))]
    onehot = (_t5_bucket(rel)[..., None] == jnp.arange(REL_BUCKETS)).astype(F32)
    table = jnp.einsum("vrcb,bh->hvrc", onehot, rb - far, precision=HI)
    return table * LOG2_E


GDN_PREP_ROWS = 256
GDN_SOLVE_LANES = 256


def _conv_with_prev(x, prev, w):
    xx = jnp.concatenate([prev, x], axis=0)
    y = w[CONV_WIDTH - 1:CONV_WIDTH, :] * x
    for j in range(CONV_WIDTH - 1):
        y = y + w[j:j + 1, :] * pltpu.roll(xx, CONV_WIDTH - 1 - j, axis=0)[8:, :]
    return y


def _gdn_prep_body(v_ref, q_ref, k_ref, pv_ref, pq_ref, pk_ref, bc_ref, ac_ref, ar_ref, wq_ref, wk_ref, wv_ref,
                   dtbc_ref, dtbr_ref, alc_ref, alr_ref,
                   qn_ref, rhs_ref, lu_ref, a_ref, egc_ref, egl_ref):
    L = GDN_CHUNK
    D = GDN_HEAD_DIM
    nv = GDN_V_HEADS // GDN_GROUPS
    rep = GDN_V_HEADS // GDN_QK_HEADS
    seq_start = pl.program_id(2) == 0

    def conv(x_ref, p_ref, w_ref):
        prev = jnp.where(seq_start, 0.0, p_ref[...])
        return _silu(_conv_with_prev(x_ref[...], prev, w_ref[...]))

    qc = conv(q_ref, pq_ref, wq_ref)
    kc = conv(k_ref, pk_ref, wk_ref)
    vc = conv(v_ref, pv_ref, wv_ref)
    qn, kn = [], []
    for kh in range(nv // rep):
        qh = qc[:, D * kh:D * (kh + 1)]
        kk_ = kc[:, D * kh:D * (kh + 1)]
        qh = qh * lax.rsqrt(jnp.sum(qh * qh, axis=-1, keepdims=True) + EPS) * (D ** -0.5)
        kn.append(kk_ * lax.rsqrt(jnp.sum(kk_ * kk_, axis=-1, keepdims=True) + EPS))
        qn.append(qh.astype(BF16))
        qn_ref[:, D * kh:D * (kh + 1)] = qn[kh]

    beta = _sigmoid(bc_ref[...])
    g_col = -jnp.exp(alc_ref[...]) * _softplus(ac_ref[...] + dtbc_ref[...])
    ri = lax.broadcasted_iota(I32, (L, L), 0)
    ci = lax.broadcasted_iota(I32, (L, L), 1)
    causal = ci <= ri
    strict = ci < ri
    for cc in range(a_ref.shape[0]):
        rows = slice(L * cc, L * (cc + 1))
        gc_col = _dot_sel_lhs(causal, g_col[rows, :])
        g_row = -jnp.exp(alr_ref[...]) * _softplus(ar_ref[cc] + dtbr_ref[...])
        gc_row = _dot_sel_rhs(g_row, ri <= ci)
        eg_col = jnp.exp(gc_col)
        egc_ref[rows, :] = eg_col
        for kh in range(nv // rep):
            kf = kn[kh][rows, :]
            kb = kf.astype(BF16)
            kk = _bdot_nt(kb, kb)
            qk = _bdot_nt(qn[kh][rows, :], kb)
            for e in range(rep):
                vh = rep * kh + e
                gcol = gc_col[:, vh:vh + 1]
                bcol = beta[rows, vh:vh + 1]
                dec = jnp.exp(jnp.where(causal, gcol - gc_row[vh:vh + 1, :], -jnp.inf))
                a_ref[cc, vh] = jnp.where(strict, bcol * kk * dec, 0.0)
                g_last = gcol[L - 1:L, :]
                kw_t = jnp.transpose(kf * jnp.exp(g_last - gcol))
                lu_ref[cc, vh] = jnp.concatenate([qk * dec, kw_t], axis=0).astype(lu_ref.dtype)
                egl_ref[cc, vh:vh + 1, :] = jnp.broadcast_to(jnp.exp(g_last), (1, LANES))
                rhs_ref[rows, 2 * D * vh:2 * D * (vh + 1)] = jnp.concatenate(
                    [vc[rows, D * vh:D * (vh + 1)] * bcol, kf * (bcol * eg_col[:, vh:vh + 1])], axis=1
                ).astype(rhs_ref.dtype)


def _gdn_solve_body(a_ref, t_ref, t_scr):
    L, _, PB = a_ref.shape
    UN = 8
    sub = lax.broadcasted_iota(I32, (UN, PB), 0)
    t_scr[...] = jnp.zeros_like(t_scr)

    for ib in range(L // UN):
        def row_body(r, carry, ib=ib):
            i = ib * UN + r
            acc = [jnp.zeros((UN, PB), F32) for _ in range(ib + 1)]
            for jb in range(ib + 1):
                a_blk = a_ref[i, UN * jb:UN * (jb + 1), :]
                for jj in range(UN):
                    a_ij = a_blk[jj:jj + 1, :]
                    for cg in range(jb + 1):
                        acc[cg] = acc[cg] + a_ij * t_scr[UN * jb + jj, UN * cg:UN * (cg + 1), :]
            for cg in range(ib + 1):
                unit = jnp.where(sub == r, 1.0, 0.0) if cg == ib else 0.0
                t_scr[i, UN * cg:UN * (cg + 1), :] = unit - acc[cg]
            return carry

        lax.fori_loop(0, UN, row_body, 0)
    t_ref[...] = t_scr[...].astype(t_ref.dtype)


def _gdn_scan_body(qn_ref, rhs_ref, lu_ref, t_ref, z_ref, egc_ref, egl_ref, gain_ref, y_ref, s_scr):
    L = qn_ref.shape[0]
    D = GDN_HEAD_DIM
    nv = GDN_V_HEADS // GDN_GROUPS
    rep = GDN_V_HEADS // GDN_QK_HEADS

    @pl.when(pl.program_id(2) == 0)
    def _():
        s_scr[...] = jnp.zeros_like(s_scr)

    eg_col = egc_ref[...]
    z = z_ref[...]
    heads = range(nv)
    sol = [_bdot(t_ref[vh], rhs_ref[:, 2 * D * vh:2 * D * (vh + 1)]) for vh in heads]
    s_st = [s_scr[vh] for vh in heads]
    r_s = [_bdot(jnp.concatenate([sol[vh][:, D:].astype(BF16),
                                  qn_ref[:, D * (vh // rep):D * (vh // rep + 1)]], axis=0), s_st[vh])
           for vh in heads]
    u = [sol[vh][:, :D] - r_s[vh][:L] for vh in heads]
    r_u = [_bdot(lu_ref[vh], u[vh]) for vh in heads]
    for vh in heads:
        s_scr[vh] = egl_ref[vh:vh + 1, :] * s_st[vh] + r_u[vh][L:]
        o = r_s[vh][L:] * eg_col[:, vh:vh + 1] + r_u[vh][:L]
        o = o * lax.rsqrt(jnp.mean(o * o, axis=-1, keepdims=True) + EPS) * gain_ref[...]
        y_ref[:, D * vh:D * (vh + 1)] = (o * _silu(z[:, D * vh:D * (vh + 1)])).astype(y_ref.dtype)


def _gdn(proj, b_col, a_col, a_row, conv_w, dt_bias, a_log, gain, bsz, t):
    L = GDN_CHUNK
    nc = t // L
    G = GDN_GROUPS
    nv = GDN_V_HEADS // G
    qw = GDN_QK_HEADS // G * GDN_HEAD_DIM
    vw = nv * GDN_HEAD_DIM
    m = bsz * t
    z_blk, q_blk, k_blk = 4096 // vw, 8192 // qw, 10240 // qw
    f = lambda a: a.astype(F32)
    cw = f(conv_w)
    pc_spec = pl.BlockSpec((None, 1, nv), lambda b, g, c: (g, 0, 0))
    pr_spec = pl.BlockSpec((None, nv, 1), lambda b, g, c: (g, 0, 0))
    wq_spec = pl.BlockSpec((CONV_WIDTH, qw), lambda b, g, c: (0, g))
    wk_spec = pl.BlockSpec((CONV_WIDTH, qw), lambda b, g, c: (0, G + g))
    wv_spec = pl.BlockSpec((CONV_WIDTH, vw), lambda b, g, c: (0, (2 * G * qw) // vw + g))

    LB = min(GDN_PREP_ROWS, t)
    ncb = LB // L
    nb = t // LB
    rowb = lambda b, g, c: b * nb + c
    prev8 = lambda b, g, c: jnp.maximum((b * t + c * LB) // 8 - 1, 0)
    D = GDN_HEAD_DIM
    qn, rhs, lu, a_mat, egc, egl = pl.pallas_call(
        _gdn_prep_body,
        grid=(bsz, G, nb),
        in_specs=[
            pl.BlockSpec((LB, vw), lambda b, g, c: (rowb(b, g, c), g)),
            pl.BlockSpec((LB, qw), lambda b, g, c: (rowb(b, g, c), q_blk + g)),
            pl.BlockSpec((LB, qw), lambda b, g, c: (rowb(b, g, c), k_blk + g)),
            pl.BlockSpec((8, vw), lambda b, g, c: (prev8(b, g, c), g)),
            pl.BlockSpec((8, qw), lambda b, g, c: (prev8(b, g, c), q_blk + g)),
            pl.BlockSpec((8, qw), lambda b, g, c: (prev8(b, g, c), k_blk + g)),
            pl.BlockSpec((None, None, LB, nv), lambda b, g, c: (b, g, c, 0)),
            pl.BlockSpec((None, None, LB, nv), lambda b, g, c: (b, g, c, 0)),
            pl.BlockSpec((None, None, ncb, nv, L), lambda b, g, c: (b, g, c, 0, 0)),
            wq_spec, wk_spec, wv_spec, pc_spec, pr_spec, pc_spec, pr_spec,
        ],
        out_specs=[
            pl.BlockSpec((LB, qw), lambda b, g, c: (rowb(b, g, c), g)),
            pl.BlockSpec((LB, 2 * vw), lambda b, g, c: (rowb(b, g, c), g)),
            pl.BlockSpec((None, None, ncb, nv, L + D, L), lambda b, g, c: (b, g, c, 0, 0, 0)),
            pl.BlockSpec((None, None, ncb, nv, L, L), lambda b, g, c: (b, g, c, 0, 0, 0)),
            pl.BlockSpec((None, None, LB, nv), lambda b, g, c: (b, g, c, 0)),
            pl.BlockSpec((None, None, ncb, nv, LANES), lambda b, g, c: (b, g, c, 0, 0)),
        ],
        out_shape=[
            jax.ShapeDtypeStruct((m, G * qw), BF16),
            jax.ShapeDtypeStruct((m, 2 * G * vw), BF16),
            jax.ShapeDtypeStruct((bsz, G, nc, nv, L + D, L), BF16),
            jax.ShapeDtypeStruct((bsz, G, nc, nv, L, L), F32),
            jax.ShapeDtypeStruct((bsz, G, t, nv), F32),
            jax.ShapeDtypeStruct((bsz, G, nc, nv, LANES), F32),
        ],
        compiler_params=_cparams(("parallel", "parallel", "parallel"), 48),
        name="gdn_prep",
    )(proj, proj, proj, proj, proj, proj, b_col, a_col, a_row, cw, cw, cw,
      f(dt_bias).reshape(G, 1, nv), f(dt_bias).reshape(G, nv, 1), f(a_log).reshape(G, 1, nv),
      f(a_log).reshape(G, nv, 1))

    n_prob = bsz * G * nc * nv
    pb = min(GDN_SOLVE_LANES, n_prob)
    a_t = jnp.transpose(a_mat.reshape(n_prob, L, L), (1, 2, 0))
    t_t = pl.pallas_call(
        _gdn_solve_body,
        grid=(n_prob // pb,),
        in_specs=[pl.BlockSpec((L, L, pb), lambda p: (0, 0, p))],
        out_specs=pl.BlockSpec((L, L, pb), lambda p: (0, 0, p)),
        out_shape=jax.ShapeDtypeStruct((L, L, n_prob), BF16),
        scratch_shapes=[pltpu.VMEM((L, L, pb), F32)],
        compiler_params=_cparams(("parallel",), 40),
        name="gdn_solve",
    )(a_t)
    t_mat = jnp.transpose(t_t, (2, 0, 1)).reshape(bsz, G, nc, nv, L, L)

    row = lambda b, g, c: b * nc + c
    return pl.pallas_call(
        _gdn_scan_body,
        grid=(bsz, G, nc),
        in_specs=[
            pl.BlockSpec((L, qw), lambda b, g, c: (row(b, g, c), g)),
            pl.BlockSpec((L, 2 * vw), lambda b, g, c: (row(b, g, c), g)),
            pl.BlockSpec((None, None, None, nv, L + D, L), lambda b, g, c: (b, g, c, 0, 0, 0)),
            pl.BlockSpec((None, None, None, nv, L, L), lambda b, g, c: (b, g, c, 0, 0, 0)),
            pl.BlockSpec((L, vw), lambda b, g, c: (row(b, g, c), z_blk + g)),
            pl.BlockSpec((None, None, L, nv), lambda b, g, c: (b, g, c, 0)),
            pl.BlockSpec((None, None, None, nv, LANES), lambda b, g, c: (b, g, c, 0, 0)),
            pl.BlockSpec((1, GDN_HEAD_DIM), lambda b, g, c: (0, 0)),
        ],
        out_specs=pl.BlockSpec((L, vw), lambda b, g, c: (row(b, g, c), g)),
        out_shape=jax.ShapeDtypeStruct((m, GDN_V_HEADS * GDN_HEAD_DIM), BF16),
        scratch_shapes=[pltpu.VMEM((nv, GDN_HEAD_DIM, GDN_HEAD_DIM), F32)],
        compiler_params=_cparams(("parallel", "parallel", "arbitrary"), 40),
        name="gdn_scan",
    )(qn, rhs, lu, t_mat, proj, egc, egl, f(gain).reshape(1, GDN_HEAD_DIM))


def _layer_cd(x2, bsz, t, norm_g, w_in, kv_norm, w_uk, w_uv, gdn_conv_w, gdn_dt_bias, gdn_a_log, gdn_norm,
              w_out, rel_bias, final_g, final_norm):
    m = bsz * t
    small = jnp.concatenate([w_in[:, 5632:5712], w_in[:, 18000:18064]], axis=1)
    w_main = jnp.concatenate(
        [w_in[:, 9808:13904], w_in[:, 13904:18000], w_in[:, 5712:9808], w_in[:, 4608:5632], w_in[:, 2048:2560],
         jnp.pad(small, ((0, 0), (0, 512 - small.shape[1])))], axis=1).astype(BF16)
    proj = _norm_matmul(x2, 0, D_MODEL, norm_g, w_main, out_dtype=F32, tn=1024, name="proj_cd")
    w_attn = jnp.concatenate([w_in[:, :2048] * (ATT_HEAD_DIM ** -0.5 * LOG2_E), w_in[:, 2560:4608]],
                             axis=1).astype(BF16)
    qz = _norm_matmul(x2, 0, D_MODEL, norm_g, w_attn, out_dtype=BF16, blocked_width=2 * ATT_HEAD_DIM,
                      tn=1024, name="proj_cd_attn")
    w_kv = jnp.concatenate([w_uk.reshape(KV_RANK, -1), w_uv.reshape(KV_RANK, -1)], axis=1).astype(BF16)
    kv = _norm_matmul(proj, 13312 // KV_RANK, KV_RANK, kv_norm, w_kv, out_dtype=BF16,
                      blocked_width=2 * ATT_HEAD_DIM, tn=1024, name="dsa_kv")

    gate_cols = proj[:, 13824:13824 + 144]
    n_kt = t // IDX_KT
    ik_tiles = jnp.transpose(gate_cols[:, :IDX_DIM].reshape(bsz, n_kt, IDX_KT, IDX_DIM), (0, 1, 3, 2)).astype(BF16)
    iw = gate_cols[:, IDX_DIM:IDX_DIM + IDX_HEADS]
    mask = _indexer(proj, 12288 // (IDX_HEADS * IDX_DIM), iw, ik_tiles, bsz, t)
    y_c = _attention(qz, kv, mask, _rel_bias_blocks(rel_bias), bsz, t)

    nv = GDN_V_HEADS // GDN_GROUPS
    b_pre = gate_cols[:, 80:112].reshape(bsz, t, GDN_GROUPS, nv)
    a_pre = gate_cols[:, 112:144].reshape(bsz, t, GDN_GROUPS, nv)
    col = lambda a: jnp.transpose(a, (0, 2, 1, 3))
    rowf = lambda a: jnp.transpose(a.reshape(bsz, t // GDN_CHUNK, GDN_CHUNK, GDN_GROUPS, nv), (0, 3, 1, 4, 2))
    y_d = _gdn(proj, col(b_pre), col(a_pre), rowf(a_pre), gdn_conv_w, gdn_dt_bias, gdn_a_log, gdn_norm, bsz, t)
    return _out_proj([y_c, y_d], w_out.astype(BF16), x2, final_g, final_norm=final_norm, name="out_cd")


def kernel(x, rel_bias, ab_norm, ab_w_in, ab_ml_b_i, ab_ml_b_f, ab_ml_norm, ab_ssm_conv_w, ab_ssm_conv_b, ab_ssm_dt_bias, ab_ssm_a_log, ab_ssm_d, ab_ssm_norm, ab_w_out, cd_norm, cd_w_in, cd_kv_norm, cd_w_uk, cd_w_uv, cd_gdn_conv_w, cd_gdn_dt_bias, cd_gdn_a_log, cd_gdn_norm, cd_w_out, final_norm):
    bsz, t, d = x.shape
    x2 = x.reshape(bsz * t, d)
    x2 = _layer_ab(x2, bsz, t, ab_norm[0], ab_w_in[0], ab_ml_b_i[0], ab_ml_b_f[0], ab_ml_norm[0],
                   ab_ssm_conv_w[0], ab_ssm_conv_b[0], ab_ssm_dt_bias[0], ab_ssm_a_log[0], ab_ssm_d[0],
                   ab_ssm_norm[0], ab_w_out[0])
    x2 = _layer_cd(x2, bsz, t, cd_norm[0], cd_w_in[0], cd_kv_norm[0], cd_w_uk[0], cd_w_uv[0], cd_gdn_conv_w[0],
                   cd_gdn_dt_bias[0], cd_gdn_a_log[0], cd_gdn_norm[0], cd_w_out[0], rel_bias, final_norm, True)
    return x2.reshape(bsz, t, d)
```

```python
import functools
import math

import jax
import jax.numpy as jnp
from jax import lax
from jax.experimental import pallas as pl
from jax.experimental.pallas import tpu as pltpu

F32 = jnp.float32
BF16 = jnp.bfloat16
I32 = jnp.int32
I16 = jnp.int16
HI = lax.Precision.HIGHEST

EPS = 1e-6
CONV_WIDTH = 4
D_MODEL = 2048

ML_HEADS = 4
ML_DV = 512
ML_DK = 256

SSM_GROUPS = 4
SSM_HEADS_PER_GROUP = 8
SSM_HEAD_DIM = 64
SSM_STATE = 128
SSM_GROUP_WIDTH = SSM_HEADS_PER_GROUP * SSM_HEAD_DIM

ATT_HEADS = 16
ATT_HEAD_DIM = 128
KV_RANK = 512
IDX_HEADS = 16
IDX_DIM = 64
TOPK_MAX = 256
DSA_CHUNK = 64
REL_BUCKETS = 32
REL_MAX_DIST = 128

GDN_QK_HEADS = 16
GDN_V_HEADS = 32
GDN_HEAD_DIM = 128
GDN_GROUPS = 2
GDN_CHUNK = 64

VMEM_BYTES_V7X = 64 * 1024 * 1024
LANES = 128

ML_CHUNK = 256
SSD_CHUNK = 256
IDX_Q = 256
IDX_KT = 512
ATT_T = 512
ATT_SUB = 128
INT_MIN = -(2 ** 31)
HALF16 = 2 ** 15


def _cparams(sem, vmem_mib):
    return pltpu.CompilerParams(dimension_semantics=sem, vmem_limit_bytes=vmem_mib * 1024 * 1024)


def _bdot(a, b):
    return jnp.dot(a.astype(BF16), b.astype(BF16), preferred_element_type=F32)


def _bdot_nt(a, b):
    return lax.dot_general(a.astype(BF16), b.astype(BF16), (((1,), (1,)), ((), ())), preferred_element_type=F32)


def _bdot_tn(a, b):
    return lax.dot_general(a.astype(BF16), b.astype(BF16), (((0,), (0,)), ((), ())), preferred_element_type=F32)


def _split3(x):
    hi = x.astype(BF16)
    r = x - hi.astype(F32)
    mid = r.astype(BF16)
    return hi, mid, (r - mid.astype(F32)).astype(BF16)


def _as_bf16_01(sel):
    return jnp.where(sel, 1.0, 0.0).astype(BF16) if sel.dtype == jnp.bool_ else sel.astype(BF16)


def _dot_sel_lhs(sel, x):
    s = _as_bf16_01(sel)
    return sum(jnp.dot(s, p, preferred_element_type=F32) for p in _split3(x))


def _dot_sel_rhs(x, sel):
    s = _as_bf16_01(sel)
    return sum(jnp.dot(p, s, preferred_element_type=F32) for p in _split3(x))


def _softplus(x):
    return jnp.maximum(x, 0.0) + jnp.log1p(jnp.exp(-jnp.abs(x)))


def _sigmoid(x):
    return 1.0 / (1.0 + jnp.exp(-x))


def _silu(x):
    return x * _sigmoid(x)


def _norm_matmul_body(x_ref, g_ref, w_ref, o_ref, h_ref, *, blocked):
    @pl.when(pl.program_id(1) == 0)
    def _():
        x = x_ref[...].astype(F32)
        ms = jnp.mean(x * x, axis=-1, keepdims=True)
        h_ref[...] = (x * lax.rsqrt(ms + EPS) * g_ref[...]).astype(h_ref.dtype)

    acc = jnp.dot(h_ref[...], w_ref[...], preferred_element_type=F32)
    if blocked:
        bw = o_ref.shape[-1]
        for t in range(o_ref.shape[0]):
            o_ref[t] = acc[:, t * bw:(t + 1) * bw].astype(o_ref.dtype)
    else:
        o_ref[...] = acc.astype(o_ref.dtype)


def _norm_matmul(x, x_col_block, k_dim, gain, w, *, out_dtype, blocked_width=None, tm=1024, tn=512, name):
    m = x.shape[0]
    n = w.shape[1]
    tm = min(tm, m)
    assert m % tm == 0 and n % tn == 0 and w.shape[0] == k_dim
    grid = (m // tm, n // tn)
    if blocked_width is None:
        out_shape = jax.ShapeDtypeStruct((m, n), out_dtype)
        out_spec = pl.BlockSpec((tm, tn), lambda i, j: (i, j))
    else:
        nb = tn // blocked_width
        out_shape = jax.ShapeDtypeStruct((n // blocked_width, m, blocked_width), out_dtype)
        out_spec = pl.BlockSpec((nb, tm, blocked_width), lambda i, j: (j, i, 0))
    return pl.pallas_call(
        functools.partial(_norm_matmul_body, blocked=blocked_width is not None),
        grid=grid,
        in_specs=[
            pl.BlockSpec((tm, k_dim), lambda i, j: (i, x_col_block), pipeline_mode=pl.Buffered(1)),
            pl.BlockSpec((1, k_dim), lambda i, j: (0, 0)),
            pl.BlockSpec((k_dim, tn), lambda i, j: (0, j)),
        ],
        out_specs=out_spec,
        out_shape=out_shape,
        scratch_shapes=[pltpu.VMEM((tm, k_dim), BF16)],
        compiler_params=_cparams(("parallel", "arbitrary"), 48),
        name=name,
    )(x, gain.reshape(1, k_dim).astype(F32), w)


def _out_proj_body(*refs, n_y, k_tiles, final_norm):
    y_refs = refs[:n_y]
    w_ref, res_ref, g_ref, o_ref, acc_ref = refs[n_y:]
    k = pl.program_id(1)

    @pl.when(k == 0)
    def _():
        acc_ref[...] = jnp.zeros_like(acc_ref)

    lo = 0
    for y_ref, nk in zip(y_refs, k_tiles):
        @pl.when((k >= lo) & (k < lo + nk))
        def _(y_ref=y_ref):
            acc_ref[...] += jnp.dot(y_ref[...], w_ref[...], preferred_element_type=F32)
        lo += nk

    @pl.when(k == lo - 1)
    def _():
        xn = res_ref[...] + acc_ref[...]
        if final_norm:
            ms = jnp.mean(xn * xn, axis=-1, keepdims=True)
            xn = xn * lax.rsqrt(ms + EPS) * g_ref[...]
        o_ref[...] = xn


def _out_proj(ys, w, res, gain, *, final_norm, tm=512, tk=1024, name):
    m, d = res.shape
    tm = min(tm, m)
    k_tiles = [y.shape[1] // tk for y in ys]
    assert sum(k_tiles) * tk == w.shape[0]
    y_specs = []
    lo = 0
    for nk in k_tiles:
        y_specs.append(pl.BlockSpec((tm, tk), lambda i, k, lo=lo, nk=nk: (i, jnp.clip(k - lo, 0, nk - 1))))
        lo += nk
    return pl.pallas_call(
        functools.partial(_out_proj_body, n_y=len(ys), k_tiles=k_tiles, final_norm=final_norm),
        grid=(m // tm, lo),
        in_specs=y_specs + [
            pl.BlockSpec((tk, d), lambda i, k: (k, 0)),
            pl.BlockSpec((tm, d), lambda i, k: (i, 0)),
            pl.BlockSpec((1, d), lambda i, k: (0, 0)),
        ],
        out_specs=pl.BlockSpec((tm, d), lambda i, k: (i, 0)),
        out_shape=jax.ShapeDtypeStruct((m, d), F32),
        scratch_shapes=[pltpu.VMEM((tm, d), F32)],
        compiler_params=_cparams(("parallel", "arbitrary"), 48),
        name=name,
    )(*ys, w, res, gain.reshape(1, d).astype(F32))


def _mlstm_body(bi_ref, bf_ref, q_ref, k_ref, v_ref, op_ref, z_ref, g_ref, gain_ref, y_ref,
                c_scr, n_scr, m_scr):
    L = q_ref.shape[0]
    h = pl.program_id(1)

    @pl.when(pl.program_id(2) == 0)
    def _():
        c_scr[...] = jnp.zeros_like(c_scr)
        n_scr[...] = jnp.zeros_like(n_scr)
        m_scr[...] = jnp.zeros_like(m_scr)

    gates = g_ref[...]
    ic = gates[0:1, :] + bi_ref[h]
    fpre = gates[1:2, :] + bf_ref[h]
    fc = -_softplus(-fpre)

    ri = lax.broadcasted_iota(I32, (L, L), 0)
    ci = lax.broadcasted_iota(I32, (L, L), 1)
    causal = ci <= ri
    eye = ci == ri
    bcum_col = jnp.sum(jnp.where(causal, fc, 0.0), axis=1, keepdims=True)
    fc_col = jnp.sum(jnp.where(eye, fc, 0.0), axis=1, keepdims=True)
    bcum_row = jnp.sum(jnp.where(ri <= ci, fc_col, 0.0), axis=0, keepdims=True)

    m_st = m_scr[...]
    d = jnp.where(causal, bcum_col - bcum_row + ic, -jnp.inf)
    inter = bcum_col + m_st
    m_row = jnp.maximum(inter, jnp.max(d, axis=1, keepdims=True))
    w_inter = jnp.exp(inter - m_row)

    q = q_ref[...]
    k = k_ref[...] * (ML_DK ** -0.5)
    v = v_ref[...]
    qb = q.astype(BF16)
    vb = v.astype(BF16)
    p = jnp.exp(d - m_row) * _bdot_nt(qb, k)
    c_st = c_scr[...]
    n_st = n_scr[...]
    num = w_inter * _bdot(qb, c_st) + _bdot(p, vb)
    den = w_inter * jnp.sum(q * n_st, axis=1, keepdims=True) + jnp.sum(p, axis=1, keepdims=True)
    hout = num / jnp.maximum(jnp.abs(den), jnp.exp(-m_row))

    b_last = jnp.sum(fc, axis=1, keepdims=True)
    g_row = b_last - bcum_row + ic
    m_new = jnp.maximum(b_last + m_st, jnp.max(g_row, axis=1, keepdims=True))
    w_old = jnp.exp(b_last + m_st - m_new)
    w_in_col = jnp.sum(jnp.where(eye, jnp.exp(g_row - m_new), 0.0), axis=1, keepdims=True)
    kw = k * w_in_col
    c_scr[...] = w_old * c_st + _bdot_tn(kw, vb)
    n_scr[...] = w_old * n_st + jnp.sum(kw, axis=0, keepdims=True)
    m_scr[...] = m_new

    hn = hout * lax.rsqrt(jnp.mean(hout * hout, axis=-1, keepdims=True) + EPS) * gain_ref[...]
    y_ref[...] = (hn * _sigmoid(op_ref[...]) * _silu(z_ref[...])).astype(y_ref.dtype)


def _mlstm(proj, gates, b_i, b_f, gain, bsz, t):
    L = min(ML_CHUNK, t)
    nc = t // L
    row = lambda b, h, c: b * nc + c
    return pl.pallas_call(
        _mlstm_body,
        grid=(bsz, ML_HEADS, nc),
        in_specs=[
            pl.BlockSpec(memory_space=pltpu.SMEM),
            pl.BlockSpec(memory_space=pltpu.SMEM),
            pl.BlockSpec((L, ML_DK), lambda b, h, c: (row(b, h, c), h)),
            pl.BlockSpec((L, ML_DK), lambda b, h, c: (row(b, h, c), ML_HEADS + h)),
            pl.BlockSpec((L, ML_DV), lambda b, h, c: (row(b, h, c), ML_HEADS + h)),
            pl.BlockSpec((L, ML_DV), lambda b, h, c: (row(b, h, c), 2 * ML_HEADS + h)),
            pl.BlockSpec((L, ML_DV), lambda b, h, c: (row(b, h, c), 3 * ML_HEADS + h)),
            pl.BlockSpec((None, None, None, 2, L), lambda b, h, c: (b, h, c, 0, 0)),
            pl.BlockSpec((None, 1, ML_DV), lambda b, h, c: (h, 0, 0)),
        ],
        out_specs=pl.BlockSpec((L, ML_DV), lambda b, h, c: (row(b, h, c), h)),
        out_shape=jax.ShapeDtypeStruct((bsz * t, ML_HEADS * ML_DV), BF16),
        scratch_shapes=[pltpu.VMEM((ML_DK, ML_DV), F32), pltpu.VMEM((1, ML_DK), F32), pltpu.VMEM((1, 1), F32)],
        compiler_params=_cparams(("parallel", "parallel", "arbitrary"), 40),
        name="mlstm",
    )(b_i.astype(F32), b_f.astype(F32), proj, proj, proj, proj, proj, gates,
      gain.reshape(ML_HEADS, 1, ML_DV).astype(F32))


def _causal_conv(x, tail_ref, w):
    L = x.shape[0]
    xx = jnp.concatenate([tail_ref[...], x], axis=0)
    y = w[CONV_WIDTH - 1:CONV_WIDTH, :] * x
    for j in range(CONV_WIDTH - 1):
        y = y + w[j:j + 1, :] * pltpu.roll(xx, CONV_WIDTH - 1 - j, axis=0)[8:, :]
    tail_ref[...] = x[L - 8:, :]
    return y


def _head_expand(n_heads, width):
    r = lax.broadcasted_iota(I32, (n_heads, n_heads * width), 0)
    c = lax.broadcasted_iota(I32, (n_heads, n_heads * width), 1)
    return ((c >= r * width) & (c < (r + 1) * width)).astype(F32)


def _ssd_body(x_ref, bm_ref, cm_ref, z_ref, dtc_ref, dtr_ref, wx_ref, wb_ref, wc_ref, bx_ref, bb_ref, bc_ref,
              dtbc_ref, dtbr_ref, alc_ref, alr_ref, dsk_ref, gain_ref, y_ref, st_scr, tx_scr, tb_scr, tc_scr):
    L = x_ref.shape[0]
    nh, hd = SSM_HEADS_PER_GROUP, SSM_HEAD_DIM

    @pl.when(pl.program_id(2) == 0)
    def _():
        st_scr[...] = jnp.zeros_like(st_scr)
        tx_scr[...] = jnp.zeros_like(tx_scr)
        tb_scr[...] = jnp.zeros_like(tb_scr)
        tc_scr[...] = jnp.zeros_like(tc_scr)

    xc = _silu(_causal_conv(x_ref[...], tx_scr, wx_ref[...]) + bx_ref[...])
    bc = _silu(_causal_conv(bm_ref[...], tb_scr, wb_ref[...]) + bb_ref[...])
    cc = _silu(_causal_conv(cm_ref[...], tc_scr, wc_ref[...]) + bc_ref[...])

    ri = lax.broadcasted_iota(I32, (L, L), 0)
    ci = lax.broadcasted_iota(I32, (L, L), 1)
    causal = ci <= ri
    dt_col = _softplus(dtc_ref[...] + dtbc_ref[...])
    acum = _dot_sel_lhs(causal, dt_col * (-jnp.exp(alc_ref[...])))
    dt_row = _softplus(dtr_ref[...] + dtbr_ref[...])
    acum_row = _dot_sel_rhs(dt_row * (-jnp.exp(alr_ref[...])), ri <= ci)

    expand = _head_expand(nh, hd)
    acum_e = _dot_sel_rhs(acum, expand)
    xdt = xc * _dot_sel_rhs(dt_col, expand)
    xdt_b = xdt.astype(BF16)
    cb = _bdot_nt(cc, bc)

    lane = lax.broadcasted_iota(I32, (L, 2 * hd), 1)
    parts = []
    for hp in range(nh // 2):
        xpair = xdt_b[:, 2 * hd * hp:2 * hd * (hp + 1)]
        acc = None
        for e in range(2):
            hh = 2 * hp + e
            seg = acum[:, hh:hh + 1] - acum_row[hh:hh + 1, :]
            dec = jnp.exp(jnp.where(causal, seg, -jnp.inf))
            xsel = jnp.where((lane >= hd * e) & (lane < hd * (e + 1)), xpair, jnp.zeros_like(xpair))
            t = _bdot(cb * dec, xsel)
            acc = t if acc is None else acc + t
        parts.append(acc)
    y = jnp.concatenate(parts, axis=1)

    st = st_scr[...]
    y = y + _bdot(cc, st) * jnp.exp(acum_e)
    a_last = acum_e[L - 1:L, :]
    st_scr[...] = jnp.exp(a_last) * st + _bdot_tn(bc, xdt * jnp.exp(a_last - acum_e))

    y = y + xc * dsk_ref[...]
    y = y * _silu(z_ref[...])
    y = y * lax.rsqrt(jnp.mean(y * y, axis=-1, keepdims=True) + EPS) * gain_ref[...]
    y_ref[...] = y.astype(y_ref.dtype)


def _ssd(proj, dt_col, dt_row, conv_w, conv_b, dt_bias, a_log, d_skip, gain, bsz, t):
    L = min(SSD_CHUNK, t)
    nc = t // L
    G, nh, gw, ns = SSM_GROUPS, SSM_HEADS_PER_GROUP, SSM_GROUP_WIDTH, SSM_STATE
    row = lambda b, g, c: b * nc + c
    x_blk = 10240 // gw
    z_blk = 8192 // gw
    b_blk = 12288 // ns
    c_blk = 12800 // ns
    wb_blk = G * gw // ns
    conv_b2 = conv_b.reshape(1, -1).astype(F32)
    f = lambda a: a.astype(F32)
    return pl.pallas_call(
        _ssd_body,
        grid=(bsz, G, nc),
        in_specs=[
            pl.BlockSpec((L, gw), lambda b, g, c: (row(b, g, c), x_blk + g)),
            pl.BlockSpec((L, ns), lambda b, g, c: (row(b, g, c), b_blk + g)),
            pl.BlockSpec((L, ns), lambda b, g, c: (row(b, g, c), c_blk + g)),
            pl.BlockSpec((L, gw), lambda b, g, c: (row(b, g, c), z_blk + g)),
            pl.BlockSpec((None, None, L, nh), lambda b, g, c: (b, g, c, 0)),
            pl.BlockSpec((None, None, nh, L), lambda b, g, c: (b, g, 0, c)),
            pl.BlockSpec((CONV_WIDTH, gw), lambda b, g, c: (0, g)),
            pl.BlockSpec((CONV_WIDTH, ns), lambda b, g, c: (0, wb_blk + g)),
            pl.BlockSpec((CONV_WIDTH, ns), lambda b, g, c: (0, wb_blk + G + g)),
            pl.BlockSpec((1, gw), lambda b, g, c: (0, g)),
            pl.BlockSpec((1, ns), lambda b, g, c: (0, wb_blk + g)),
            pl.BlockSpec((1, ns), lambda b, g, c: (0, wb_blk + G + g)),
            pl.BlockSpec((None, 1, nh), lambda b, g, c: (g, 0, 0)),
            pl.BlockSpec((None, nh, 1), lambda b, g, c: (g, 0, 0)),
            pl.BlockSpec((None, 1, nh), lambda b, g, c: (g, 0, 0)),
            pl.BlockSpec((None, nh, 1), lambda b, g, c: (g, 0, 0)),
            pl.BlockSpec((None, 1, gw), lambda b, g, c: (g, 0, 0)),
            pl.BlockSpec((None, 1, gw), lambda b, g, c: (g, 0, 0)),
        ],
        out_specs=pl.BlockSpec((L, gw), lambda b, g, c: (row(b, g, c), g)),
        out_shape=jax.ShapeDtypeStruct((bsz * t, G * gw), BF16),
        scratch_shapes=[pltpu.VMEM((ns, gw), F32), pltpu.VMEM((8, gw), F32),
                        pltpu.VMEM((8, ns), F32), pltpu.VMEM((8, ns), F32)],
        compiler_params=_cparams(("parallel", "parallel", "arbitrary"), 40),
        name="ssd",
    )(proj, proj, proj, proj, dt_col, dt_row, f(conv_w), f(conv_w), f(conv_w), conv_b2, conv_b2, conv_b2,
      f(dt_bias).reshape(G, 1, nh), f(dt_bias).reshape(G, nh, 1), f(a_log).reshape(G, 1, nh),
      f(a_log).reshape(G, nh, 1), jnp.repeat(f(d_skip), SSM_HEAD_DIM).reshape(G, 1, gw), f(gain).reshape(G, 1, gw))


def _layer_ab(x2, bsz, t, norm_g, w_in, ml_b_i, ml_b_f, ml_norm, ssm_conv_w, ssm_conv_b, ssm_dt_bias,
              ssm_a_log, ssm_d, ssm_norm, w_out):
    m = bsz * t
    small = jnp.concatenate([w_in[:, 8192:8200], w_in[:, 13320:13352]], axis=1)
    w_main = jnp.concatenate(
        [w_in[:, :8192], w_in[:, 8200:13320], jnp.pad(small, ((0, 0), (0, 512 - small.shape[1])))], axis=1
    ).astype(BF16)
    proj = _norm_matmul(x2, 0, D_MODEL, norm_g, w_main, out_dtype=F32, tn=1536, name="proj_ab")
    gate_cols = proj[:, 13312:13352]
    ml_gates = gate_cols[:, :8].reshape(bsz, t // min(ML_CHUNK, t), min(ML_CHUNK, t), 2, ML_HEADS)
    ml_gates = jnp.transpose(ml_gates, (0, 4, 1, 3, 2))
    dt_raw = gate_cols[:, 8:40].reshape(bsz, t, SSM_GROUPS, SSM_HEADS_PER_GROUP)
    dt_col = jnp.transpose(dt_raw, (0, 2, 1, 3))
    dt_row = jnp.transpose(dt_raw, (0, 2, 3, 1))
    y_a = _mlstm(proj, ml_gates, ml_b_i, ml_b_f, ml_norm, bsz, t)
    y_b = _ssd(proj, dt_col, dt_row, ssm_conv_w, ssm_conv_b, ssm_dt_bias, ssm_a_log, ssm_d, ssm_norm, bsz, t)
    return _out_proj([y_a, y_b], w_out.astype(BF16), x2, jnp.ones((D_MODEL,), F32), final_norm=False,
                     name="out_ab")


def _indexer_body(iq_ref, iw_ref, ik_ref, mask_ref, key_scr, hi_scr, lo_scr, pos_scr, *, top_k):
    n_kt = key_scr.shape[0]
    KT, Q = IDX_KT, IDX_Q
    i = pl.program_id(1)
    start = i * Q
    n_tiles = (start + Q + KT - 1) // KT
    iqb = iq_ref[...].astype(BF16)
    iw = iw_ref[...] * (IDX_HEADS ** -0.5 * IDX_DIM ** -0.5)
    qpos = start + lax.broadcasted_iota(I32, (1, Q), 1)
    limit = (lax.shift_right_logical(qpos, int(math.log2(DSA_CHUNK))) + 1) * DSA_CHUNK
    kiota = lax.broadcasted_iota(I32, (KT, 1), 0)

    def score_tile(tt, carry):
        acc = jnp.zeros((KT, Q), F32)
        for p in range(IDX_HEADS // 2):
            xq = iqb[:, 2 * IDX_DIM * p:2 * IDX_DIM * (p + 1)]
            s0 = _bdot_nt(ik_ref[tt, 0], xq)
            s1 = _bdot_nt(ik_ref[tt, 1], xq)
            acc = acc + iw[2 * p:2 * p + 1, :] * jnp.maximum(s0, 0.0)
            acc = acc + iw[2 * p + 1:2 * p + 2, :] * jnp.maximum(s1, 0.0)
        bits = lax.bitcast_convert_type(acc + 0.0, I32)
        key = jnp.where(bits < 0, bits ^ jnp.int32(0x7FFFFFFF), bits)
        key = jnp.where(tt * KT + kiota < limit, key, jnp.int32(INT_MIN))
        key_scr[tt] = key
        hi_scr[tt] = lax.shift_right_arithmetic(key, 16).astype(I16)
        lo_scr[tt] = ((key & jnp.int32(0xFFFF)) - jnp.int32(HALF16)).astype(I16)
        return carry

    lax.fori_loop(0, n_tiles, score_tile, 0)

    def count(pred):
        def body(tt, acc):
            w = jnp.where(pred(key_scr[tt], tt * KT + kiota), 1.0, 0.0)
            for r in range(KT // 8):
                acc = acc + w[8 * r:8 * (r + 1), :]
            return acc
        acc = lax.fori_loop(0, n_tiles, body, jnp.zeros((8, Q), F32))
        return jnp.sum(acc, axis=0, keepdims=True)

    def count16(ref, pred):
        def body(tt, accs):
            accs = list(accs)
            w = jnp.where(pred(ref[tt]), jnp.int16(1), jnp.int16(0))
            for r in range(KT // 16):
                accs[r % 4] = accs[r % 4] + w[16 * r:16 * (r + 1), :]
            return tuple(accs)
        zero = jnp.zeros((16, Q), I16)
        a0, a1, a2, a3 = lax.fori_loop(0, n_tiles, body, (zero, zero, zero, zero))
        return jnp.sum(((a0 + a1) + (a2 + a3)).astype(F32), axis=0, keepdims=True)

    def rows16(v):
        return jnp.concatenate([jnp.broadcast_to(v, (16, Q)).astype(I16)] * (KT // 16), axis=0)

    def bisect16(ref, n_above):
        t16 = jnp.full((1, Q), -HALF16, I32)
        for bit in range(15, -1, -1):
            cand = jnp.zeros_like(t16) if bit == 15 else t16 + jnp.int32(1 << bit)
            ct = rows16(cand)
            t16 = jnp.where(n_above + count16(ref, lambda v, ct=ct: v >= ct) >= top_k, cand, t16)
        return t16

    thr_hi = bisect16(hi_scr, 0.0)
    th = rows16(thr_hi)
    n_hi_gt = count16(hi_scr, lambda v: v > th)

    def keep_lo(tt, carry):
        lo_scr[tt] = jnp.where(hi_scr[tt] == th, lo_scr[tt], jnp.int16(-HALF16))
        return carry

    lax.fori_loop(0, n_tiles, keep_lo, 0)
    thr = thr_hi * jnp.int32(2 * HALF16) + (bisect16(lo_scr, n_hi_gt) + jnp.int32(HALF16))

    n_gt = count(lambda k, pos: k > thr)
    n_ge = count(lambda k, pos: k >= thr)
    room = top_k - n_gt
    pos_scr[...] = jnp.full((1, Q), 2 * n_kt * KT, I32)

    @pl.when(jnp.max(jnp.where(thr > INT_MIN, n_ge, 0.0)) > top_k)
    def _():
        pcut = jnp.zeros((1, Q), I32)
        for bit in range(int(math.log2(n_kt * KT)), -1, -1):
            cand = pcut + jnp.int32(1 << bit)
            n_tie = count(lambda k, pos, cand=cand: (k == thr) & (pos < cand))
            pcut = jnp.where(n_tie <= room, cand, pcut)
        pos_scr[...] = pcut

    pcut = pos_scr[...]

    def write_tile(tt, carry):
        k = key_scr[tt]
        pos = tt * KT + kiota
        sel = (k > thr) | ((k == thr) & (pos < pcut))
        sel = sel & (k > INT_MIN)
        mask_ref[tt] = jnp.transpose(jnp.where(sel, 1.0, 0.0)).astype(jnp.int8)
        return carry

    lax.fori_loop(0, n_tiles, write_tile, 0)

    def zero_tile(tt, carry):
        mask_ref[tt] = jnp.zeros((Q, KT), jnp.int8)
        return carry

    lax.fori_loop(n_tiles, n_kt, zero_tile, 0)


def _indexer(proj, iq_blk, iw_t, ik_pad, bsz, t):
    nq = t // IDX_Q
    n_kt = t // IDX_KT
    top_k = min(TOPK_MAX, t // 4)
    return pl.pallas_call(
        functools.partial(_indexer_body, top_k=top_k),
        grid=(bsz, nq),
        in_specs=[
            pl.BlockSpec((IDX_Q, IDX_HEADS * IDX_DIM), lambda b, i: (b * nq + i, iq_blk)),
            pl.BlockSpec((None, IDX_HEADS, IDX_Q), lambda b, i: (b * nq + i, 0, 0)),
            pl.BlockSpec((None, n_kt, 2, IDX_KT, 2 * IDX_DIM), lambda b, i: (b, 0, 0, 0, 0)),
        ],
        out_specs=pl.BlockSpec((None, n_kt, IDX_Q, IDX_KT), lambda b, i: (b * nq + i, 0, 0, 0)),
        out_shape=jax.ShapeDtypeStruct((bsz * nq, n_kt, IDX_Q, IDX_KT), jnp.int8),
        scratch_shapes=[pltpu.VMEM((n_kt, IDX_KT, IDX_Q), I32), pltpu.VMEM((n_kt, IDX_KT, IDX_Q), I16),
                        pltpu.VMEM((n_kt, IDX_KT, IDX_Q), I16), pltpu.VMEM((1, IDX_Q), I32)],
        compiler_params=_cparams(("parallel", "arbitrary"), 48),
        name="dsa_indexer",
    )(proj, iw_t, ik_pad)


NEG_BIG = -3.0e38
LOG2_E = math.log2(math.e)


def _attn_body(q_ref, k_ref, v_ref, mask_ref, z_ref, tb_ref, o_ref, acc_scr, m_scr):
    T = ATT_T
    D = ATT_HEAD_DIM
    nsub = T // ATT_SUB
    i = pl.program_id(1)
    j = pl.program_id(2)

    @pl.when(j == 0)
    def _():
        acc_scr[...] = jnp.zeros_like(acc_scr)
        m_scr[...] = jnp.full_like(m_scr, NEG_BIG)

    def tile_step(near):
        sel = jnp.concatenate([mask_ref[a, 0].astype(F32) for a in range(mask_ref.shape[0])], axis=0)
        mask_bias = (sel - 1.0) * (-NEG_BIG)
        ones = jnp.ones((T, D), BF16)

        def pair_body(p, carry):
            q2, k2, v2 = q_ref[p], k_ref[p], v_ref[p]
            m_old = [m_scr[2 * p + e] for e in range(2)]
            a_old = [acc_scr[2 * p + e] for e in range(2)]
            new = []
            logits = [_bdot_nt(q2[:, D * e:D * (e + 1)], k2[:, D * e:D * (e + 1)]) for e in range(2)]
            for e in range(2):
                s = logits[e] + mask_bias
                if near is not None:
                    tb = tb_ref[2 * p + e]
                    zero = jnp.zeros((ATT_SUB, ATT_SUB), F32)
                    rows = []
                    for a in range(nsub):
                        cols = []
                        for bb in range(nsub):
                            rel_blocks = (bb - a) if near == "diag" else (bb - a - nsub)
                            cols.append(tb[0] if rel_blocks == 0 else tb[1] if rel_blocks == -1 else zero)
                        rows.append(jnp.concatenate(cols, axis=1))
                    s = s + jnp.concatenate(rows, axis=0)
                m_new = jnp.maximum(m_old[e], jnp.max(s, axis=1, keepdims=True))
                alpha = jnp.exp2(m_old[e] - m_new)
                pr = jnp.exp2(s - jnp.concatenate([m_new] * (T // LANES), axis=1))
                v_aug = jnp.concatenate([v2[:, D * e:D * (e + 1)], ones], axis=1)
                a_new = jnp.concatenate([alpha] * (2 * D // LANES), axis=1) * a_old[e] + _bdot(pr, v_aug)
                new.append((m_new, a_new))
            for e in range(2):
                m_scr[2 * p + e] = new[e][0]
                acc_scr[2 * p + e] = new[e][1]
            return carry

        lax.fori_loop(0, ATT_HEADS // 2, pair_body, 0, unroll=8)

    @pl.when(j < i - 1)
    def _():
        tile_step(None)

    @pl.when(j == i - 1)
    def _():
        tile_step("prev")

    @pl.when(j == i)
    def _():
        tile_step("diag")
        for p in range(ATT_HEADS // 2):
            for e in range(2):
                hh = 2 * p + e
                a = acc_scr[hh]
                o = a[:, :D] / a[:, D:]
                o = o * _silu(z_ref[p, :, D * e:D * (e + 1)].astype(F32))
                o_ref[:, D * hh:D * (hh + 1)] = o.astype(o_ref.dtype)


def _attention(qz, kv, mask, tb, bsz, t):
    T = ATT_T
    nq = t // T
    npair = ATT_HEADS // 2
    pw = 2 * ATT_HEAD_DIM
    nrow = T // IDX_Q
    ncol = T // IDX_KT
    return pl.pallas_call(
        _attn_body,
        grid=(bsz, nq, nq),
        in_specs=[
            pl.BlockSpec((npair, T, pw), lambda b, i, j: (0, b * nq + i, 0)),
            pl.BlockSpec((npair, T, pw), lambda b, i, j: (0, b * nq + jnp.minimum(j, i), 0)),
            pl.BlockSpec((npair, T, pw), lambda b, i, j: (1, b * nq + jnp.minimum(j, i), 0)),
            pl.BlockSpec((nrow, ncol, IDX_Q, IDX_KT), lambda b, i, j: (b * nq + i, jnp.minimum(j, i), 0, 0)),
            pl.BlockSpec((npair, T, pw), lambda b, i, j: (1, b * nq + i, 0)),
            pl.BlockSpec((ATT_HEADS, 2, ATT_SUB, ATT_SUB), lambda b, i, j: (0, 0, 0, 0)),
        ],
        out_specs=pl.BlockSpec((T, ATT_HEADS * ATT_HEAD_DIM), lambda b, i, j: (b * nq + i, 0)),
        out_shape=jax.ShapeDtypeStruct((bsz * t, ATT_HEADS * ATT_HEAD_DIM), BF16),
        scratch_shapes=[pltpu.VMEM((ATT_HEADS, T, 2 * ATT_HEAD_DIM), F32), pltpu.VMEM((ATT_HEADS, T, LANES), F32)],
        compiler_params=_cparams(("parallel", "parallel", "arbitrary"), 56),
        name="dsa_attention",
    )(qz, kv, kv, mask, qz, tb)


def _t5_bucket(rel):
    nb = REL_BUCKETS // 2
    max_exact = nb // 2
    ret = jnp.where(rel > 0, nb, 0)
    n = jnp.abs(rel)
    nf = jnp.maximum(n, max_exact).astype(F32)
    large = max_exact + (jnp.log(nf / max_exact) / math.log(REL_MAX_DIST / max_exact)
                         * (nb - max_exact)).astype(jnp.int32)
    large = jnp.minimum(large, nb - 1)
    return ret + jnp.where(n < max_exact, n, large)


def _rel_bias_blocks(rel_bias):
    r = jnp.arange(ATT_SUB)[:, None]
    c = jnp.arange(ATT_SUB)[None, :]
    rel = jnp.stack([c - r, c - r - ATT_SUB])
    assert ATT_SUB >= REL_MAX_DIST and ATT_SUB % DSA_CHUNK == 0
    rb = rel_bias.astype(F32)
    far = rb[_t5_bucket(jnp.int32(-2 * ATT_SUB))]
    onehot = (_t5_bucket(rel)[..., None] == jnp.arange(REL_BUCKETS)).astype(F32)
    table = jnp.einsum("vrcb,bh->hvrc", onehot, rb - far, precision=HI)
    return table * LOG2_E


GDN_PREP_ROWS = 256
GDN_SOLVE_LANES = 256


def _conv_with_prev(x, prev, w):
    xx = jnp.concatenate([prev, x], axis=0)
    y = w[CONV_WIDTH - 1:CONV_WIDTH, :] * x
    for j in range(CONV_WIDTH - 1):
        y = y + w[j:j + 1, :] * pltpu.roll(xx, CONV_WIDTH - 1 - j, axis=0)[8:, :]
    return y


def _gdn_prep_body(v_ref, q_ref, k_ref, pv_ref, pq_ref, pk_ref, bc_ref, ac_ref, ar_ref, wq_ref, wk_ref, wv_ref,
                   dtbc_ref, dtbr_ref, alc_ref, alr_ref,
                   qn_ref, rhs_ref, lu_ref, a_ref, egc_ref, egl_ref):
    L = GDN_CHUNK
    D = GDN_HEAD_DIM
    nv = GDN_V_HEADS // GDN_GROUPS
    rep = GDN_V_HEADS // GDN_QK_HEADS
    seq_start = pl.program_id(2) == 0

    def conv(x_ref, p_ref, w_ref):
        prev = jnp.where(seq_start, 0.0, p_ref[...])
        return _silu(_conv_with_prev(x_ref[...], prev, w_ref[...]))

    qc = conv(q_ref, pq_ref, wq_ref)
    kc = conv(k_ref, pk_ref, wk_ref)
    vc = conv(v_ref, pv_ref, wv_ref)
    qn, kn = [], []
    for kh in range(nv // rep):
        qh = qc[:, D * kh:D * (kh + 1)]
        kk_ = kc[:, D * kh:D * (kh + 1)]
        qh = qh * lax.rsqrt(jnp.sum(qh * qh, axis=-1, keepdims=True) + EPS) * (D ** -0.5)
        kn.append(kk_ * lax.rsqrt(jnp.sum(kk_ * kk_, axis=-1, keepdims=True) + EPS))
        qn.append(qh.astype(BF16))
        qn_ref[:, D * kh:D * (kh + 1)] = qn[kh]

    beta = _sigmoid(bc_ref[...])
    g_col = -jnp.exp(alc_ref[...]) * _softplus(ac_ref[...] + dtbc_ref[...])
    ri = lax.broadcasted_iota(I32, (L, L), 0)
    ci = lax.broadcasted_iota(I32, (L, L), 1)
    causal = ci <= ri
    strict = ci < ri
    for cc in range(a_ref.shape[0]):
        rows = slice(L * cc, L * (cc + 1))
        gc_col = _dot_sel_lhs(causal, g_col[rows, :])
        g_row = -jnp.exp(alr_ref[...]) * _softplus(ar_ref[cc] + dtbr_ref[...])
        gc_row = _dot_sel_rhs(g_row, ri <= ci)
        eg_col = jnp.exp(gc_col)
        egc_ref[rows, :] = eg_col
        for kh in range(nv // rep):
            kf = kn[kh][rows, :]
            kb = kf.astype(BF16)
            kk = _bdot_nt(kb, kb)
            qk = _bdot_nt(qn[kh][rows, :], kb)
            for e in range(rep):
                vh = rep * kh + e
                gcol = gc_col[:, vh:vh + 1]
                bcol = beta[rows, vh:vh + 1]
                dec = jnp.exp(jnp.where(causal, gcol - gc_row[vh:vh + 1, :], -jnp.inf))
                a_ref[cc, vh] = jnp.where(strict, bcol * kk * dec, 0.0).astype(a_ref.dtype)
                g_last = gcol[L - 1:L, :]
                kw_t = jnp.transpose(kf * jnp.exp(g_last - gcol))
                lu_ref[cc, vh] = jnp.concatenate([qk * dec, kw_t], axis=0).astype(lu_ref.dtype)
                egl_ref[cc, vh:vh + 1, :] = jnp.broadcast_to(jnp.exp(g_last), (1, LANES))
                rhs_ref[rows, 2 * D * vh:2 * D * (vh + 1)] = jnp.concatenate(
                    [vc[rows, D * vh:D * (vh + 1)] * bcol, kf * (bcol * eg_col[:, vh:vh + 1])], axis=1
                ).astype(rhs_ref.dtype)


def _gdn_solve_body(a_in_ref, t_ref, t_scr, a_ref):
    L, _, PB = a_in_ref.shape
    UN = 8
    sub = lax.broadcasted_iota(I32, (UN, PB), 0)
    t_scr[...] = jnp.zeros_like(t_scr)
    a_ref[...] = a_in_ref[...].astype(F32)

    for ib in range(L // UN):
        def row_body(r, carry, ib=ib):
            i = ib * UN + r
            acc = [jnp.zeros((UN, PB), F32) for _ in range(ib + 1)]
            for jb in range(ib + 1):
                a_blk = a_ref[i, UN * jb:UN * (jb + 1), :]
                for jj in range(UN):
                    a_ij = a_blk[jj:jj + 1, :]
                    for cg in range(jb + 1):
                        acc[cg] = acc[cg] + a_ij * t_scr[UN * jb + jj, UN * cg:UN * (cg + 1), :]
            for cg in range(ib + 1):
                unit = jnp.where(sub == r, 1.0, 0.0) if cg == ib else 0.0
                t_scr[i, UN * cg:UN * (cg + 1), :] = unit - acc[cg]
            return carry

        lax.fori_loop(0, UN, row_body, 0)
    t_ref[...] = t_scr[...].astype(t_ref.dtype)


def _gdn_scan_body(qn_ref, rhs_ref, lu_ref, t_ref, z_ref, egc_ref, egl_ref, gain_ref, y_ref, s_scr):
    L = qn_ref.shape[0]
    D = GDN_HEAD_DIM
    nv = GDN_V_HEADS // GDN_GROUPS
    rep = GDN_V_HEADS // GDN_QK_HEADS

    @pl.when(pl.program_id(2) == 0)
    def _():
        s_scr[...] = jnp.zeros_like(s_scr)

    eg_col = egc_ref[...]
    z = z_ref[...]
    heads = range(nv)
    sol = [_bdot(t_ref[vh], rhs_ref[:, 2 * D * vh:2 * D * (vh + 1)]) for vh in heads]
    s_st = [s_scr[vh] for vh in heads]
    r_s = [_bdot(jnp.concatenate([sol[vh][:, D:].astype(BF16),
                                  qn_ref[:, D * (vh // rep):D * (vh // rep + 1)]], axis=0), s_st[vh])
           for vh in heads]
    u = [sol[vh][:, :D] - r_s[vh][:L] for vh in heads]
    r_u = [_bdot(lu_ref[vh], u[vh]) for vh in heads]
    for vh in heads:
        s_scr[vh] = egl_ref[vh:vh + 1, :] * s_st[vh] + r_u[vh][L:]
        o = r_s[vh][L:] * eg_col[:, vh:vh + 1] + r_u[vh][:L]
        o = o * lax.rsqrt(jnp.mean(o * o, axis=-1, keepdims=True) + EPS) * gain_ref[...]
        y_ref[:, D * vh:D * (vh + 1)] = (o * _silu(z[:, D * vh:D * (vh + 1)])).astype(y_ref.dtype)


def _gdn(proj, b_col, a_col, a_row, conv_w, dt_bias, a_log, gain, bsz, t):
    L = GDN_CHUNK
    nc = t // L
    G = GDN_GROUPS
    nv = GDN_V_HEADS // G
    qw = GDN_QK_HEADS // G * GDN_HEAD_DIM
    vw = nv * GDN_HEAD_DIM
    m = bsz * t
    z_blk, q_blk, k_blk = 4096 // vw, 8192 // qw, 10240 // qw
    f = lambda a: a.astype(F32)
    cw = f(conv_w)
    pc_spec = pl.BlockSpec((None, 1, nv), lambda b, g, c: (g, 0, 0))
    pr_spec = pl.BlockSpec((None, nv, 1), lambda b, g, c: (g, 0, 0))
    wq_spec = pl.BlockSpec((CONV_WIDTH, qw), lambda b, g, c: (0, g))
    wk_spec = pl.BlockSpec((CONV_WIDTH, qw), lambda b, g, c: (0, G + g))
    wv_spec = pl.BlockSpec((CONV_WIDTH, vw), lambda b, g, c: (0, (2 * G * qw) // vw + g))

    LB = min(GDN_PREP_ROWS, t)
    ncb = LB // L
    nb = t // LB
    rowb = lambda b, g, c: b * nb + c
    prev8 = lambda b, g, c: jnp.maximum((b * t + c * LB) // 8 - 1, 0)
    D = GDN_HEAD_DIM
    qn, rhs, lu, a_mat, egc, egl = pl.pallas_call(
        _gdn_prep_body,
        grid=(bsz, G, nb),
        in_specs=[
            pl.BlockSpec((LB, vw), lambda b, g, c: (rowb(b, g, c), g)),
            pl.BlockSpec((LB, qw), lambda b, g, c: (rowb(b, g, c), q_blk + g)),
            pl.BlockSpec((LB, qw), lambda b, g, c: (rowb(b, g, c), k_blk + g)),
            pl.BlockSpec((8, vw), lambda b, g, c: (prev8(b, g, c), g)),
            pl.BlockSpec((8, qw), lambda b, g, c: (prev8(b, g, c), q_blk + g)),
            pl.BlockSpec((8, qw), lambda b, g, c: (prev8(b, g, c), k_blk + g)),
            pl.BlockSpec((None, None, LB, nv), lambda b, g, c: (b, g, c, 0)),
            pl.BlockSpec((None, None, LB, nv), lambda b, g, c: (b, g, c, 0)),
            pl.BlockSpec((None, None, ncb, nv, L), lambda b, g, c: (b, g, c, 0, 0)),
            wq_spec, wk_spec, wv_spec, pc_spec, pr_spec, pc_spec, pr_spec,
        ],
        out_specs=[
            pl.BlockSpec((LB, qw), lambda b, g, c: (rowb(b, g, c), g)),
            pl.BlockSpec((LB, 2 * vw), lambda b, g, c: (rowb(b, g, c), g)),
            pl.BlockSpec((None, None, ncb, nv, L + D, L), lambda b, g, c: (b, g, c, 0, 0, 0)),
            pl.BlockSpec((None, None, ncb, nv, L, L), lambda b, g, c: (b, g, c, 0, 0, 0)),
            pl.BlockSpec((None, None, LB, nv), lambda b, g, c: (b, g, c, 0)),
            pl.BlockSpec((None, None, ncb, nv, LANES), lambda b, g, c: (b, g, c, 0, 0)),
        ],
        out_shape=[
            jax.ShapeDtypeStruct((m, G * qw), BF16),
            jax.ShapeDtypeStruct((m, 2 * G * vw), BF16),
            jax.ShapeDtypeStruct((bsz, G, nc, nv, L + D, L), BF16),
            jax.ShapeDtypeStruct((bsz, G, nc, nv, L, L), BF16),
            jax.ShapeDtypeStruct((bsz, G, t, nv), F32),
            jax.ShapeDtypeStruct((bsz, G, nc, nv, LANES), F32),
        ],
        compiler_params=_cparams(("parallel", "parallel", "parallel"), 48),
        name="gdn_prep",
    )(proj, proj, proj, proj, proj, proj, b_col, a_col, a_row, cw, cw, cw,
      f(dt_bias).reshape(G, 1, nv), f(dt_bias).reshape(G, nv, 1), f(a_log).reshape(G, 1, nv),
      f(a_log).reshape(G, nv, 1))

    n_prob = bsz * G * nc * nv
    pb = min(GDN_SOLVE_LANES, n_prob)
    a_t = jnp.transpose(a_mat.reshape(n_prob, L, L), (1, 2, 0))
    t_t = pl.pallas_call(
        _gdn_solve_body,
        grid=(n_prob // pb,),
        in_specs=[pl.BlockSpec((L, L, pb), lambda p: (0, 0, p))],
        out_specs=pl.BlockSpec((L, L, pb), lambda p: (0, 0, p)),
        out_shape=jax.ShapeDtypeStruct((L, L, n_prob), BF16),
        scratch_shapes=[pltpu.VMEM((L, L, pb), F32), pltpu.VMEM((L, L, pb), F32)],
        compiler_params=_cparams(("parallel",), 40),
        name="gdn_solve",
    )(a_t)
    t_mat = jnp.transpose(t_t, (2, 0, 1)).reshape(bsz, G, nc, nv, L, L)

    row = lambda b, g, c: b * nc + c
    return pl.pallas_call(
        _gdn_scan_body,
        grid=(bsz, G, nc),
        in_specs=[
            pl.BlockSpec((L, qw), lambda b, g, c: (row(b, g, c), g)),
            pl.BlockSpec((L, 2 * vw), lambda b, g, c: (row(b, g, c), g)),
            pl.BlockSpec((None, None, None, nv, L + D, L), lambda b, g, c: (b, g, c, 0, 0, 0)),
            pl.BlockSpec((None, None, None, nv, L, L), lambda b, g, c: (b, g, c, 0, 0, 0)),
            pl.BlockSpec((L, vw), lambda b, g, c: (row(b, g, c), z_blk + g)),
            pl.BlockSpec((None, None, L, nv), lambda b, g, c: (b, g, c, 0)),
            pl.BlockSpec((None, None, None, nv, LANES), lambda b, g, c: (b, g, c, 0, 0)),
            pl.BlockSpec((1, GDN_HEAD_DIM), lambda b, g, c: (0, 0)),
        ],
        out_specs=pl.BlockSpec((L, vw), lambda b, g, c: (row(b, g, c), g)),
        out_shape=jax.ShapeDtypeStruct((m, GDN_V_HEADS * GDN_HEAD_DIM), BF16),
        scratch_shapes=[pltpu.VMEM((nv, GDN_HEAD_DIM, GDN_HEAD_DIM), F32)],
        compiler_params=_cparams(("parallel", "parallel", "arbitrary"), 40),
        name="gdn_scan",
    )(qn, rhs, lu, t_mat, proj, egc, egl, f(gain).reshape(1, GDN_HEAD_DIM))


def _layer_cd(x2, bsz, t, norm_g, w_in, kv_norm, w_uk, w_uv, gdn_conv_w, gdn_dt_bias, gdn_a_log, gdn_norm,
              w_out, rel_bias, final_g, final_norm):
    m = bsz * t
    small = jnp.concatenate([w_in[:, 5632:5712], w_in[:, 18000:18064]], axis=1)
    w_main = jnp.concatenate(
        [w_in[:, 9808:13904], w_in[:, 13904:18000], w_in[:, 5712:9808], w_in[:, 4608:5632], w_in[:, 2048:2560],
         jnp.pad(small, ((0, 0), (0, 512 - small.shape[1])))], axis=1).astype(BF16)
    proj = _norm_matmul(x2, 0, D_MODEL, norm_g, w_main, out_dtype=F32, tn=1024, name="proj_cd")
    w_attn = jnp.concatenate([w_in[:, :2048] * (ATT_HEAD_DIM ** -0.5 * LOG2_E), w_in[:, 2560:4608]],
                             axis=1).astype(BF16)
    qz = _norm_matmul(x2, 0, D_MODEL, norm_g, w_attn, out_dtype=BF16, blocked_width=2 * ATT_HEAD_DIM,
                      tn=1024, name="proj_cd_attn")
    w_kv = jnp.concatenate([w_uk.reshape(KV_RANK, -1), w_uv.reshape(KV_RANK, -1)], axis=1).astype(BF16)
    kv = _norm_matmul(proj, 13312 // KV_RANK, KV_RANK, kv_norm, w_kv, out_dtype=BF16,
                      blocked_width=2 * ATT_HEAD_DIM, tn=1024, name="dsa_kv")

    gate_cols = proj[:, 13824:13824 + 144]
    n_kt = t // IDX_KT
    ik = gate_cols[:, :IDX_DIM].reshape(bsz, n_kt, 1, IDX_KT, IDX_DIM).astype(BF16)
    zk = jnp.zeros_like(ik)
    ik_pad = jnp.concatenate([jnp.concatenate([ik, zk], axis=-1), jnp.concatenate([zk, ik], axis=-1)], axis=2)
    iw_t = jnp.transpose(gate_cols[:, IDX_DIM:IDX_DIM + IDX_HEADS].reshape(m // IDX_Q, IDX_Q, IDX_HEADS), (0, 2, 1))
    mask = _indexer(proj, 12288 // (IDX_HEADS * IDX_DIM), iw_t, ik_pad, bsz, t)
    y_c = _attention(qz, kv, mask, _rel_bias_blocks(rel_bias), bsz, t)

    nv = GDN_V_HEADS // GDN_GROUPS
    b_pre = gate_cols[:, 80:112].reshape(bsz, t, GDN_GROUPS, nv)
    a_pre = gate_cols[:, 112:144].reshape(bsz, t, GDN_GROUPS, nv)
    col = lambda a: jnp.transpose(a, (0, 2, 1, 3))
    rowf = lambda a: jnp.transpose(a.reshape(bsz, t // GDN_CHUNK, GDN_CHUNK, GDN_GROUPS, nv), (0, 3, 1, 4, 2))
    y_d = _gdn(proj, col(b_pre), col(a_pre), rowf(a_pre), gdn_conv_w, gdn_dt_bias, gdn_a_log, gdn_norm, bsz, t)
    return _out_proj([y_c, y_d], w_out.astype(BF16), x2, final_g, final_norm=final_norm, name="out_cd")


def kernel(x, rel_bias, ab_norm, ab_w_in, ab_ml_b_i, ab_ml_b_f, ab_ml_norm, ab_ssm_conv_w, ab_ssm_conv_b, ab_ssm_dt_bias, ab_ssm_a_log, ab_ssm_d, ab_ssm_norm, ab_w_out, cd_norm, cd_w_in, cd_kv_norm, cd_w_uk, cd_w_uv, cd_gdn_conv_w, cd_gdn_dt_bias, cd_gdn_a_log, cd_gdn_norm, cd_w_out, final_norm):
    bsz, t, d = x.shape
    x2 = x.reshape(bsz * t, d)
    x2 = _layer_ab(x2, bsz, t, ab_norm[0], ab_w_in[0], ab_ml_b_i[0], ab_ml_b_f[0], ab_ml_norm[0],
                   ab_ssm_conv_w[0], ab_ssm_conv_b[0], ab_ssm_dt_bias[0], ab_ssm_a_log[0], ab_ssm_d[0],
                   ab_ssm_norm[0], ab_w_out[0])
    x2 = _layer_cd(x2, bsz, t, cd_norm[0], cd_w_in[0], cd_kv_norm[0], cd_w_uk[0], cd_w_uv[0], cd_gdn_conv_w[0],
                   cd_gdn_dt_bias[0], cd_gdn_a_log[0], cd_gdn_norm[0], cd_w_out[0], rel_bias, final_norm, True)
    return x2.reshape(bsz, t, d)
```

```python
import functools
import math

import jax
import jax.numpy as jnp
from jax import lax
from jax.experimental import pallas as pl
from jax.experimental.pallas import tpu as pltpu

F32 = jnp.float32
BF16 = jnp.bfloat16
I32 = jnp.int32
I16 = jnp.int16
HI = lax.Precision.HIGHEST

EPS = 1e-6
CONV_WIDTH = 4
D_MODEL = 2048

ML_HEADS = 4
ML_DV = 512
ML_DK = 256

SSM_GROUPS = 4
SSM_HEADS_PER_GROUP = 8
SSM_HEAD_DIM = 64
SSM_STATE = 128
SSM_GROUP_WIDTH = SSM_HEADS_PER_GROUP * SSM_HEAD_DIM

ATT_HEADS = 16
ATT_HEAD_DIM = 128
KV_RANK = 512
IDX_HEADS = 16
IDX_DIM = 64
TOPK_MAX = 256
DSA_CHUNK = 64
REL_BUCKETS = 32
REL_MAX_DIST = 128

GDN_QK_HEADS = 16
GDN_V_HEADS = 32
GDN_HEAD_DIM = 128
GDN_GROUPS = 2
GDN_CHUNK = 64

VMEM_BYTES_V7X = 64 * 1024 * 1024
LANES = 128

ML_CHUNK = 1024
SSD_CHUNK = 256
IDX_Q = 256
IDX_KT = 512
ATT_T = 512
ATT_SUB = 128
INT_MIN = -(2 ** 31)
HALF16 = 2 ** 15


def _cparams(sem, vmem_mib):
    return pltpu.CompilerParams(dimension_semantics=sem, vmem_limit_bytes=vmem_mib * 1024 * 1024)


def _bdot(a, b):
    return jnp.dot(a.astype(BF16), b.astype(BF16), preferred_element_type=F32)


def _bdot_nt(a, b):
    return lax.dot_general(a.astype(BF16), b.astype(BF16), (((1,), (1,)), ((), ())), preferred_element_type=F32)


def _bdot_tn(a, b):
    return lax.dot_general(a.astype(BF16), b.astype(BF16), (((0,), (0,)), ((), ())), preferred_element_type=F32)


def _split3(x):
    hi = x.astype(BF16)
    r = x - hi.astype(F32)
    mid = r.astype(BF16)
    return hi, mid, (r - mid.astype(F32)).astype(BF16)


def _as_bf16_01(sel):
    return jnp.where(sel, 1.0, 0.0).astype(BF16) if sel.dtype == jnp.bool_ else sel.astype(BF16)


def _dot_sel_lhs(sel, x):
    s = _as_bf16_01(sel)
    return sum(jnp.dot(s, p, preferred_element_type=F32) for p in _split3(x))


def _dot_sel_rhs(x, sel):
    s = _as_bf16_01(sel)
    return sum(jnp.dot(p, s, preferred_element_type=F32) for p in _split3(x))


def _softplus(x):
    return jnp.maximum(x, 0.0) + jnp.log1p(jnp.exp(-jnp.abs(x)))


def _sigmoid(x):
    return 0.5 * jnp.tanh(0.5 * x) + 0.5


def _silu(x):
    return x * _sigmoid(x)


def _norm_matmul_body(x_ref, g_ref, w_ref, o_ref, h_ref, *, blocked):
    @pl.when(pl.program_id(1) == 0)
    def _():
        x = x_ref[...].astype(F32)
        ms = jnp.mean(x * x, axis=-1, keepdims=True)
        h_ref[...] = (x * lax.rsqrt(ms + EPS) * g_ref[...]).astype(h_ref.dtype)

    acc = jnp.dot(h_ref[...], w_ref[...], preferred_element_type=F32)
    if blocked:
        bw = o_ref.shape[-1]
        for t in range(o_ref.shape[0]):
            o_ref[t] = acc[:, t * bw:(t + 1) * bw].astype(o_ref.dtype)
    else:
        o_ref[...] = acc.astype(o_ref.dtype)


def _norm_matmul(x, x_col_block, k_dim, gain, w, *, out_dtype, blocked_width=None, tm=1024, tn=512, name):
    m = x.shape[0]
    n = w.shape[1]
    tm = min(tm, m)
    assert m % tm == 0 and n % tn == 0 and w.shape[0] == k_dim
    grid = (m // tm, n // tn)
    if blocked_width is None:
        out_shape = jax.ShapeDtypeStruct((m, n), out_dtype)
        out_spec = pl.BlockSpec((tm, tn), lambda i, j: (i, j))
    else:
        nb = tn // blocked_width
        out_shape = jax.ShapeDtypeStruct((n // blocked_width, m, blocked_width), out_dtype)
        out_spec = pl.BlockSpec((nb, tm, blocked_width), lambda i, j: (j, i, 0))
    return pl.pallas_call(
        functools.partial(_norm_matmul_body, blocked=blocked_width is not None),
        grid=grid,
        in_specs=[
            pl.BlockSpec((tm, k_dim), lambda i, j: (i, x_col_block), pipeline_mode=pl.Buffered(1)),
            pl.BlockSpec((1, k_dim), lambda i, j: (0, 0)),
            pl.BlockSpec((k_dim, tn), lambda i, j: (0, j)),
        ],
        out_specs=out_spec,
        out_shape=out_shape,
        scratch_shapes=[pltpu.VMEM((tm, k_dim), BF16)],
        compiler_params=_cparams(("parallel", "arbitrary"), 48),
        name=name,
    )(x, gain.reshape(1, k_dim).astype(F32), w)


def _out_proj_body(*refs, n_y, k_tiles, final_norm):
    y_refs = refs[:n_y]
    w_ref, res_ref, g_ref, o_ref, acc_ref = refs[n_y:]
    k = pl.program_id(1)

    @pl.when(k == 0)
    def _():
        acc_ref[...] = jnp.zeros_like(acc_ref)

    lo = 0
    for y_ref, nk in zip(y_refs, k_tiles):
        @pl.when((k >= lo) & (k < lo + nk))
        def _(y_ref=y_ref):
            acc_ref[...] += jnp.dot(y_ref[...], w_ref[...], preferred_element_type=F32)
        lo += nk

    @pl.when(k == lo - 1)
    def _():
        xn = res_ref[...] + acc_ref[...]
        if final_norm:
            ms = jnp.mean(xn * xn, axis=-1, keepdims=True)
            xn = xn * lax.rsqrt(ms + EPS) * g_ref[...]
        o_ref[...] = xn


def _out_proj(ys, w, res, gain, *, final_norm, tm=512, tk=1024, name):
    m, d = res.shape
    tm = min(tm, m)
    k_tiles = [y.shape[1] // tk for y in ys]
    assert sum(k_tiles) * tk == w.shape[0]
    y_specs = []
    lo = 0
    for nk in k_tiles:
        y_specs.append(pl.BlockSpec((tm, tk), lambda i, k, lo=lo, nk=nk: (i, jnp.clip(k - lo, 0, nk - 1))))
        lo += nk
    return pl.pallas_call(
        functools.partial(_out_proj_body, n_y=len(ys), k_tiles=k_tiles, final_norm=final_norm),
        grid=(m // tm, lo),
        in_specs=y_specs + [
            pl.BlockSpec((tk, d), lambda i, k: (k, 0)),
            pl.BlockSpec((tm, d), lambda i, k: (i, 0)),
            pl.BlockSpec((1, d), lambda i, k: (0, 0)),
        ],
        out_specs=pl.BlockSpec((tm, d), lambda i, k: (i, 0)),
        out_shape=jax.ShapeDtypeStruct((m, d), F32),
        scratch_shapes=[pltpu.VMEM((tm, d), F32)],
        compiler_params=_cparams(("parallel", "arbitrary"), 48),
        name=name,
    )(*ys, w, res, gain.reshape(1, d).astype(F32))


def _mlstm_body(bi_ref, bf_ref, q_ref, k_ref, v_ref, op_ref, z_ref, g_ref, gain_ref, y_ref,
                c_scr, n_scr, m_scr):
    L = q_ref.shape[0]
    h = pl.program_id(1)

    @pl.when(pl.program_id(2) == 0)
    def _():
        c_scr[...] = jnp.zeros_like(c_scr)
        n_scr[...] = jnp.zeros_like(n_scr)
        m_scr[...] = jnp.zeros_like(m_scr)

    gates = g_ref[...]
    ic = gates[0:1, :] + bi_ref[h]
    fpre = gates[1:2, :] + bf_ref[h]
    fc = -_softplus(-fpre)

    ri = lax.broadcasted_iota(I32, (L, L), 0)
    ci = lax.broadcasted_iota(I32, (L, L), 1)
    causal = ci <= ri
    eye = ci == ri
    bcum_col = jnp.sum(jnp.where(causal, fc, 0.0), axis=1, keepdims=True)
    fc_col = jnp.sum(jnp.where(eye, fc, 0.0), axis=1, keepdims=True)
    bcum_row = jnp.sum(jnp.where(ri <= ci, fc_col, 0.0), axis=0, keepdims=True)

    m_st = m_scr[...]
    d = jnp.where(causal, bcum_col - bcum_row + ic, -jnp.inf)
    inter = bcum_col + m_st
    m_row = jnp.maximum(inter, jnp.max(d, axis=1, keepdims=True))
    w_inter = jnp.exp(inter - m_row)

    q = q_ref[...]
    k = k_ref[...] * (ML_DK ** -0.5)
    v = v_ref[...]
    qb = q.astype(BF16)
    vb = v.astype(BF16)
    p = jnp.exp(d - m_row) * _bdot_nt(qb, k)
    c_st = c_scr[...]
    n_st = n_scr[...]
    num = w_inter * _bdot(qb, c_st) + _bdot(p, vb)
    den = w_inter * jnp.sum(q * n_st, axis=1, keepdims=True) + jnp.sum(p, axis=1, keepdims=True)
    hout = num / jnp.maximum(jnp.abs(den), jnp.exp(-m_row))

    b_last = jnp.sum(fc, axis=1, keepdims=True)
    g_row = b_last - bcum_row + ic
    m_new = jnp.maximum(b_last + m_st, jnp.max(g_row, axis=1, keepdims=True))
    w_old = jnp.exp(b_last + m_st - m_new)
    w_in_col = jnp.sum(jnp.where(eye, jnp.exp(g_row - m_new), 0.0), axis=1, keepdims=True)
    kw = k * w_in_col
    c_scr[...] = w_old * c_st + _bdot_tn(kw, vb)
    n_scr[...] = w_old * n_st + jnp.sum(kw, axis=0, keepdims=True)
    m_scr[...] = m_new

    hn = hout * lax.rsqrt(jnp.mean(hout * hout, axis=-1, keepdims=True) + EPS) * gain_ref[...]
    y_ref[...] = (hn * _sigmoid(op_ref[...]) * _silu(z_ref[...])).astype(y_ref.dtype)


def _mlstm(proj, gates, b_i, b_f, gain, bsz, t):
    L = min(ML_CHUNK, t)
    nc = t // L
    row = lambda b, h, c: b * nc + c
    return pl.pallas_call(
        _mlstm_body,
        grid=(bsz, ML_HEADS, nc),
        in_specs=[
            pl.BlockSpec(memory_space=pltpu.SMEM),
            pl.BlockSpec(memory_space=pltpu.SMEM),
            pl.BlockSpec((L, ML_DK), lambda b, h, c: (row(b, h, c), h)),
            pl.BlockSpec((L, ML_DK), lambda b, h, c: (row(b, h, c), ML_HEADS + h)),
            pl.BlockSpec((L, ML_DV), lambda b, h, c: (row(b, h, c), ML_HEADS + h)),
            pl.BlockSpec((L, ML_DV), lambda b, h, c: (row(b, h, c), 2 * ML_HEADS + h)),
            pl.BlockSpec((L, ML_DV), lambda b, h, c: (row(b, h, c), 3 * ML_HEADS + h)),
            pl.BlockSpec((None, None, None, 2, L), lambda b, h, c: (b, h, c, 0, 0)),
            pl.BlockSpec((None, 1, ML_DV), lambda b, h, c: (h, 0, 0)),
        ],
        out_specs=pl.BlockSpec((L, ML_DV), lambda b, h, c: (row(b, h, c), h)),
        out_shape=jax.ShapeDtypeStruct((bsz * t, ML_HEADS * ML_DV), BF16),
        scratch_shapes=[pltpu.VMEM((ML_DK, ML_DV), F32), pltpu.VMEM((1, ML_DK), F32), pltpu.VMEM((1, 1), F32)],
        compiler_params=_cparams(("parallel", "parallel", "arbitrary"), 40),
        name="mlstm",
    )(b_i.astype(F32), b_f.astype(F32), proj, proj, proj, proj, proj, gates,
      gain.reshape(ML_HEADS, 1, ML_DV).astype(F32))


def _causal_conv(x, tail_ref, w):
    L = x.shape[0]
    xx = jnp.concatenate([tail_ref[...], x], axis=0)
    y = w[CONV_WIDTH - 1:CONV_WIDTH, :] * x
    for j in range(CONV_WIDTH - 1):
        y = y + w[j:j + 1, :] * pltpu.roll(xx, CONV_WIDTH - 1 - j, axis=0)[8:, :]
    tail_ref[...] = x[L - 8:, :]
    return y


def _head_expand(n_heads, width):
    r = lax.broadcasted_iota(I32, (n_heads, n_heads * width), 0)
    c = lax.broadcasted_iota(I32, (n_heads, n_heads * width), 1)
    return ((c >= r * width) & (c < (r + 1) * width)).astype(F32)


def _ssd_body(x_ref, bm_ref, cm_ref, z_ref, dtc_ref, dtr_ref, wx_ref, wb_ref, wc_ref, bx_ref, bb_ref, bc_ref,
              dtbc_ref, dtbr_ref, alc_ref, alr_ref, dsk_ref, gain_ref, y_ref, st_scr, tx_scr, tb_scr, tc_scr):
    L = x_ref.shape[0]
    nh, hd = SSM_HEADS_PER_GROUP, SSM_HEAD_DIM

    @pl.when(pl.program_id(2) == 0)
    def _():
        st_scr[...] = jnp.zeros_like(st_scr)
        tx_scr[...] = jnp.zeros_like(tx_scr)
        tb_scr[...] = jnp.zeros_like(tb_scr)
        tc_scr[...] = jnp.zeros_like(tc_scr)

    xc = _silu(_causal_conv(x_ref[...], tx_scr, wx_ref[...]) + bx_ref[...])
    bc = _silu(_causal_conv(bm_ref[...], tb_scr, wb_ref[...]) + bb_ref[...])
    cc = _silu(_causal_conv(cm_ref[...], tc_scr, wc_ref[...]) + bc_ref[...])

    ri = lax.broadcasted_iota(I32, (L, L), 0)
    ci = lax.broadcasted_iota(I32, (L, L), 1)
    causal = ci <= ri
    dt_col = _softplus(dtc_ref[...] + dtbc_ref[...])
    acum = _dot_sel_lhs(causal, dt_col * (-jnp.exp(alc_ref[...])))
    dt_row = _softplus(dtr_ref[...] + dtbr_ref[...])
    acum_row = _dot_sel_rhs(dt_row * (-jnp.exp(alr_ref[...])), ri <= ci)

    expand = _head_expand(nh, hd)
    acum_e = _dot_sel_rhs(acum, expand)
    xdt = xc * _dot_sel_rhs(dt_col, expand)
    xdt_b = xdt.astype(BF16)
    cb = _bdot_nt(cc, bc)

    lane = lax.broadcasted_iota(I32, (L, 2 * hd), 1)
    parts = []
    for hp in range(nh // 2):
        xpair = xdt_b[:, 2 * hd * hp:2 * hd * (hp + 1)]
        acc = None
        for e in range(2):
            hh = 2 * hp + e
            seg = acum[:, hh:hh + 1] - acum_row[hh:hh + 1, :]
            dec = jnp.exp(jnp.where(causal, seg, -jnp.inf))
            xsel = jnp.where((lane >= hd * e) & (lane < hd * (e + 1)), xpair, jnp.zeros_like(xpair))
            t = _bdot(cb * dec, xsel)
            acc = t if acc is None else acc + t
        parts.append(acc)
    y = jnp.concatenate(parts, axis=1)

    st = st_scr[...]
    y = y + _bdot(cc, st) * jnp.exp(acum_e)
    a_last = acum_e[L - 1:L, :]
    st_scr[...] = jnp.exp(a_last) * st + _bdot_tn(bc, xdt * jnp.exp(a_last - acum_e))

    y = y + xc * dsk_ref[...]
    y = y * _silu(z_ref[...])
    y = y * lax.rsqrt(jnp.mean(y * y, axis=-1, keepdims=True) + EPS) * gain_ref[...]
    y_ref[...] = y.astype(y_ref.dtype)


def _ssd(proj, dt_col, dt_row, conv_w, conv_b, dt_bias, a_log, d_skip, gain, bsz, t):
    L = min(SSD_CHUNK, t)
    nc = t // L
    G, nh, gw, ns = SSM_GROUPS, SSM_HEADS_PER_GROUP, SSM_GROUP_WIDTH, SSM_STATE
    row = lambda b, g, c: b * nc + c
    x_blk = 10240 // gw
    z_blk = 8192 // gw
    b_blk = 12288 // ns
    c_blk = 12800 // ns
    wb_blk = G * gw // ns
    conv_b2 = conv_b.reshape(1, -1).astype(F32)
    f = lambda a: a.astype(F32)
    return pl.pallas_call(
        _ssd_body,
        grid=(bsz, G, nc),
        in_specs=[
            pl.BlockSpec((L, gw), lambda b, g, c: (row(b, g, c), x_blk + g)),
            pl.BlockSpec((L, ns), lambda b, g, c: (row(b, g, c), b_blk + g)),
            pl.BlockSpec((L, ns), lambda b, g, c: (row(b, g, c), c_blk + g)),
            pl.BlockSpec((L, gw), lambda b, g, c: (row(b, g, c), z_blk + g)),
            pl.BlockSpec((None, None, L, nh), lambda b, g, c: (b, g, c, 0)),
            pl.BlockSpec((None, None, nh, L), lambda b, g, c: (b, g, 0, c)),
            pl.BlockSpec((CONV_WIDTH, gw), lambda b, g, c: (0, g)),
            pl.BlockSpec((CONV_WIDTH, ns), lambda b, g, c: (0, wb_blk + g)),
            pl.BlockSpec((CONV_WIDTH, ns), lambda b, g, c: (0, wb_blk + G + g)),
            pl.BlockSpec((1, gw), lambda b, g, c: (0, g)),
            pl.BlockSpec((1, ns), lambda b, g, c: (0, wb_blk + g)),
            pl.BlockSpec((1, ns), lambda b, g, c: (0, wb_blk + G + g)),
            pl.BlockSpec((None, 1, nh), lambda b, g, c: (g, 0, 0)),
            pl.BlockSpec((None, nh, 1), lambda b, g, c: (g, 0, 0)),
            pl.BlockSpec((None, 1, nh), lambda b, g, c: (g, 0, 0)),
            pl.BlockSpec((None, nh, 1), lambda b, g, c: (g, 0, 0)),
            pl.BlockSpec((None, 1, gw), lambda b, g, c: (g, 0, 0)),
            pl.BlockSpec((None, 1, gw), lambda b, g, c: (g, 0, 0)),
        ],
        out_specs=pl.BlockSpec((L, gw), lambda b, g, c: (row(b, g, c), g)),
        out_shape=jax.ShapeDtypeStruct((bsz * t, G * gw), BF16),
        scratch_shapes=[pltpu.VMEM((ns, gw), F32), pltpu.VMEM((8, gw), F32),
                        pltpu.VMEM((8, ns), F32), pltpu.VMEM((8, ns), F32)],
        compiler_params=_cparams(("parallel", "parallel", "arbitrary"), 40),
        name="ssd",
    )(proj, proj, proj, proj, dt_col, dt_row, f(conv_w), f(conv_w), f(conv_w), conv_b2, conv_b2, conv_b2,
      f(dt_bias).reshape(G, 1, nh), f(dt_bias).reshape(G, nh, 1), f(a_log).reshape(G, 1, nh),
      f(a_log).reshape(G, nh, 1), jnp.repeat(f(d_skip), SSM_HEAD_DIM).reshape(G, 1, gw), f(gain).reshape(G, 1, gw))


def _layer_ab(x2, bsz, t, norm_g, w_in, ml_b_i, ml_b_f, ml_norm, ssm_conv_w, ssm_conv_b, ssm_dt_bias,
              ssm_a_log, ssm_d, ssm_norm, w_out):
    m = bsz * t
    small = jnp.concatenate([w_in[:, 8192:8200], w_in[:, 13320:13352]], axis=1)
    w_main = jnp.concatenate(
        [w_in[:, :8192], w_in[:, 8200:13320], jnp.pad(small, ((0, 0), (0, 512 - small.shape[1])))], axis=1
    ).astype(BF16)
    proj = _norm_matmul(x2, 0, D_MODEL, norm_g, w_main, out_dtype=F32, tn=1536, name="proj_ab")
    gate_cols = proj[:, 13312:13352]
    ml_gates = gate_cols[:, :8].reshape(bsz, t // min(ML_CHUNK, t), min(ML_CHUNK, t), 2, ML_HEADS)
    ml_gates = jnp.transpose(ml_gates, (0, 4, 1, 3, 2))
    dt_raw = gate_cols[:, 8:40].reshape(bsz, t, SSM_GROUPS, SSM_HEADS_PER_GROUP)
    dt_col = jnp.transpose(dt_raw, (0, 2, 1, 3))
    dt_row = jnp.transpose(dt_raw, (0, 2, 3, 1))
    y_a = _mlstm(proj, ml_gates, ml_b_i, ml_b_f, ml_norm, bsz, t)
    y_b = _ssd(proj, dt_col, dt_row, ssm_conv_w, ssm_conv_b, ssm_dt_bias, ssm_a_log, ssm_d, ssm_norm, bsz, t)
    return _out_proj([y_a, y_b], w_out.astype(BF16), x2, jnp.ones((D_MODEL,), F32), final_norm=False,
                     name="out_ab")


def _indexer_body(iq_ref, iw_ref, ik_ref, mask_ref, hi_scr, lo_scr, pos_scr, *, top_k):
    n_kt = hi_scr.shape[0]
    KT, Q = IDX_KT, IDX_Q
    i = pl.program_id(1)
    start = i * Q
    n_tiles = (start + Q + KT - 1) // KT
    iqb = iq_ref[...].astype(BF16)
    iw = iw_ref[...] * (IDX_HEADS ** -0.5 * IDX_DIM ** -0.5)
    qpos = start + lax.broadcasted_iota(I32, (1, Q), 1)
    limit = (lax.shift_right_logical(qpos, int(math.log2(DSA_CHUNK))) + 1) * DSA_CHUNK
    kiota = lax.broadcasted_iota(I32, (KT, 1), 0)

    def score_tile(tt, carry):
        acc = jnp.zeros((KT, Q), F32)
        for p in range(IDX_HEADS // 2):
            xq = iqb[:, 2 * IDX_DIM * p:2 * IDX_DIM * (p + 1)]
            s0 = _bdot_nt(ik_ref[tt, 0], xq)
            s1 = _bdot_nt(ik_ref[tt, 1], xq)
            acc = acc + iw[2 * p:2 * p + 1, :] * jnp.maximum(s0, 0.0)
            acc = acc + iw[2 * p + 1:2 * p + 2, :] * jnp.maximum(s1, 0.0)
        bits = lax.bitcast_convert_type(acc + 0.0, I32)
        key = jnp.where(bits < 0, bits ^ jnp.int32(0x7FFFFFFF), bits)
        key = jnp.where(tt * KT + kiota < limit, key, jnp.int32(INT_MIN))
        hi_scr[tt] = lax.shift_right_arithmetic(key, 16).astype(I16)
        lo_scr[tt] = ((key & jnp.int32(0xFFFF)) - jnp.int32(HALF16)).astype(I16)
        return carry

    lax.fori_loop(0, n_tiles, score_tile, 0)

    def count16(pred):
        def body(tt, accs):
            accs = list(accs)
            w = jnp.where(pred(tt), jnp.int16(1), jnp.int16(0))
            for r in range(KT // 16):
                accs[r % 4] = accs[r % 4] + w[16 * r:16 * (r + 1), :]
            return tuple(accs)
        zero = jnp.zeros((16, Q), I16)
        a0, a1, a2, a3 = lax.fori_loop(0, n_tiles, body, (zero, zero, zero, zero))
        return jnp.sum(((a0 + a1) + (a2 + a3)).astype(F32), axis=0, keepdims=True)

    def rows16(v):
        return jnp.concatenate([jnp.broadcast_to(v, (16, Q)).astype(I16)] * (KT // 16), axis=0)

    def bisect16(ref, n_above):
        t16 = jnp.full((1, Q), -HALF16, I32)
        for bit in range(15, -1, -1):
            cand = jnp.zeros_like(t16) if bit == 15 else t16 + jnp.int32(1 << bit)
            ct = rows16(cand)
            t16 = jnp.where(n_above + count16(lambda tt, ct=ct: ref[tt] >= ct) >= top_k, cand, t16)
        return t16

    thr_hi = bisect16(hi_scr, 0.0)
    th = rows16(thr_hi)
    n_hi_gt = count16(lambda tt: hi_scr[tt] > th)

    def keep_lo(tt, carry):
        lo_scr[tt] = jnp.where(hi_scr[tt] == th, lo_scr[tt], jnp.int16(-HALF16))
        return carry

    lax.fori_loop(0, n_tiles, keep_lo, 0)
    thr_lo = bisect16(lo_scr, n_hi_gt)
    tl = rows16(thr_lo)

    def pos16(tt):
        return jnp.broadcast_to(tt * KT + kiota, (KT, Q)).astype(I16)

    n_gt = n_hi_gt + count16(lambda tt: lo_scr[tt] > tl)
    n_ge = n_hi_gt + count16(lambda tt: lo_scr[tt] >= tl)
    room = top_k - n_gt
    has_thr = (thr_hi > -HALF16) | (thr_lo > -HALF16)
    pos_scr[...] = jnp.full((1, Q), 2 * n_kt * KT, I32)

    @pl.when(jnp.max(jnp.where(has_thr, n_ge, 0.0)) > top_k)
    def _():
        pcut = jnp.zeros((1, Q), I32)
        for bit in range(int(math.log2(n_kt * KT)), -1, -1):
            cand = pcut + jnp.int32(1 << bit)
            c16 = rows16(cand)
            n_tie = count16(lambda tt, c16=c16: (hi_scr[tt] == th) & (lo_scr[tt] == tl) & (pos16(tt) < c16))
            pcut = jnp.where(n_tie <= room, cand, pcut)
        pos_scr[...] = pcut

    p16 = rows16(pos_scr[...])
    lim16 = rows16(limit)

    def write_tile(tt, carry):
        hi, lo, pos = hi_scr[tt], lo_scr[tt], pos16(tt)
        sel = (hi > th) | ((hi == th) & ((lo > tl) | ((lo == tl) & (pos < p16))))
        sel = sel & (pos < lim16)
        mask_ref[tt] = jnp.transpose(jnp.where(sel, jnp.int16(1), jnp.int16(0)).astype(F32)).astype(jnp.int8)
        return carry

    lax.fori_loop(0, n_tiles, write_tile, 0)

    def zero_tile(tt, carry):
        mask_ref[tt] = jnp.zeros((Q, KT), jnp.int8)
        return carry

    lax.fori_loop(n_tiles, n_kt, zero_tile, 0)


def _indexer(proj, iq_blk, iw_t, ik_pad, bsz, t):
    nq = t // IDX_Q
    n_kt = t // IDX_KT
    top_k = min(TOPK_MAX, t // 4)
    return pl.pallas_call(
        functools.partial(_indexer_body, top_k=top_k),
        grid=(bsz, nq),
        in_specs=[
            pl.BlockSpec((IDX_Q, IDX_HEADS * IDX_DIM), lambda b, i: (b * nq + i, iq_blk)),
            pl.BlockSpec((None, IDX_HEADS, IDX_Q), lambda b, i: (b * nq + i, 0, 0)),
            pl.BlockSpec((None, n_kt, 2, IDX_KT, 2 * IDX_DIM), lambda b, i: (b, 0, 0, 0, 0)),
        ],
        out_specs=pl.BlockSpec((None, n_kt, IDX_Q, IDX_KT), lambda b, i: (b * nq + i, 0, 0, 0)),
        out_shape=jax.ShapeDtypeStruct((bsz * nq, n_kt, IDX_Q, IDX_KT), jnp.int8),
        scratch_shapes=[pltpu.VMEM((n_kt, IDX_KT, IDX_Q), I16), pltpu.VMEM((n_kt, IDX_KT, IDX_Q), I16),
                        pltpu.VMEM((1, IDX_Q), I32)],
        compiler_params=_cparams(("parallel", "arbitrary"), 48),
        name="dsa_indexer",
    )(proj, iw_t, ik_pad)


NEG_BIG = -3.0e38
LOG2_E = math.log2(math.e)


def _attn_body(q_ref, k_ref, v_ref, mask_ref, z_ref, tb_ref, o_ref, acc_scr, m_scr):
    T = ATT_T
    D = ATT_HEAD_DIM
    nsub = T // ATT_SUB
    i = pl.program_id(1)
    j = pl.program_id(2)

    @pl.when(j == 0)
    def _():
        acc_scr[...] = jnp.zeros_like(acc_scr)
        m_scr[...] = jnp.full_like(m_scr, NEG_BIG)

    def tile_step(near):
        sel = jnp.concatenate([mask_ref[a, 0].astype(F32) for a in range(mask_ref.shape[0])], axis=0)
        mask_bias = (sel - 1.0) * (-NEG_BIG)
        ones = jnp.ones((T, D), BF16)

        def pair_body(p, carry):
            q2, k2, v2 = q_ref[p], k_ref[p], v_ref[p]
            m_old = [m_scr[2 * p + e] for e in range(2)]
            a_old = [acc_scr[2 * p + e] for e in range(2)]
            new = []
            logits = [_bdot_nt(q2[:, D * e:D * (e + 1)], k2[:, D * e:D * (e + 1)]) for e in range(2)]
            for e in range(2):
                s = logits[e] + mask_bias
                if near is not None:
                    tb = tb_ref[2 * p + e]
                    zero = jnp.zeros((ATT_SUB, ATT_SUB), F32)
                    rows = []
                    for a in range(nsub):
                        cols = []
                        for bb in range(nsub):
                            rel_blocks = (bb - a) if near == "diag" else (bb - a - nsub)
                            cols.append(tb[0] if rel_blocks == 0 else tb[1] if rel_blocks == -1 else zero)
                        rows.append(jnp.concatenate(cols, axis=1))
                    s = s + jnp.concatenate(rows, axis=0)
                m_new = jnp.maximum(m_old[e], jnp.max(s, axis=1, keepdims=True))
                alpha = jnp.exp2(m_old[e] - m_new)
                pr = jnp.exp2(s - jnp.concatenate([m_new] * (T // LANES), axis=1))
                v_aug = jnp.concatenate([v2[:, D * e:D * (e + 1)], ones], axis=1)
                a_new = jnp.concatenate([alpha] * (2 * D // LANES), axis=1) * a_old[e] + _bdot(pr, v_aug)
                new.append((m_new, a_new))
            for e in range(2):
                m_scr[2 * p + e] = new[e][0]
                acc_scr[2 * p + e] = new[e][1]
            return carry

        lax.fori_loop(0, ATT_HEADS // 2, pair_body, 0, unroll=8)

    @pl.when(j < i - 1)
    def _():
        tile_step(None)

    @pl.when(j == i - 1)
    def _():
        tile_step("prev")

    @pl.when(j == i)
    def _():
        tile_step("diag")
        for p in range(ATT_HEADS // 2):
            for e in range(2):
                hh = 2 * p + e
                a = acc_scr[hh]
                o = a[:, :D] / a[:, D:]
                o = o * _silu(z_ref[p, :, D * e:D * (e + 1)].astype(F32))
                o_ref[:, D * hh:D * (hh + 1)] = o.astype(o_ref.dtype)


def _attention(qz, kv, mask, tb, bsz, t):
    T = ATT_T
    nq = t // T
    npair = ATT_HEADS // 2
    pw = 2 * ATT_HEAD_DIM
    nrow = T // IDX_Q
    ncol = T // IDX_KT
    return pl.pallas_call(
        _attn_body,
        grid=(bsz, nq, nq),
        in_specs=[
            pl.BlockSpec((npair, T, pw), lambda b, i, j: (0, b * nq + i, 0)),
            pl.BlockSpec((npair, T, pw), lambda b, i, j: (0, b * nq + jnp.minimum(j, i), 0)),
            pl.BlockSpec((npair, T, pw), lambda b, i, j: (1, b * nq + jnp.minimum(j, i), 0)),
            pl.BlockSpec((nrow, ncol, IDX_Q, IDX_KT), lambda b, i, j: (b * nq + i, jnp.minimum(j, i), 0, 0)),
            pl.BlockSpec((npair, T, pw), lambda b, i, j: (1, b * nq + i, 0)),
            pl.BlockSpec((ATT_HEADS, 2, ATT_SUB, ATT_SUB), lambda b, i, j: (0, 0, 0, 0)),
        ],
        out_specs=pl.BlockSpec((T, ATT_HEADS * ATT_HEAD_DIM), lambda b, i, j: (b * nq + i, 0)),
        out_shape=jax.ShapeDtypeStruct((bsz * t, ATT_HEADS * ATT_HEAD_DIM), BF16),
        scratch_shapes=[pltpu.VMEM((ATT_HEADS, T, 2 * ATT_HEAD_DIM), F32), pltpu.VMEM((ATT_HEADS, T, LANES), F32)],
        compiler_params=_cparams(("parallel", "parallel", "arbitrary"), 56),
        name="dsa_attention",
    )(qz, kv, kv, mask, qz, tb)


def _t5_bucket(rel):
    nb = REL_BUCKETS // 2
    max_exact = nb // 2
    ret = jnp.where(rel > 0, nb, 0)
    n = jnp.abs(rel)
    nf = jnp.maximum(n, max_exact).astype(F32)
    large = max_exact + (jnp.log(nf / max_exact) / math.log(REL_MAX_DIST / max_exact)
                         * (nb - max_exact)).astype(jnp.int32)
    large = jnp.minimum(large, nb - 1)
    return ret + jnp.where(n < max_exact, n, large)


def _rel_bias_blocks(rel_bias):
    r = jnp.arange(ATT_SUB)[:, None]
    c = jnp.arange(ATT_SUB)[None, :]
    rel = jnp.stack([c - r, c - r - ATT_SUB])
    assert ATT_SUB >= REL_MAX_DIST and ATT_SUB % DSA_CHUNK == 0
    rb = rel_bias.astype(F32)
    far = rb[_t5_bucket(jnp.int32(-2 * ATT_SUB))]
    onehot = (_t5_bucket(rel)[..., None] == jnp.arange(REL_BUCKETS)).astype(F32)
    table = jnp.einsum("vrcb,bh->hvrc", onehot, rb - far, precision=HI)
    return table * LOG2_E


GDN_PREP_ROWS = 256
GDN_SOLVE_LANES = 256
GDN_SCAN_CHUNKS = 2


def _conv_with_prev(x, prev, w):
    xx = jnp.concatenate([prev, x], axis=0)
    y = w[CONV_WIDTH - 1:CONV_WIDTH, :] * x
    for j in range(CONV_WIDTH - 1):
        y = y + w[j:j + 1, :] * pltpu.roll(xx, CONV_WIDTH - 1 - j, axis=0)[8:, :]
    return y


def _gdn_prep_body(v_ref, q_ref, k_ref, pv_ref, pq_ref, pk_ref, bc_ref, ac_ref, ar_ref, wq_ref, wk_ref, wv_ref,
                   dtbc_ref, dtbr_ref, alc_ref, alr_ref,
                   qn_ref, rhs_ref, lu_ref, a_ref, egc_ref, egl_ref):
    L = GDN_CHUNK
    D = GDN_HEAD_DIM
    nv = GDN_V_HEADS // GDN_GROUPS
    rep = GDN_V_HEADS // GDN_QK_HEADS
    seq_start = pl.program_id(2) == 0

    def conv(x_ref, p_ref, w_ref):
        prev = jnp.where(seq_start, 0.0, p_ref[...])
        return _silu(_conv_with_prev(x_ref[...], prev, w_ref[...]))

    qc = conv(q_ref, pq_ref, wq_ref)
    kc = conv(k_ref, pk_ref, wk_ref)
    vc = conv(v_ref, pv_ref, wv_ref)
    qn, kn = [], []
    for kh in range(nv // rep):
        qh = qc[:, D * kh:D * (kh + 1)]
        kk_ = kc[:, D * kh:D * (kh + 1)]
        qh = qh * lax.rsqrt(jnp.sum(qh * qh, axis=-1, keepdims=True) + EPS) * (D ** -0.5)
        kn.append(kk_ * lax.rsqrt(jnp.sum(kk_ * kk_, axis=-1, keepdims=True) + EPS))
        qn.append(qh.astype(BF16))
        qn_ref[:, D * kh:D * (kh + 1)] = qn[kh]

    beta = _sigmoid(bc_ref[...])
    g_col = -jnp.exp(alc_ref[...]) * _softplus(ac_ref[...] + dtbc_ref[...])
    ri = lax.broadcasted_iota(I32, (L, L), 0)
    ci = lax.broadcasted_iota(I32, (L, L), 1)
    causal = ci <= ri
    strict = ci < ri
    for cc in range(a_ref.shape[0]):
        rows = slice(L * cc, L * (cc + 1))
        gc_col = _dot_sel_lhs(causal, g_col[rows, :])
        g_row = -jnp.exp(alr_ref[...]) * _softplus(ar_ref[cc] + dtbr_ref[...])
        gc_row = _dot_sel_rhs(g_row, ri <= ci)
        eg_col = jnp.exp(gc_col)
        egc_ref[rows, :] = eg_col
        for kh in range(nv // rep):
            kf = kn[kh][rows, :]
            kb = kf.astype(BF16)
            kk = _bdot_nt(kb, kb)
            qk = _bdot_nt(qn[kh][rows, :], kb)
            for e in range(rep):
                vh = rep * kh + e
                gcol = gc_col[:, vh:vh + 1]
                bcol = beta[rows, vh:vh + 1]
                dec = jnp.exp(jnp.where(causal, gcol - gc_row[vh:vh + 1, :], -jnp.inf))
                a_ref[cc, vh] = jnp.where(strict, bcol * kk * dec, 0.0).astype(a_ref.dtype)
                g_last = gcol[L - 1:L, :]
                kw_t = jnp.transpose(kf * jnp.exp(g_last - gcol))
                lu_ref[cc, vh] = jnp.concatenate([qk * dec, kw_t], axis=0).astype(lu_ref.dtype)
                egl_ref[cc, vh:vh + 1, :] = jnp.broadcast_to(jnp.exp(g_last), (1, LANES))
                rhs_ref[rows, 2 * D * vh:2 * D * (vh + 1)] = jnp.concatenate(
                    [vc[rows, D * vh:D * (vh + 1)] * bcol, kf * (bcol * eg_col[:, vh:vh + 1])], axis=1
                ).astype(rhs_ref.dtype)


def _gdn_solve_body(a_in_ref, t_ref, t_scr, a_ref):
    L, _, PB = a_in_ref.shape
    UN = 8
    sub = lax.broadcasted_iota(I32, (UN, PB), 0)
    t_scr[...] = jnp.zeros_like(t_scr)
    a_ref[...] = a_in_ref[...].astype(F32)

    for ib in range(L // UN):
        def row_body(r, carry, ib=ib):
            i = ib * UN + r
            acc = [jnp.zeros((UN, PB), F32) for _ in range(ib + 1)]
            for jb in range(ib + 1):
                a_blk = a_ref[i, UN * jb:UN * (jb + 1), :]
                for jj in range(UN):
                    a_ij = a_blk[jj:jj + 1, :]
                    for cg in range(jb + 1):
                        acc[cg] = acc[cg] + a_ij * t_scr[UN * jb + jj, UN * cg:UN * (cg + 1), :]
            for cg in range(ib + 1):
                unit = jnp.where(sub == r, 1.0, 0.0) if cg == ib else 0.0
                t_scr[i, UN * cg:UN * (cg + 1), :] = unit - acc[cg]
            return carry

        lax.fori_loop(0, UN, row_body, 0)
    t_ref[...] = t_scr[...].astype(t_ref.dtype)


def _gdn_scan_body(qn_ref, rhs_ref, lu_ref, t_ref, z_ref, egc_ref, egl_ref, gain_ref, y_ref, s_scr):
    L = GDN_CHUNK
    D = GDN_HEAD_DIM
    nv = GDN_V_HEADS // GDN_GROUPS
    rep = GDN_V_HEADS // GDN_QK_HEADS
    ncs = t_ref.shape[0]

    @pl.when(pl.program_id(2) == 0)
    def _():
        s_scr[...] = jnp.zeros_like(s_scr)

    heads = range(nv)
    sol = [[_bdot(t_ref[cc, vh], rhs_ref[L * cc:L * (cc + 1), 2 * D * vh:2 * D * (vh + 1)]) for vh in heads]
           for cc in range(ncs)]
    s_st = [s_scr[vh] for vh in heads]
    for cc in range(ncs):
        rows = slice(L * cc, L * (cc + 1))
        eg_col = egc_ref[rows, :]
        r_s = [_bdot(jnp.concatenate([sol[cc][vh][:, D:].astype(BF16),
                                      qn_ref[rows, D * (vh // rep):D * (vh // rep + 1)]], axis=0), s_st[vh])
               for vh in heads]
        u = [sol[cc][vh][:, :D] - r_s[vh][:L] for vh in heads]
        r_u = [_bdot(lu_ref[cc, vh], u[vh]) for vh in heads]
        s_st = [egl_ref[cc, vh:vh + 1, :] * s_st[vh] + r_u[vh][L:] for vh in heads]
        for vh in heads:
            o = r_s[vh][L:] * eg_col[:, vh:vh + 1] + r_u[vh][:L]
            o = o * lax.rsqrt(jnp.mean(o * o, axis=-1, keepdims=True) + EPS) * gain_ref[...]
            y_ref[rows, D * vh:D * (vh + 1)] = (o * _silu(z_ref[rows, D * vh:D * (vh + 1)])).astype(y_ref.dtype)
    for vh in heads:
        s_scr[vh] = s_st[vh]


def _gdn(proj, b_col, a_col, a_row, conv_w, dt_bias, a_log, gain, bsz, t):
    L = GDN_CHUNK
    nc = t // L
    G = GDN_GROUPS
    nv = GDN_V_HEADS // G
    qw = GDN_QK_HEADS // G * GDN_HEAD_DIM
    vw = nv * GDN_HEAD_DIM
    m = bsz * t
    z_blk, q_blk, k_blk = 4096 // vw, 8192 // qw, 10240 // qw
    f = lambda a: a.astype(F32)
    cw = f(conv_w)
    pc_spec = pl.BlockSpec((None, 1, nv), lambda b, g, c: (g, 0, 0))
    pr_spec = pl.BlockSpec((None, nv, 1), lambda b, g, c: (g, 0, 0))
    wq_spec = pl.BlockSpec((CONV_WIDTH, qw), lambda b, g, c: (0, g))
    wk_spec = pl.BlockSpec((CONV_WIDTH, qw), lambda b, g, c: (0, G + g))
    wv_spec = pl.BlockSpec((CONV_WIDTH, vw), lambda b, g, c: (0, (2 * G * qw) // vw + g))

    LB = min(GDN_PREP_ROWS, t)
    ncb = LB // L
    nb = t // LB
    rowb = lambda b, g, c: b * nb + c
    prev8 = lambda b, g, c: jnp.maximum((b * t + c * LB) // 8 - 1, 0)
    D = GDN_HEAD_DIM
    qn, rhs, lu, a_mat, egc, egl = pl.pallas_call(
        _gdn_prep_body,
        grid=(bsz, G, nb),
        in_specs=[
            pl.BlockSpec((LB, vw), lambda b, g, c: (rowb(b, g, c), g)),
            pl.BlockSpec((LB, qw), lambda b, g, c: (rowb(b, g, c), q_blk + g)),
            pl.BlockSpec((LB, qw), lambda b, g, c: (rowb(b, g, c), k_blk + g)),
            pl.BlockSpec((8, vw), lambda b, g, c: (prev8(b, g, c), g)),
            pl.BlockSpec((8, qw), lambda b, g, c: (prev8(b, g, c), q_blk + g)),
            pl.BlockSpec((8, qw), lambda b, g, c: (prev8(b, g, c), k_blk + g)),
            pl.BlockSpec((None, None, LB, nv), lambda b, g, c: (b, g, c, 0)),
            pl.BlockSpec((None, None, LB, nv), lambda b, g, c: (b, g, c, 0)),
            pl.BlockSpec((None, None, ncb, nv, L), lambda b, g, c: (b, g, c, 0, 0)),
            wq_spec, wk_spec, wv_spec, pc_spec, pr_spec, pc_spec, pr_spec,
        ],
        out_specs=[
            pl.BlockSpec((LB, qw), lambda b, g, c: (rowb(b, g, c), g)),
            pl.BlockSpec((LB, 2 * vw), lambda b, g, c: (rowb(b, g, c), g)),
            pl.BlockSpec((None, None, ncb, nv, L + D, L), lambda b, g, c: (b, g, c, 0, 0, 0)),
            pl.BlockSpec((None, None, ncb, nv, L, L), lambda b, g, c: (b, g, c, 0, 0, 0)),
            pl.BlockSpec((None, None, LB, nv), lambda b, g, c: (b, g, c, 0)),
            pl.BlockSpec((None, None, ncb, nv, LANES), lambda b, g, c: (b, g, c, 0, 0)),
        ],
        out_shape=[
            jax.ShapeDtypeStruct((m, G * qw), BF16),
            jax.ShapeDtypeStruct((m, 2 * G * vw), BF16),
            jax.ShapeDtypeStruct((bsz, G, nc, nv, L + D, L), BF16),
            jax.ShapeDtypeStruct((bsz, G, nc, nv, L, L), BF16),
            jax.ShapeDtypeStruct((bsz, G, t, nv), F32),
            jax.ShapeDtypeStruct((bsz, G, nc, nv, LANES), F32),
        ],
        compiler_params=_cparams(("parallel", "parallel", "parallel"), 48),
        name="gdn_prep",
    )(proj, proj, proj, proj, proj, proj, b_col, a_col, a_row, cw, cw, cw,
      f(dt_bias).reshape(G, 1, nv), f(dt_bias).reshape(G, nv, 1), f(a_log).reshape(G, 1, nv),
      f(a_log).reshape(G, nv, 1))

    n_prob = bsz * G * nc * nv
    pb = min(GDN_SOLVE_LANES, n_prob)
    a_t = jnp.transpose(a_mat.reshape(n_prob, L, L), (1, 2, 0))
    t_t = pl.pallas_call(
        _gdn_solve_body,
        grid=(n_prob // pb,),
        in_specs=[pl.BlockSpec((L, L, pb), lambda p: (0, 0, p))],
        out_specs=pl.BlockSpec((L, L, pb), lambda p: (0, 0, p)),
        out_shape=jax.ShapeDtypeStruct((L, L, n_prob), BF16),
        scratch_shapes=[pltpu.VMEM((L, L, pb), F32), pltpu.VMEM((L, L, pb), F32)],
        compiler_params=_cparams(("parallel",), 40),
        name="gdn_solve",
    )(a_t)
    t_mat = jnp.transpose(t_t, (2, 0, 1)).reshape(bsz, G, nc, nv, L, L)

    ncs = min(GDN_SCAN_CHUNKS, nc)
    ns = nc // ncs
    row = lambda b, g, c: b * ns + c
    return pl.pallas_call(
        _gdn_scan_body,
        grid=(bsz, G, ns),
        in_specs=[
            pl.BlockSpec((ncs * L, qw), lambda b, g, c: (row(b, g, c), g)),
            pl.BlockSpec((ncs * L, 2 * vw), lambda b, g, c: (row(b, g, c), g)),
            pl.BlockSpec((None, None, ncs, nv, L + D, L), lambda b, g, c: (b, g, c, 0, 0, 0)),
            pl.BlockSpec((None, None, ncs, nv, L, L), lambda b, g, c: (b, g, c, 0, 0, 0)),
            pl.BlockSpec((ncs * L, vw), lambda b, g, c: (row(b, g, c), z_blk + g)),
            pl.BlockSpec((None, None, ncs * L, nv), lambda b, g, c: (b, g, c, 0)),
            pl.BlockSpec((None, None, ncs, nv, LANES), lambda b, g, c: (b, g, c, 0, 0)),
            pl.BlockSpec((1, GDN_HEAD_DIM), lambda b, g, c: (0, 0)),
        ],
        out_specs=pl.BlockSpec((ncs * L, vw), lambda b, g, c: (row(b, g, c), g)),
        out_shape=jax.ShapeDtypeStruct((m, GDN_V_HEADS * GDN_HEAD_DIM), BF16),
        scratch_shapes=[pltpu.VMEM((nv, GDN_HEAD_DIM, GDN_HEAD_DIM), F32)],
        compiler_params=_cparams(("parallel", "parallel", "arbitrary"), 40),
        name="gdn_scan",
    )(qn, rhs, lu, t_mat, proj, egc, egl, f(gain).reshape(1, GDN_HEAD_DIM))


def _layer_cd(x2, bsz, t, norm_g, w_in, kv_norm, w_uk, w_uv, gdn_conv_w, gdn_dt_bias, gdn_a_log, gdn_norm,
              w_out, rel_bias, final_g, final_norm):
    m = bsz * t
    small = jnp.concatenate([w_in[:, 5632:5712], w_in[:, 18000:18064]], axis=1)
    w_main = jnp.concatenate(
        [w_in[:, 9808:13904], w_in[:, 13904:18000], w_in[:, 5712:9808], w_in[:, 4608:5632], w_in[:, 2048:2560],
         jnp.pad(small, ((0, 0), (0, 512 - small.shape[1])))], axis=1).astype(BF16)
    proj = _norm_matmul(x2, 0, D_MODEL, norm_g, w_main, out_dtype=F32, tn=1024, name="proj_cd")
    w_attn = jnp.concatenate([w_in[:, :2048] * (ATT_HEAD_DIM ** -0.5 * LOG2_E), w_in[:, 2560:4608]],
                             axis=1).astype(BF16)
    qz = _norm_matmul(x2, 0, D_MODEL, norm_g, w_attn, out_dtype=BF16, blocked_width=2 * ATT_HEAD_DIM,
                      tn=1024, name="proj_cd_attn")
    w_kv = jnp.concatenate([w_uk.reshape(KV_RANK, -1), w_uv.reshape(KV_RANK, -1)], axis=1).astype(BF16)
    kv = _norm_matmul(proj, 13312 // KV_RANK, KV_RANK, kv_norm, w_kv, out_dtype=BF16,
                      blocked_width=2 * ATT_HEAD_DIM, tn=1024, name="dsa_kv")

    gate_cols = proj[:, 13824:13824 + 144]
    n_kt = t // IDX_KT
    ik = gate_cols[:, :IDX_DIM].reshape(bsz, n_kt, 1, IDX_KT, IDX_DIM).astype(BF16)
    zk = jnp.zeros_like(ik)
    ik_pad = jnp.concatenate([jnp.concatenate([ik, zk], axis=-1), jnp.concatenate([zk, ik], axis=-1)], axis=2)
    iw_t = jnp.transpose(gate_cols[:, IDX_DIM:IDX_DIM + IDX_HEADS].reshape(m // IDX_Q, IDX_Q, IDX_HEADS), (0, 2, 1))
    mask = _indexer(proj, 12288 // (IDX_HEADS * IDX_DIM), iw_t, ik_pad, bsz, t)
    y_c = _attention(qz, kv, mask, _rel_bias_blocks(rel_bias), bsz, t)

    nv = GDN_V_HEADS // GDN_GROUPS
    b_pre = gate_cols[:, 80:112].reshape(bsz, t, GDN_GROUPS, nv)
    a_pre = gate_cols[:, 112:144].reshape(bsz, t, GDN_GROUPS, nv)
    col = lambda a: jnp.transpose(a, (0, 2, 1, 3))
    rowf = lambda a: jnp.transpose(a.reshape(bsz, t // GDN_CHUNK, GDN_CHUNK, GDN_GROUPS, nv), (0, 3, 1, 4, 2))
    y_d = _gdn(proj, col(b_pre), col(a_pre), rowf(a_pre), gdn_conv_w, gdn_dt_bias, gdn_a_log, gdn_norm, bsz, t)
    return _out_proj([y_c, y_d], w_out.astype(BF16), x2, final_g, final_norm=final_norm, name="out_cd")


def kernel(x, rel_bias, ab_norm, ab_w_in, ab_ml_b_i, ab_ml_b_f, ab_ml_norm, ab_ssm_conv_w, ab_ssm_conv_b, ab_ssm_dt_bias, ab_ssm_a_log, ab_ssm_d, ab_ssm_norm, ab_w_out, cd_norm, cd_w_in, cd_kv_norm, cd_w_uk, cd_w_uv, cd_gdn_conv_w, cd_gdn_dt_bias, cd_gdn_a_log, cd_gdn_norm, cd_w_out, final_norm):
    bsz, t, d = x.shape
    x2 = x.reshape(bsz * t, d)
    x2 = _layer_ab(x2, bsz, t, ab_norm[0], ab_w_in[0], ab_ml_b_i[0], ab_ml_b_f[0], ab_ml_norm[0],
                   ab_ssm_conv_w[0], ab_ssm_conv_b[0], ab_ssm_dt_bias[0], ab_ssm_a_log[0], ab_ssm_d[0],
                   ab_ssm_norm[0], ab_w_out[0])
    x2 = _layer_cd(x2, bsz, t, cd_norm[0], cd_w_in[0], cd_kv_norm[0], cd_w_uk[0], cd_w_uv[0], cd_gdn_conv_w[0],
                   cd_gdn_dt_bias[0], cd_gdn_a_log[0], cd_gdn_norm[0], cd_w_out[0], rel_bias, final_norm, True)
    return x2.reshape(bsz, t, d)
```

```python
import functools
import math

import jax
import jax.numpy as jnp
from jax import lax
from jax.experimental import pallas as pl
from jax.experimental.pallas import tpu as pltpu

F32 = jnp.float32
BF16 = jnp.bfloat16
I32 = jnp.int32
I16 = jnp.int16
HI = lax.Precision.HIGHEST

EPS = 1e-6
CONV_WIDTH = 4
D_MODEL = 2048

ML_HEADS = 4
ML_DV = 512
ML_DK = 256

SSM_GROUPS = 4
SSM_HEADS_PER_GROUP = 8
SSM_HEAD_DIM = 64
SSM_STATE = 128
SSM_GROUP_WIDTH = SSM_HEADS_PER_GROUP * SSM_HEAD_DIM

ATT_HEADS = 16
ATT_HEAD_DIM = 128
KV_RANK = 512
IDX_HEADS = 16
IDX_DIM = 64
TOPK_MAX = 256
DSA_CHUNK = 64
REL_BUCKETS = 32
REL_MAX_DIST = 128

GDN_QK_HEADS = 16
GDN_V_HEADS = 32
GDN_HEAD_DIM = 128
GDN_GROUPS = 2
GDN_CHUNK = 64

LANES = 128

ML_CHUNK = 1024
SSD_CHUNK = 256
IDX_Q = 512
IDX_KT = 512
ATT_T = 512
ATT_SUB = 128
INT_MIN = -(2 ** 31)
HALF16 = 2 ** 15


def _cparams(sem, vmem_mib):
    return pltpu.CompilerParams(dimension_semantics=sem, vmem_limit_bytes=vmem_mib * 1024 * 1024)


def _bdot(a, b):
    return jnp.dot(a.astype(BF16), b.astype(BF16), preferred_element_type=F32)


def _bdot_nt(a, b):
    return lax.dot_general(a.astype(BF16), b.astype(BF16), (((1,), (1,)), ((), ())), preferred_element_type=F32)


def _bdot_tn(a, b):
    return lax.dot_general(a.astype(BF16), b.astype(BF16), (((0,), (0,)), ((), ())), preferred_element_type=F32)


def _split3(x):
    hi = x.astype(BF16)
    r = x - hi.astype(F32)
    mid = r.astype(BF16)
    return hi, mid, (r - mid.astype(F32)).astype(BF16)


def _as_bf16_01(sel):
    return jnp.where(sel, 1.0, 0.0).astype(BF16) if sel.dtype == jnp.bool_ else sel.astype(BF16)


def _dot_sel_lhs(sel, x):
    s = _as_bf16_01(sel)
    return sum(jnp.dot(s, p, preferred_element_type=F32) for p in _split3(x))


def _dot_sel_rhs(x, sel):
    s = _as_bf16_01(sel)
    return sum(jnp.dot(p, s, preferred_element_type=F32) for p in _split3(x))


def _softplus(x):
    return jnp.maximum(x, 0.0) + jnp.log1p(jnp.exp(-jnp.abs(x)))


def _sigmoid(x):
    return 0.5 * jnp.tanh(0.5 * x) + 0.5


def _silu(x):
    return x * _sigmoid(x)


def _norm_matmul_body(x_ref, g_ref, w_ref, o_ref, h_ref, *, blocked):
    @pl.when(pl.program_id(1) == 0)
    def _():
        x = x_ref[...].astype(F32)
        ms = jnp.mean(x * x, axis=-1, keepdims=True)
        h_ref[...] = (x * lax.rsqrt(ms + EPS) * g_ref[...]).astype(h_ref.dtype)

    acc = jnp.dot(h_ref[...], w_ref[...], preferred_element_type=F32)
    if blocked:
        bw = o_ref.shape[-1]
        for t in range(o_ref.shape[0]):
            o_ref[t] = acc[:, t * bw:(t + 1) * bw].astype(o_ref.dtype)
    else:
        o_ref[...] = acc.astype(o_ref.dtype)


def _norm_matmul(x, x_col_block, k_dim, gain, w, *, out_dtype, blocked_width=None, tm=1024, tn=512, name):
    m = x.shape[0]
    n = w.shape[1]
    tm = min(tm, m)
    assert m % tm == 0 and n % tn == 0 and w.shape[0] == k_dim
    grid = (m // tm, n // tn)
    if blocked_width is None:
        out_shape = jax.ShapeDtypeStruct((m, n), out_dtype)
        out_spec = pl.BlockSpec((tm, tn), lambda i, j: (i, j))
    else:
        nb = tn // blocked_width
        out_shape = jax.ShapeDtypeStruct((n // blocked_width, m, blocked_width), out_dtype)
        out_spec = pl.BlockSpec((nb, tm, blocked_width), lambda i, j: (j, i, 0))
    return pl.pallas_call(
        functools.partial(_norm_matmul_body, blocked=blocked_width is not None),
        grid=grid,
        in_specs=[
            pl.BlockSpec((tm, k_dim), lambda i, j: (i, x_col_block), pipeline_mode=pl.Buffered(1)),
            pl.BlockSpec((1, k_dim), lambda i, j: (0, 0)),
            pl.BlockSpec((k_dim, tn), lambda i, j: (0, j)),
        ],
        out_specs=out_spec,
        out_shape=out_shape,
        scratch_shapes=[pltpu.VMEM((tm, k_dim), BF16)],
        compiler_params=_cparams(("parallel", "arbitrary"), 48),
        name=name,
    )(x, gain.reshape(1, k_dim).astype(F32), w)


def _out_proj_body(*refs, n_y, k_tiles, final_norm):
    y_refs = refs[:n_y]
    w_ref, res_ref, g_ref, o_ref, acc_ref = refs[n_y:]
    k = pl.program_id(1)

    @pl.when(k == 0)
    def _():
        acc_ref[...] = jnp.zeros_like(acc_ref)

    lo = 0
    for y_ref, nk in zip(y_refs, k_tiles):
        @pl.when((k >= lo) & (k < lo + nk))
        def _(y_ref=y_ref):
            acc_ref[...] += jnp.dot(y_ref[...], w_ref[...], preferred_element_type=F32)
        lo += nk

    @pl.when(k == lo - 1)
    def _():
        xn = res_ref[...] + acc_ref[...]
        if final_norm:
            ms = jnp.mean(xn * xn, axis=-1, keepdims=True)
            xn = xn * lax.rsqrt(ms + EPS) * g_ref[...]
        o_ref[...] = xn


def _out_proj(ys, w, res, gain, *, final_norm, tm=512, tk=1024, name):
    m, d = res.shape
    tm = min(tm, m)
    k_tiles = [y.shape[1] // tk for y in ys]
    assert sum(k_tiles) * tk == w.shape[0]
    y_specs = []
    lo = 0
    for nk in k_tiles:
        y_specs.append(pl.BlockSpec((tm, tk), lambda i, k, lo=lo, nk=nk: (i, jnp.clip(k - lo, 0, nk - 1))))
        lo += nk
    return pl.pallas_call(
        functools.partial(_out_proj_body, n_y=len(ys), k_tiles=k_tiles, final_norm=final_norm),
        grid=(m // tm, lo),
        in_specs=y_specs + [
            pl.BlockSpec((tk, d), lambda i, k: (k, 0)),
            pl.BlockSpec((tm, d), lambda i, k: (i, 0)),
            pl.BlockSpec((1, d), lambda i, k: (0, 0)),
        ],
        out_specs=pl.BlockSpec((tm, d), lambda i, k: (i, 0)),
        out_shape=jax.ShapeDtypeStruct((m, d), F32),
        scratch_shapes=[pltpu.VMEM((tm, d), F32)],
        compiler_params=_cparams(("parallel", "arbitrary"), 48),
        name=name,
    )(*ys, w, res, gain.reshape(1, d).astype(F32))


def _mlstm_body(bi_ref, bf_ref, q_ref, k_ref, v_ref, op_ref, z_ref, g_ref, gain_ref, y_ref,
                c_scr, n_scr, m_scr):
    L = q_ref.shape[0]
    h = pl.program_id(1)

    @pl.when(pl.program_id(2) == 0)
    def _():
        c_scr[...] = jnp.zeros_like(c_scr)
        n_scr[...] = jnp.zeros_like(n_scr)
        m_scr[...] = jnp.zeros_like(m_scr)

    gates = g_ref[...]
    ic = gates[0:1, :] + bi_ref[h]
    fpre = gates[1:2, :] + bf_ref[h]
    fc = -_softplus(-fpre)

    ri = lax.broadcasted_iota(I32, (L, L), 0)
    ci = lax.broadcasted_iota(I32, (L, L), 1)
    causal = ci <= ri
    eye = ci == ri
    bcum_col = jnp.sum(jnp.where(causal, fc, 0.0), axis=1, keepdims=True)
    fc_col = jnp.sum(jnp.where(eye, fc, 0.0), axis=1, keepdims=True)
    bcum_row = jnp.sum(jnp.where(ri <= ci, fc_col, 0.0), axis=0, keepdims=True)

    m_st = m_scr[...]
    d = jnp.where(causal, bcum_col - bcum_row + ic, -jnp.inf)
    inter = bcum_col + m_st
    m_row = jnp.maximum(inter, jnp.max(d, axis=1, keepdims=True))
    w_inter = jnp.exp(inter - m_row)

    q = q_ref[...]
    k = k_ref[...] * (ML_DK ** -0.5)
    v = v_ref[...]
    qb = q.astype(BF16)
    vb = v.astype(BF16)
    p = jnp.exp(d - m_row) * _bdot_nt(qb, k)
    c_st = c_scr[...]
    n_st = n_scr[...]
    num = w_inter * _bdot(qb, c_st) + _bdot(p, vb)
    den = w_inter * jnp.sum(q * n_st, axis=1, keepdims=True) + jnp.sum(p, axis=1, keepdims=True)
    hout = num / jnp.maximum(jnp.abs(den), jnp.exp(-m_row))

    b_last = jnp.sum(fc, axis=1, keepdims=True)
    g_row = b_last - bcum_row + ic
    m_new = jnp.maximum(b_last + m_st, jnp.max(g_row, axis=1, keepdims=True))
    w_old = jnp.exp(b_last + m_st - m_new)
    w_in_col = jnp.sum(jnp.where(eye, jnp.exp(g_row - m_new), 0.0), axis=1, keepdims=True)
    kw = k * w_in_col
    c_scr[...] = w_old * c_st + _bdot_tn(kw, vb)
    n_scr[...] = w_old * n_st + jnp.sum(kw, axis=0, keepdims=True)
    m_scr[...] = m_new

    hn = hout * lax.rsqrt(jnp.mean(hout * hout, axis=-1, keepdims=True) + EPS) * gain_ref[...]
    y_ref[...] = (hn * _sigmoid(op_ref[...]) * _silu(z_ref[...])).astype(y_ref.dtype)


def _mlstm(proj, gates, b_i, b_f, gain, bsz, t):
    L = min(ML_CHUNK, t)
    nc = t // L
    row = lambda b, h, c: b * nc + c
    return pl.pallas_call(
        _mlstm_body,
        grid=(bsz, ML_HEADS, nc),
        in_specs=[
            pl.BlockSpec(memory_space=pltpu.SMEM),
            pl.BlockSpec(memory_space=pltpu.SMEM),
            pl.BlockSpec((L, ML_DK), lambda b, h, c: (row(b, h, c), h)),
            pl.BlockSpec((L, ML_DK), lambda b, h, c: (row(b, h, c), ML_HEADS + h)),
            pl.BlockSpec((L, ML_DV), lambda b, h, c: (row(b, h, c), ML_HEADS + h)),
            pl.BlockSpec((L, ML_DV), lambda b, h, c: (row(b, h, c), 2 * ML_HEADS + h)),
            pl.BlockSpec((L, ML_DV), lambda b, h, c: (row(b, h, c), 3 * ML_HEADS + h)),
            pl.BlockSpec((None, None, None, 2, L), lambda b, h, c: (b, h, c, 0, 0)),
            pl.BlockSpec((None, 1, ML_DV), lambda b, h, c: (h, 0, 0)),
        ],
        out_specs=pl.BlockSpec((L, ML_DV), lambda b, h, c: (row(b, h, c), h)),
        out_shape=jax.ShapeDtypeStruct((bsz * t, ML_HEADS * ML_DV), BF16),
        scratch_shapes=[pltpu.VMEM((ML_DK, ML_DV), F32), pltpu.VMEM((1, ML_DK), F32), pltpu.VMEM((1, 1), F32)],
        compiler_params=_cparams(("parallel", "parallel", "arbitrary"), 40),
        name="mlstm",
    )(b_i.astype(F32), b_f.astype(F32), proj, proj, proj, proj, proj, gates,
      gain.reshape(ML_HEADS, 1, ML_DV).astype(F32))


def _causal_conv(x, tail_ref, w):
    L = x.shape[0]
    xx = jnp.concatenate([tail_ref[...], x], axis=0)
    y = w[CONV_WIDTH - 1:CONV_WIDTH, :] * x
    for j in range(CONV_WIDTH - 1):
        y = y + w[j:j + 1, :] * pltpu.roll(xx, CONV_WIDTH - 1 - j, axis=0)[8:, :]
    tail_ref[...] = x[L - 8:, :]
    return y


def _head_expand(n_heads, width):
    r = lax.broadcasted_iota(I32, (n_heads, n_heads * width), 0)
    c = lax.broadcasted_iota(I32, (n_heads, n_heads * width), 1)
    return ((c >= r * width) & (c < (r + 1) * width)).astype(F32)


def _ssd_body(x_ref, bm_ref, cm_ref, z_ref, dtc_ref, dtr_ref, wx_ref, wb_ref, wc_ref, bx_ref, bb_ref, bc_ref,
              dtbc_ref, dtbr_ref, alc_ref, alr_ref, dsk_ref, gain_ref, y_ref, st_scr, tx_scr, tb_scr, tc_scr):
    L = x_ref.shape[0]
    nh, hd = SSM_HEADS_PER_GROUP, SSM_HEAD_DIM

    @pl.when(pl.program_id(2) == 0)
    def _():
        st_scr[...] = jnp.zeros_like(st_scr)
        tx_scr[...] = jnp.zeros_like(tx_scr)
        tb_scr[...] = jnp.zeros_like(tb_scr)
        tc_scr[...] = jnp.zeros_like(tc_scr)

    xc = _silu(_causal_conv(x_ref[...], tx_scr, wx_ref[...]) + bx_ref[...])
    bc = _silu(_causal_conv(bm_ref[...], tb_scr, wb_ref[...]) + bb_ref[...])
    cc = _silu(_causal_conv(cm_ref[...], tc_scr, wc_ref[...]) + bc_ref[...])

    ri = lax.broadcasted_iota(I32, (L, L), 0)
    ci = lax.broadcasted_iota(I32, (L, L), 1)
    causal = ci <= ri
    dt_col = _softplus(dtc_ref[...] + dtbc_ref[...])
    acum = _dot_sel_lhs(causal, dt_col * (-jnp.exp(alc_ref[...])))
    dt_row = _softplus(dtr_ref[...] + dtbr_ref[...])
    acum_row = _dot_sel_rhs(dt_row * (-jnp.exp(alr_ref[...])), ri <= ci)

    expand = _head_expand(nh, hd)
    acum_e = _dot_sel_rhs(acum, expand)
    xdt = xc * _dot_sel_rhs(dt_col, expand)
    xdt_b = xdt.astype(BF16)
    cb = _bdot_nt(cc, bc)

    lane = lax.broadcasted_iota(I32, (L, 2 * hd), 1)
    parts = []
    for hp in range(nh // 2):
        xpair = xdt_b[:, 2 * hd * hp:2 * hd * (hp + 1)]
        acc = None
        for e in range(2):
            hh = 2 * hp + e
            seg = acum[:, hh:hh + 1] - acum_row[hh:hh + 1, :]
            dec = jnp.exp(jnp.where(causal, seg, -jnp.inf))
            xsel = jnp.where((lane >= hd * e) & (lane < hd * (e + 1)), xpair, jnp.zeros_like(xpair))
            t = _bdot(cb * dec, xsel)
            acc = t if acc is None else acc + t
        parts.append(acc)
    y = jnp.concatenate(parts, axis=1)

    st = st_scr[...]
    y = y + _bdot(cc, st) * jnp.exp(acum_e)
    a_last = acum_e[L - 1:L, :]
    st_scr[...] = jnp.exp(a_last) * st + _bdot_tn(bc, xdt * jnp.exp(a_last - acum_e))

    y = y + xc * dsk_ref[...]
    y = y * _silu(z_ref[...])
    y = y * lax.rsqrt(jnp.mean(y * y, axis=-1, keepdims=True) + EPS) * gain_ref[...]
    y_ref[...] = y.astype(y_ref.dtype)


def _ssd(proj, dt_col, dt_row, conv_w, conv_b, dt_bias, a_log, d_skip, gain, bsz, t):
    L = min(SSD_CHUNK, t)
    nc = t // L
    G, nh, gw, ns = SSM_GROUPS, SSM_HEADS_PER_GROUP, SSM_GROUP_WIDTH, SSM_STATE
    row = lambda b, g, c: b * nc + c
    x_blk = 10240 // gw
    z_blk = 8192 // gw
    b_blk = 12288 // ns
    c_blk = 12800 // ns
    wb_blk = G * gw // ns
    conv_b2 = conv_b.reshape(1, -1).astype(F32)
    f = lambda a: a.astype(F32)
    return pl.pallas_call(
        _ssd_body,
        grid=(bsz, G, nc),
        in_specs=[
            pl.BlockSpec((L, gw), lambda b, g, c: (row(b, g, c), x_blk + g)),
            pl.BlockSpec((L, ns), lambda b, g, c: (row(b, g, c), b_blk + g)),
            pl.BlockSpec((L, ns), lambda b, g, c: (row(b, g, c), c_blk + g)),
            pl.BlockSpec((L, gw), lambda b, g, c: (row(b, g, c), z_blk + g)),
            pl.BlockSpec((None, None, L, nh), lambda b, g, c: (b, g, c, 0)),
            pl.BlockSpec((None, None, nh, L), lambda b, g, c: (b, g, 0, c)),
            pl.BlockSpec((CONV_WIDTH, gw), lambda b, g, c: (0, g)),
            pl.BlockSpec((CONV_WIDTH, ns), lambda b, g, c: (0, wb_blk + g)),
            pl.BlockSpec((CONV_WIDTH, ns), lambda b, g, c: (0, wb_blk + G + g)),
            pl.BlockSpec((1, gw), lambda b, g, c: (0, g)),
            pl.BlockSpec((1, ns), lambda b, g, c: (0, wb_blk + g)),
            pl.BlockSpec((1, ns), lambda b, g, c: (0, wb_blk + G + g)),
            pl.BlockSpec((None, 1, nh), lambda b, g, c: (g, 0, 0)),
            pl.BlockSpec((None, nh, 1), lambda b, g, c: (g, 0, 0)),
            pl.BlockSpec((None, 1, nh), lambda b, g, c: (g, 0, 0)),
            pl.BlockSpec((None, nh, 1), lambda b, g, c: (g, 0, 0)),
            pl.BlockSpec((None, 1, gw), lambda b, g, c: (g, 0, 0)),
            pl.BlockSpec((None, 1, gw), lambda b, g, c: (g, 0, 0)),
        ],
        out_specs=pl.BlockSpec((L, gw), lambda b, g, c: (row(b, g, c), g)),
        out_shape=jax.ShapeDtypeStruct((bsz * t, G * gw), BF16),
        scratch_shapes=[pltpu.VMEM((ns, gw), F32), pltpu.VMEM((8, gw), F32),
                        pltpu.VMEM((8, ns), F32), pltpu.VMEM((8, ns), F32)],
        compiler_params=_cparams(("parallel", "parallel", "arbitrary"), 40),
        name="ssd",
    )(proj, proj, proj, proj, dt_col, dt_row, f(conv_w), f(conv_w), f(conv_w), conv_b2, conv_b2, conv_b2,
      f(dt_bias).reshape(G, 1, nh), f(dt_bias).reshape(G, nh, 1), f(a_log).reshape(G, 1, nh),
      f(a_log).reshape(G, nh, 1), jnp.repeat(f(d_skip), SSM_HEAD_DIM).reshape(G, 1, gw), f(gain).reshape(G, 1, gw))


def _layer_ab(x2, bsz, t, norm_g, w_in, ml_b_i, ml_b_f, ml_norm, ssm_conv_w, ssm_conv_b, ssm_dt_bias,
              ssm_a_log, ssm_d, ssm_norm, w_out):
    m = bsz * t
    small = jnp.concatenate([w_in[:, 8192:8200], w_in[:, 13320:13352]], axis=1)
    w_main = jnp.concatenate(
        [w_in[:, :8192], w_in[:, 8200:13320], jnp.pad(small, ((0, 0), (0, 512 - small.shape[1])))], axis=1
    ).astype(BF16)
    proj = _norm_matmul(x2, 0, D_MODEL, norm_g, w_main, out_dtype=F32, tn=1536, name="proj_ab")
    gate_cols = proj[:, 13312:13352]
    ml_gates = gate_cols[:, :8].reshape(bsz, t // min(ML_CHUNK, t), min(ML_CHUNK, t), 2, ML_HEADS)
    ml_gates = jnp.transpose(ml_gates, (0, 4, 1, 3, 2))
    dt_raw = gate_cols[:, 8:40].reshape(bsz, t, SSM_GROUPS, SSM_HEADS_PER_GROUP)
    dt_col = jnp.transpose(dt_raw, (0, 2, 1, 3))
    dt_row = jnp.transpose(dt_raw, (0, 2, 3, 1))
    y_a = _mlstm(proj, ml_gates, ml_b_i, ml_b_f, ml_norm, bsz, t)
    y_b = _ssd(proj, dt_col, dt_row, ssm_conv_w, ssm_conv_b, ssm_dt_bias, ssm_a_log, ssm_d, ssm_norm, bsz, t)
    return _out_proj([y_a, y_b], w_out.astype(BF16), x2, jnp.ones((D_MODEL,), F32), final_norm=False,
                     name="out_ab")


def _indexer_body(iq_ref, iw_ref, ik_ref, mask_ref, hi_scr, lo_scr, pos_scr, *, top_k):
    n_kt = hi_scr.shape[0]
    KT, Q = IDX_KT, IDX_Q
    i = pl.program_id(1)
    start = i * Q
    n_tiles = (start + Q + KT - 1) // KT
    iqb = iq_ref[...].astype(BF16)
    iw = iw_ref[...] * (IDX_HEADS ** -0.5 * IDX_DIM ** -0.5)
    qpos = start + lax.broadcasted_iota(I32, (1, Q), 1)
    limit = (lax.shift_right_logical(qpos, int(math.log2(DSA_CHUNK))) + 1) * DSA_CHUNK
    kiota = lax.broadcasted_iota(I32, (KT, 1), 0)

    def score_tile(tt, carry):
        acc = jnp.zeros((KT, Q), F32)
        for p in range(IDX_HEADS // 2):
            xq = iqb[:, 2 * IDX_DIM * p:2 * IDX_DIM * (p + 1)]
            s0 = _bdot_nt(ik_ref[tt, 0], xq)
            s1 = _bdot_nt(ik_ref[tt, 1], xq)
            acc = acc + iw[2 * p:2 * p + 1, :] * jnp.maximum(s0, 0.0)
            acc = acc + iw[2 * p + 1:2 * p + 2, :] * jnp.maximum(s1, 0.0)
        bits = lax.bitcast_convert_type(acc + 0.0, I32)
        key = jnp.where(bits < 0, bits ^ jnp.int32(0x7FFFFFFF), bits)
        key = jnp.where(tt * KT + kiota < limit, key, jnp.int32(INT_MIN))
        hi_scr[tt] = lax.shift_right_arithmetic(key, 16).astype(I16)
        lo_scr[tt] = ((key & jnp.int32(0xFFFF)) - jnp.int32(HALF16)).astype(I16)
        return carry

    lax.fori_loop(0, n_tiles, score_tile, 0)

    def count16(pred):
        def body(tt, accs):
            accs = list(accs)
            w = jnp.where(pred(tt), jnp.int16(1), jnp.int16(0))
            for r in range(KT // 16):
                accs[r % 4] = accs[r % 4] + w[16 * r:16 * (r + 1), :]
            return tuple(accs)
        zero = jnp.zeros((16, Q), I16)
        a0, a1, a2, a3 = lax.fori_loop(0, n_tiles, body, (zero, zero, zero, zero))
        return jnp.sum(((a0 + a1) + (a2 + a3)).astype(F32), axis=0, keepdims=True)

    def rows16(v):
        return jnp.concatenate([jnp.broadcast_to(v, (16, Q)).astype(I16)] * (KT // 16), axis=0)

    def bisect16(ref, n_above):
        t16 = jnp.full((1, Q), -HALF16, I32)
        for bit in range(15, -1, -1):
            cand = jnp.zeros_like(t16) if bit == 15 else t16 + jnp.int32(1 << bit)
            ct = rows16(cand)
            t16 = jnp.where(n_above + count16(lambda tt, ct=ct: ref[tt] >= ct) >= top_k, cand, t16)
        return t16

    thr_hi = bisect16(hi_scr, 0.0)
    th = rows16(thr_hi)
    n_hi_gt = count16(lambda tt: hi_scr[tt] > th)

    def keep_lo(tt, carry):
        lo_scr[tt] = jnp.where(hi_scr[tt] == th, lo_scr[tt], jnp.int16(-HALF16))
        return carry

    lax.fori_loop(0, n_tiles, keep_lo, 0)
    thr_lo = bisect16(lo_scr, n_hi_gt)
    tl = rows16(thr_lo)

    def pos16(tt):
        return jnp.broadcast_to(tt * KT + kiota, (KT, Q)).astype(I16)

    n_gt = n_hi_gt + count16(lambda tt: lo_scr[tt] > tl)
    n_ge = n_hi_gt + count16(lambda tt: lo_scr[tt] >= tl)
    room = top_k - n_gt
    has_thr = (thr_hi > -HALF16) | (thr_lo > -HALF16)
    pos_scr[...] = jnp.full((1, Q), 2 * n_kt * KT, I32)

    @pl.when(jnp.max(jnp.where(has_thr, n_ge, 0.0)) > top_k)
    def _():
        pcut = jnp.zeros((1, Q), I32)
        for bit in range(int(math.log2(n_kt * KT)), -1, -1):
            cand = pcut + jnp.int32(1 << bit)
            c16 = rows16(cand)
            n_tie = count16(lambda tt, c16=c16: (hi_scr[tt] == th) & (lo_scr[tt] == tl) & (pos16(tt) < c16))
            pcut = jnp.where(n_tie <= room, cand, pcut)
        pos_scr[...] = pcut

    p16 = rows16(pos_scr[...])
    lim16 = rows16(limit)

    def write_tile(tt, carry):
        hi, lo, pos = hi_scr[tt], lo_scr[tt], pos16(tt)
        sel = (hi > th) | ((hi == th) & ((lo > tl) | ((lo == tl) & (pos < p16))))
        sel = sel & (pos < lim16)
        mask_ref[tt] = jnp.transpose(jnp.where(sel, jnp.int16(1), jnp.int16(0)).astype(F32)).astype(jnp.int8)
        return carry

    lax.fori_loop(0, n_tiles, write_tile, 0)

    def zero_tile(tt, carry):
        mask_ref[tt] = jnp.zeros((Q, KT), jnp.int8)
        return carry

    lax.fori_loop(n_tiles, n_kt, zero_tile, 0)


def _indexer(proj, iq_blk, iw_t, ik_pad, bsz, t):
    nq = t // IDX_Q
    n_kt = t // IDX_KT
    top_k = min(TOPK_MAX, t // 4)
    return pl.pallas_call(
        functools.partial(_indexer_body, top_k=top_k),
        grid=(bsz, nq),
        in_specs=[
            pl.BlockSpec((IDX_Q, IDX_HEADS * IDX_DIM), lambda b, i: (b * nq + i, iq_blk)),
            pl.BlockSpec((None, IDX_HEADS, IDX_Q), lambda b, i: (b * nq + i, 0, 0)),
            pl.BlockSpec((None, n_kt, 2, IDX_KT, 2 * IDX_DIM), lambda b, i: (b, 0, 0, 0, 0)),
        ],
        out_specs=pl.BlockSpec((None, n_kt, IDX_Q, IDX_KT), lambda b, i: (b * nq + i, 0, 0, 0)),
        out_shape=jax.ShapeDtypeStruct((bsz * nq, n_kt, IDX_Q, IDX_KT), jnp.int8),
        scratch_shapes=[pltpu.VMEM((n_kt, IDX_KT, IDX_Q), I16), pltpu.VMEM((n_kt, IDX_KT, IDX_Q), I16),
                        pltpu.VMEM((1, IDX_Q), I32)],
        compiler_params=_cparams(("parallel", "arbitrary"), 48),
        name="dsa_indexer",
    )(proj, iw_t, ik_pad)


NEG_BIG = -3.0e38
LOG2_E = math.log2(math.e)


def _attn_body(q_ref, k_ref, v_ref, mask_ref, z_ref, tb_ref, o_ref, acc_scr, m_scr):
    T = ATT_T
    D = ATT_HEAD_DIM
    nsub = T // ATT_SUB
    i = pl.program_id(1)
    j = pl.program_id(2)

    @pl.when(j == 0)
    def _():
        acc_scr[...] = jnp.zeros_like(acc_scr)
        m_scr[...] = jnp.full_like(m_scr, NEG_BIG)

    def tile_step(near):
        sel = jnp.concatenate([mask_ref[a, 0].astype(F32) for a in range(mask_ref.shape[0])], axis=0)
        mask_bias = (sel - 1.0) * (-NEG_BIG)
        ones = jnp.ones((T, D), BF16)

        def pair_body(p, carry):
            q2, k2, v2 = q_ref[p], k_ref[p], v_ref[p]
            m_old = [m_scr[2 * p + e] for e in range(2)]
            a_old = [acc_scr[2 * p + e] for e in range(2)]
            new = []
            logits = [_bdot_nt(q2[:, D * e:D * (e + 1)], k2[:, D * e:D * (e + 1)]) for e in range(2)]
            for e in range(2):
                s = logits[e] + mask_bias
                if near is not None:
                    tb = tb_ref[2 * p + e]
                    zero = jnp.zeros((ATT_SUB, ATT_SUB), F32)
                    rows = []
                    for a in range(nsub):
                        cols = []
                        for bb in range(nsub):
                            rel_blocks = (bb - a) if near == "diag" else (bb - a - nsub)
                            cols.append(tb[0] if rel_blocks == 0 else tb[1] if rel_blocks == -1 else zero)
                        rows.append(jnp.concatenate(cols, axis=1))
                    s = s + jnp.concatenate(rows, axis=0)
                m_new = jnp.maximum(m_old[e], jnp.max(s, axis=1, keepdims=True))
                alpha = jnp.exp2(m_old[e] - m_new)
                pr = jnp.exp2(s - jnp.concatenate([m_new] * (T // LANES), axis=1))
                v_aug = jnp.concatenate([v2[:, D * e:D * (e + 1)], ones], axis=1)
                a_new = jnp.concatenate([alpha] * (2 * D // LANES), axis=1) * a_old[e] + _bdot(pr, v_aug)
                new.append((m_new, a_new))
            for e in range(2):
                m_scr[2 * p + e] = new[e][0]
                acc_scr[2 * p + e] = new[e][1]
            return carry

        lax.fori_loop(0, ATT_HEADS // 2, pair_body, 0, unroll=8)

    @pl.when(j < i - 1)
    def _():
        tile_step(None)

    @pl.when(j == i - 1)
    def _():
        tile_step("prev")

    @pl.when(j == i)
    def _():
        tile_step("diag")
        for p in range(ATT_HEADS // 2):
            for e in range(2):
                hh = 2 * p + e
                a = acc_scr[hh]
                o = a[:, :D] / a[:, D:]
                o = o * _silu(z_ref[p, :, D * e:D * (e + 1)].astype(F32))
                o_ref[:, D * hh:D * (hh + 1)] = o.astype(o_ref.dtype)


def _attention(qz, kv, mask, tb, bsz, t):
    T = ATT_T
    nq = t // T
    npair = ATT_HEADS // 2
    pw = 2 * ATT_HEAD_DIM
    nrow = T // IDX_Q
    ncol = T // IDX_KT
    return pl.pallas_call(
        _attn_body,
        grid=(bsz, nq, nq),
        in_specs=[
            pl.BlockSpec((npair, T, pw), lambda b, i, j: (0, b * nq + i, 0)),
            pl.BlockSpec((npair, T, pw), lambda b, i, j: (0, b * nq + jnp.minimum(j, i), 0)),
            pl.BlockSpec((npair, T, pw), lambda b, i, j: (1, b * nq + jnp.minimum(j, i), 0)),
            pl.BlockSpec((nrow, ncol, IDX_Q, IDX_KT), lambda b, i, j: (b * nq + i, jnp.minimum(j, i), 0, 0)),
            pl.BlockSpec((npair, T, pw), lambda b, i, j: (1, b * nq + i, 0)),
            pl.BlockSpec((ATT_HEADS, 2, ATT_SUB, ATT_SUB), lambda b, i, j: (0, 0, 0, 0)),
        ],
        out_specs=pl.BlockSpec((T, ATT_HEADS * ATT_HEAD_DIM), lambda b, i, j: (b * nq + i, 0)),
        out_shape=jax.ShapeDtypeStruct((bsz * t, ATT_HEADS * ATT_HEAD_DIM), BF16),
        scratch_shapes=[pltpu.VMEM((ATT_HEADS, T, 2 * ATT_HEAD_DIM), F32), pltpu.VMEM((ATT_HEADS, T, LANES), F32)],
        compiler_params=_cparams(("parallel", "parallel", "arbitrary"), 56),
        name="dsa_attention",
    )(qz, kv, kv, mask, qz, tb)


def _t5_bucket(rel):
    nb = REL_BUCKETS // 2
    max_exact = nb // 2
    ret = jnp.where(rel > 0, nb, 0)
    n = jnp.abs(rel)
    nf = jnp.maximum(n, max_exact).astype(F32)
    large = max_exact + (jnp.log(nf / max_exact) / math.log(REL_MAX_DIST / max_exact)
                         * (nb - max_exact)).astype(jnp.int32)
    large = jnp.minimum(large, nb - 1)
    return ret + jnp.where(n < max_exact, n, large)


def _rel_bias_blocks(rel_bias):
    r = jnp.arange(ATT_SUB)[:, None]
    c = jnp.arange(ATT_SUB)[None, :]
    rel = jnp.stack([c - r, c - r - ATT_SUB])
    assert ATT_SUB >= REL_MAX_DIST and ATT_SUB % DSA_CHUNK == 0
    rb = rel_bias.astype(F32)
    far = rb[_t5_bucket(jnp.int32(-2 * ATT_SUB))]
    onehot = (_t5_bucket(rel)[..., None] == jnp.arange(REL_BUCKETS)).astype(F32)
    table = jnp.einsum("vrcb,bh->hvrc", onehot, rb - far, precision=HI)
    return table * LOG2_E


GDN_PREP_ROWS = 256
GDN_SOLVE_LANES = 512
GDN_SCAN_CHUNKS = 2


def _conv_with_prev(x, prev, w):
    xx = jnp.concatenate([prev, x], axis=0)
    y = w[CONV_WIDTH - 1:CONV_WIDTH, :] * x
    for j in range(CONV_WIDTH - 1):
        y = y + w[j:j + 1, :] * pltpu.roll(xx, CONV_WIDTH - 1 - j, axis=0)[8:, :]
    return y


def _gdn_prep_body(v_ref, q_ref, k_ref, pv_ref, pq_ref, pk_ref, bc_ref, ac_ref, ar_ref, wq_ref, wk_ref, wv_ref,
                   dtbc_ref, dtbr_ref, alc_ref, alr_ref,
                   qn_ref, rhs_ref, lu_ref, a_ref, egc_ref, egl_ref):
    L = GDN_CHUNK
    D = GDN_HEAD_DIM
    nv = GDN_V_HEADS // GDN_GROUPS
    rep = GDN_V_HEADS // GDN_QK_HEADS
    seq_start = pl.program_id(2) == 0

    def conv(x_ref, p_ref, w_ref):
        prev = jnp.where(seq_start, 0.0, p_ref[...])
        return _silu(_conv_with_prev(x_ref[...], prev, w_ref[...]))

    qc = conv(q_ref, pq_ref, wq_ref)
    kc = conv(k_ref, pk_ref, wk_ref)
    vc = conv(v_ref, pv_ref, wv_ref)
    qn, kn = [], []
    for kh in range(nv // rep):
        qh = qc[:, D * kh:D * (kh + 1)]
        kk_ = kc[:, D * kh:D * (kh + 1)]
        qh = qh * lax.rsqrt(jnp.sum(qh * qh, axis=-1, keepdims=True) + EPS) * (D ** -0.5)
        kn.append(kk_ * lax.rsqrt(jnp.sum(kk_ * kk_, axis=-1, keepdims=True) + EPS))
        qn.append(qh.astype(BF16))
        qn_ref[:, D * kh:D * (kh + 1)] = qn[kh]

    beta = _sigmoid(bc_ref[...])
    g_col = -jnp.exp(alc_ref[...]) * _softplus(ac_ref[...] + dtbc_ref[...])
    ri = lax.broadcasted_iota(I32, (L, L), 0)
    ci = lax.broadcasted_iota(I32, (L, L), 1)
    causal = ci <= ri
    strict = ci < ri
    for cc in range(a_ref.shape[0]):
        rows = slice(L * cc, L * (cc + 1))
        gc_col = _dot_sel_lhs(causal, g_col[rows, :])
        g_row = -jnp.exp(alr_ref[...]) * _softplus(ar_ref[cc] + dtbr_ref[...])
        gc_row = _dot_sel_rhs(g_row, ri <= ci)
        eg_col = jnp.exp(gc_col)
        egc_ref[rows, :] = eg_col
        for kh in range(nv // rep):
            kf = kn[kh][rows, :]
            kb = kf.astype(BF16)
            kk = _bdot_nt(kb, kb)
            qk = _bdot_nt(qn[kh][rows, :], kb)
            for e in range(rep):
                vh = rep * kh + e
                gcol = gc_col[:, vh:vh + 1]
                bcol = beta[rows, vh:vh + 1]
                dec = jnp.exp(jnp.where(causal, gcol - gc_row[vh:vh + 1, :], -jnp.inf))
                a_ref[cc, vh] = jnp.where(strict, bcol * kk * dec, 0.0).astype(a_ref.dtype)
                g_last = gcol[L - 1:L, :]
                kw_t = jnp.transpose(kf * jnp.exp(g_last - gcol))
                lu_ref[cc, vh] = jnp.concatenate([qk * dec, kw_t], axis=0).astype(lu_ref.dtype)
                egl_ref[cc, vh:vh + 1, :] = jnp.broadcast_to(jnp.exp(g_last), (1, LANES))
                rhs_ref[rows, 2 * D * vh:2 * D * (vh + 1)] = jnp.concatenate(
                    [vc[rows, D * vh:D * (vh + 1)] * bcol, kf * (bcol * eg_col[:, vh:vh + 1])], axis=1
                ).astype(rhs_ref.dtype)


def _gdn_solve_body(a_in_ref, t_ref, t_scr, a_ref):
    L, _, PB = a_in_ref.shape
    UN = 8
    sub = lax.broadcasted_iota(I32, (UN, PB), 0)
    t_scr[...] = jnp.zeros_like(t_scr)
    a_ref[...] = a_in_ref[...].astype(F32)

    for ib in range(L // UN):
        def row_body(r, carry, ib=ib):
            i = ib * UN + r
            acc = [jnp.zeros((UN, PB), F32) for _ in range(ib + 1)]
            for jb in range(ib + 1):
                a_blk = a_ref[i, UN * jb:UN * (jb + 1), :]
                for jj in range(UN):
                    a_ij = a_blk[jj:jj + 1, :]
                    for cg in range(jb + 1):
                        acc[cg] = acc[cg] + a_ij * t_scr[UN * jb + jj, UN * cg:UN * (cg + 1), :]
            for cg in range(ib + 1):
                unit = jnp.where(sub == r, 1.0, 0.0) if cg == ib else 0.0
                t_scr[i, UN * cg:UN * (cg + 1), :] = unit - acc[cg]
            return carry

        lax.fori_loop(0, UN, row_body, 0)
    t_ref[...] = t_scr[...].astype(t_ref.dtype)


def _gdn_scan_body(qn_ref, rhs_ref, lu_ref, t_ref, z_ref, egc_ref, egl_ref, gain_ref, y_ref, s_scr):
    L = GDN_CHUNK
    D = GDN_HEAD_DIM
    nv = GDN_V_HEADS // GDN_GROUPS
    rep = GDN_V_HEADS // GDN_QK_HEADS
    ncs = t_ref.shape[0]

    @pl.when(pl.program_id(2) == 0)
    def _():
        s_scr[...] = jnp.zeros_like(s_scr)

    heads = range(nv)
    sol = [[_bdot(t_ref[cc, vh], rhs_ref[L * cc:L * (cc + 1), 2 * D * vh:2 * D * (vh + 1)]) for vh in heads]
           for cc in range(ncs)]
    s_st = [s_scr[vh] for vh in heads]
    for cc in range(ncs):
        rows = slice(L * cc, L * (cc + 1))
        eg_col = egc_ref[rows, :]
        r_s = [_bdot(jnp.concatenate([sol[cc][vh][:, D:].astype(BF16),
                                      qn_ref[rows, D * (vh // rep):D * (vh // rep + 1)]], axis=0), s_st[vh])
               for vh in heads]
        u = [sol[cc][vh][:, :D] - r_s[vh][:L] for vh in heads]
        r_u = [_bdot(lu_ref[cc, vh], u[vh]) for vh in heads]
        s_st = [egl_ref[cc, vh:vh + 1, :] * s_st[vh] + r_u[vh][L:] for vh in heads]
        for vh in heads:
            o = r_s[vh][L:] * eg_col[:, vh:vh + 1] + r_u[vh][:L]
            o = o * lax.rsqrt(jnp.mean(o * o, axis=-1, keepdims=True) + EPS) * gain_ref[...]
            y_ref[rows, D * vh:D * (vh + 1)] = (o * _silu(z_ref[rows, D * vh:D * (vh + 1)])).astype(y_ref.dtype)
    for vh in heads:
        s_scr[vh] = s_st[vh]


def _gdn(proj, b_col, a_col, a_row, conv_w, dt_bias, a_log, gain, bsz, t):
    L = GDN_CHUNK
    nc = t // L
    G = GDN_GROUPS
    nv = GDN_V_HEADS // G
    qw = GDN_QK_HEADS // G * GDN_HEAD_DIM
    vw = nv * GDN_HEAD_DIM
    m = bsz * t
    z_blk, q_blk, k_blk = 4096 // vw, 8192 // qw, 10240 // qw
    f = lambda a: a.astype(F32)
    cw = f(conv_w)
    pc_spec = pl.BlockSpec((None, 1, nv), lambda b, g, c: (g, 0, 0))
    pr_spec = pl.BlockSpec((None, nv, 1), lambda b, g, c: (g, 0, 0))
    wq_spec = pl.BlockSpec((CONV_WIDTH, qw), lambda b, g, c: (0, g))
    wk_spec = pl.BlockSpec((CONV_WIDTH, qw), lambda b, g, c: (0, G + g))
    wv_spec = pl.BlockSpec((CONV_WIDTH, vw), lambda b, g, c: (0, (2 * G * qw) // vw + g))

    LB = min(GDN_PREP_ROWS, t)
    ncb = LB // L
    nb = t // LB
    rowb = lambda b, g, c: b * nb + c
    prev8 = lambda b, g, c: jnp.maximum((b * t + c * LB) // 8 - 1, 0)
    D = GDN_HEAD_DIM
    qn, rhs, lu, a_mat, egc, egl = pl.pallas_call(
        _gdn_prep_body,
        grid=(bsz, G, nb),
        in_specs=[
            pl.BlockSpec((LB, vw), lambda b, g, c: (rowb(b, g, c), g)),
            pl.BlockSpec((LB, qw), lambda b, g, c: (rowb(b, g, c), q_blk + g)),
            pl.BlockSpec((LB, qw), lambda b, g, c: (rowb(b, g, c), k_blk + g)),
            pl.BlockSpec((8, vw), lambda b, g, c: (prev8(b, g, c), g)),
            pl.BlockSpec((8, qw), lambda b, g, c: (prev8(b, g, c), q_blk + g)),
            pl.BlockSpec((8, qw), lambda b, g, c: (prev8(b, g, c), k_blk + g)),
            pl.BlockSpec((None, None, LB, nv), lambda b, g, c: (b, g, c, 0)),
            pl.BlockSpec((None, None, LB, nv), lambda b, g, c: (b, g, c, 0)),
            pl.BlockSpec((None, None, ncb, nv, L), lambda b, g, c: (b, g, c, 0, 0)),
            wq_spec, wk_spec, wv_spec, pc_spec, pr_spec, pc_spec, pr_spec,
        ],
        out_specs=[
            pl.BlockSpec((LB, qw), lambda b, g, c: (rowb(b, g, c), g)),
            pl.BlockSpec((LB, 2 * vw), lambda b, g, c: (rowb(b, g, c), g)),
            pl.BlockSpec((None, None, ncb, nv, L + D, L), lambda b, g, c: (b, g, c, 0, 0, 0)),
            pl.BlockSpec((None, None, ncb, nv, L, L), lambda b, g, c: (b, g, c, 0, 0, 0)),
            pl.BlockSpec((None, None, LB, nv), lambda b, g, c: (b, g, c, 0)),
            pl.BlockSpec((None, None, ncb, nv, LANES), lambda b, g, c: (b, g, c, 0, 0)),
        ],
        out_shape=[
            jax.ShapeDtypeStruct((m, G * qw), BF16),
            jax.ShapeDtypeStruct((m, 2 * G * vw), BF16),
            jax.ShapeDtypeStruct((bsz, G, nc, nv, L + D, L), BF16),
            jax.ShapeDtypeStruct((bsz, G, nc, nv, L, L), BF16),
            jax.ShapeDtypeStruct((bsz, G, t, nv), F32),
            jax.ShapeDtypeStruct((bsz, G, nc, nv, LANES), F32),
        ],
        compiler_params=_cparams(("parallel", "parallel", "parallel"), 48),
        name="gdn_prep",
    )(proj, proj, proj, proj, proj, proj, b_col, a_col, a_row, cw, cw, cw,
      f(dt_bias).reshape(G, 1, nv), f(dt_bias).reshape(G, nv, 1), f(a_log).reshape(G, 1, nv),
      f(a_log).reshape(G, nv, 1))

    n_prob = bsz * G * nc * nv
    pb = min(GDN_SOLVE_LANES, n_prob)
    a_t = jnp.transpose(a_mat.reshape(n_prob, L, L), (1, 2, 0))
    t_t = pl.pallas_call(
        _gdn_solve_body,
        grid=(n_prob // pb,),
        in_specs=[pl.BlockSpec((L, L, pb), lambda p: (0, 0, p))],
        out_specs=pl.BlockSpec((L, L, pb), lambda p: (0, 0, p)),
        out_shape=jax.ShapeDtypeStruct((L, L, n_prob), BF16),
        scratch_shapes=[pltpu.VMEM((L, L, pb), F32), pltpu.VMEM((L, L, pb), F32)],
        compiler_params=_cparams(("parallel",), 40),
        name="gdn_solve",
    )(a_t)
    t_mat = jnp.transpose(t_t, (2, 0, 1)).reshape(bsz, G, nc, nv, L, L)

    ncs = min(GDN_SCAN_CHUNKS, nc)
    ns = nc // ncs
    row = lambda b, g, c: b * ns + c
    return pl.pallas_call(
        _gdn_scan_body,
        grid=(bsz, G, ns),
        in_specs=[
            pl.BlockSpec((ncs * L, qw), lambda b, g, c: (row(b, g, c), g)),
            pl.BlockSpec((ncs * L, 2 * vw), lambda b, g, c: (row(b, g, c), g)),
            pl.BlockSpec((None, None, ncs, nv, L + D, L), lambda b, g, c: (b, g, c, 0, 0, 0)),
            pl.BlockSpec((None, None, ncs, nv, L, L), lambda b, g, c: (b, g, c, 0, 0, 0)),
            pl.BlockSpec((ncs * L, vw), lambda b, g, c: (row(b, g, c), z_blk + g)),
            pl.BlockSpec((None, None, ncs * L, nv), lambda b, g, c: (b, g, c, 0)),
            pl.BlockSpec((None, None, ncs, nv, LANES), lambda b, g, c: (b, g, c, 0, 0)),
            pl.BlockSpec((1, GDN_HEAD_DIM), lambda b, g, c: (0, 0)),
        ],
        out_specs=pl.BlockSpec((ncs * L, vw), lambda b, g, c: (row(b, g, c), g)),
        out_shape=jax.ShapeDtypeStruct((m, GDN_V_HEADS * GDN_HEAD_DIM), BF16),
        scratch_shapes=[pltpu.VMEM((nv, GDN_HEAD_DIM, GDN_HEAD_DIM), F32)],
        compiler_params=_cparams(("parallel", "parallel", "arbitrary"), 40),
        name="gdn_scan",
    )(qn, rhs, lu, t_mat, proj, egc, egl, f(gain).reshape(1, GDN_HEAD_DIM))


def _layer_cd(x2, bsz, t, norm_g, w_in, kv_norm, w_uk, w_uv, gdn_conv_w, gdn_dt_bias, gdn_a_log, gdn_norm,
              w_out, rel_bias, final_g, final_norm):
    m = bsz * t
    small = jnp.concatenate([w_in[:, 5632:5712], w_in[:, 18000:18064]], axis=1)
    w_main = jnp.concatenate(
        [w_in[:, 9808:13904], w_in[:, 13904:18000], w_in[:, 5712:9808], w_in[:, 4608:5632], w_in[:, 2048:2560],
         jnp.pad(small, ((0, 0), (0, 512 - small.shape[1])))], axis=1).astype(BF16)
    proj = _norm_matmul(x2, 0, D_MODEL, norm_g, w_main, out_dtype=F32, tn=1024, name="proj_cd")
    w_attn = jnp.concatenate([w_in[:, :2048] * (ATT_HEAD_DIM ** -0.5 * LOG2_E), w_in[:, 2560:4608]],
                             axis=1).astype(BF16)
    qz = _norm_matmul(x2, 0, D_MODEL, norm_g, w_attn, out_dtype=BF16, blocked_width=2 * ATT_HEAD_DIM,
                      tn=1024, name="proj_cd_attn")
    w_kv = jnp.concatenate([w_uk.reshape(KV_RANK, -1), w_uv.reshape(KV_RANK, -1)], axis=1).astype(BF16)
    kv = _norm_matmul(proj, 13312 // KV_RANK, KV_RANK, kv_norm, w_kv, out_dtype=BF16,
                      blocked_width=2 * ATT_HEAD_DIM, tn=1024, name="dsa_kv")

    gate_cols = proj[:, 13824:13824 + 144]
    n_kt = t // IDX_KT
    ik = gate_cols[:, :IDX_DIM].reshape(bsz, n_kt, 1, IDX_KT, IDX_DIM).astype(BF16)
    zk = jnp.zeros_like(ik)
    ik_pad = jnp.concatenate([jnp.concatenate([ik, zk], axis=-1), jnp.concatenate([zk, ik], axis=-1)], axis=2)
    iw_t = jnp.transpose(gate_cols[:, IDX_DIM:IDX_DIM + IDX_HEADS].reshape(m // IDX_Q, IDX_Q, IDX_HEADS), (0, 2, 1))
    mask = _indexer(proj, 12288 // (IDX_HEADS * IDX_DIM), iw_t, ik_pad, bsz, t)
    y_c = _attention(qz, kv, mask, _rel_bias_blocks(rel_bias), bsz, t)

    nv = GDN_V_HEADS // GDN_GROUPS
    b_pre = gate_cols[:, 80:112].reshape(bsz, t, GDN_GROUPS, nv)
    a_pre = gate_cols[:, 112:144].reshape(bsz, t, GDN_GROUPS, nv)
    col = lambda a: jnp.transpose(a, (0, 2, 1, 3))
    rowf = lambda a: jnp.transpose(a.reshape(bsz, t // GDN_CHUNK, GDN_CHUNK, GDN_GROUPS, nv), (0, 3, 1, 4, 2))
    y_d = _gdn(proj, col(b_pre), col(a_pre), rowf(a_pre), gdn_conv_w, gdn_dt_bias, gdn_a_log, gdn_norm, bsz, t)
    return _out_proj([y_c, y_d], w_out.astype(BF16), x2, final_g, final_norm=final_norm, name="out_cd")


def kernel(x, rel_bias, ab_norm, ab_w_in, ab_ml_b_i, ab_ml_b_f, ab_ml_norm, ab_ssm_conv_w, ab_ssm_conv_b, ab_ssm_dt_bias, ab_ssm_a_log, ab_ssm_d, ab_ssm_norm, ab_w_out, cd_norm, cd_w_in, cd_kv_norm, cd_w_uk, cd_w_uv, cd_gdn_conv_w, cd_gdn_dt_bias, cd_gdn_a_log, cd_gdn_norm, cd_w_out, final_norm):
    bsz, t, d = x.shape
    x2 = x.reshape(bsz * t, d)
    x2 = _layer_ab(x2, bsz, t, ab_norm[0], ab_w_in[0], ab_ml_b_i[0], ab_ml_b_f[0], ab_ml_norm[0],
                   ab_ssm_conv_w[0], ab_ssm_conv_b[0], ab_ssm_dt_bias[0], ab_ssm_a_log[0], ab_ssm_d[0],
                   ab_ssm_norm[0], ab_w_out[0])
    x2 = _layer_cd(x2, bsz, t, cd_norm[0], cd_w_in[0], cd_kv_norm[0], cd_w_uk[0], cd_w_uv[0], cd_gdn_conv_w[0],
                   cd_gdn_dt_bias[0], cd_gdn_a_log[0], cd_gdn_norm[0], cd_w_out[0], rel_bias, final_norm, True)
    return x2.reshape(bsz, t, d)
```

```python
import functools
import math

import jax
import jax.numpy as jnp
from jax import lax
from jax.experimental import pallas as pl
from jax.experimental.pallas import tpu as pltpu

F32 = jnp.float32
BF16 = jnp.bfloat16
I32 = jnp.int32
I16 = jnp.int16
HI = lax.Precision.HIGHEST

EPS = 1e-6
CONV_WIDTH = 4
D_MODEL = 2048

ML_HEADS = 4
ML_DV = 512
ML_DK = 256

SSM_GROUPS = 4
SSM_HEADS_PER_GROUP = 8
SSM_HEAD_DIM = 64
SSM_STATE = 128
SSM_GROUP_WIDTH = SSM_HEADS_PER_GROUP * SSM_HEAD_DIM

ATT_HEADS = 16
ATT_HEAD_DIM = 128
KV_RANK = 512
IDX_HEADS = 16
IDX_DIM = 64
TOPK_MAX = 256
DSA_CHUNK = 64
REL_BUCKETS = 32
REL_MAX_DIST = 128

GDN_QK_HEADS = 16
GDN_V_HEADS = 32
GDN_HEAD_DIM = 128
GDN_GROUPS = 2
GDN_CHUNK = 64

LANES = 128

ML_CHUNK = 1024
SSD_CHUNK = 256
IDX_Q = 512
IDX_KT = 512
ATT_T = 512
ATT_SUB = 128
INT_MIN = -(2 ** 31)
HALF16 = 2 ** 15


def _cparams(sem, vmem_mib):
    return pltpu.CompilerParams(dimension_semantics=sem, vmem_limit_bytes=vmem_mib * 1024 * 1024)


def _bdot(a, b):
    return jnp.dot(a.astype(BF16), b.astype(BF16), preferred_element_type=F32)


def _bdot_nt(a, b):
    return lax.dot_general(a.astype(BF16), b.astype(BF16), (((1,), (1,)), ((), ())), preferred_element_type=F32)


def _bdot_tn(a, b):
    return lax.dot_general(a.astype(BF16), b.astype(BF16), (((0,), (0,)), ((), ())), preferred_element_type=F32)


def _split3(x):
    hi = x.astype(BF16)
    r = x - hi.astype(F32)
    mid = r.astype(BF16)
    return hi, mid, (r - mid.astype(F32)).astype(BF16)


def _as_bf16_01(sel):
    return jnp.where(sel, 1.0, 0.0).astype(BF16) if sel.dtype == jnp.bool_ else sel.astype(BF16)


def _dot_sel_lhs(sel, x):
    s = _as_bf16_01(sel)
    return sum(jnp.dot(s, p, preferred_element_type=F32) for p in _split3(x))


def _dot_sel_rhs(x, sel):
    s = _as_bf16_01(sel)
    return sum(jnp.dot(p, s, preferred_element_type=F32) for p in _split3(x))


def _softplus(x):
    return jnp.maximum(x, 0.0) + jnp.log1p(jnp.exp(-jnp.abs(x)))


def _sigmoid(x):
    return 0.5 * jnp.tanh(0.5 * x) + 0.5


def _silu(x):
    return x * _sigmoid(x)


def _norm_matmul_body(x_ref, g_ref, w_ref, o_ref, h_ref, *, blocked):
    @pl.when(pl.program_id(1) == 0)
    def _():
        x = x_ref[...].astype(F32)
        ms = jnp.mean(x * x, axis=-1, keepdims=True)
        h_ref[...] = (x * lax.rsqrt(ms + EPS) * g_ref[...]).astype(h_ref.dtype)

    acc = jnp.dot(h_ref[...], w_ref[...], preferred_element_type=F32)
    if blocked:
        bw = o_ref.shape[-1]
        for t in range(o_ref.shape[0]):
            o_ref[t] = acc[:, t * bw:(t + 1) * bw].astype(o_ref.dtype)
    else:
        o_ref[...] = acc.astype(o_ref.dtype)


def _norm_matmul(x, x_col_block, k_dim, gain, w, *, out_dtype, blocked_width=None, tm=1024, tn=512, name):
    m = x.shape[0]
    n = w.shape[1]
    tm = min(tm, m)
    assert m % tm == 0 and n % tn == 0 and w.shape[0] == k_dim
    grid = (m // tm, n // tn)
    if blocked_width is None:
        out_shape = jax.ShapeDtypeStruct((m, n), out_dtype)
        out_spec = pl.BlockSpec((tm, tn), lambda i, j: (i, j))
    else:
        nb = tn // blocked_width
        out_shape = jax.ShapeDtypeStruct((n // blocked_width, m, blocked_width), out_dtype)
        out_spec = pl.BlockSpec((nb, tm, blocked_width), lambda i, j: (j, i, 0))
    return pl.pallas_call(
        functools.partial(_norm_matmul_body, blocked=blocked_width is not None),
        grid=grid,
        in_specs=[
            pl.BlockSpec((tm, k_dim), lambda i, j: (i, x_col_block), pipeline_mode=pl.Buffered(1)),
            pl.BlockSpec((1, k_dim), lambda i, j: (0, 0)),
            pl.BlockSpec((k_dim, tn), lambda i, j: (0, j)),
        ],
        out_specs=out_spec,
        out_shape=out_shape,
        scratch_shapes=[pltpu.VMEM((tm, k_dim), BF16)],
        compiler_params=_cparams(("parallel", "arbitrary"), 48),
        name=name,
    )(x, gain.reshape(1, k_dim).astype(F32), w)


def _out_proj_body(*refs, n_y, k_tiles, final_norm):
    y_refs = refs[:n_y]
    w_ref, res_ref, g_ref, o_ref, acc_ref = refs[n_y:]
    k = pl.program_id(1)

    @pl.when(k == 0)
    def _():
        acc_ref[...] = jnp.zeros_like(acc_ref)

    lo = 0
    for y_ref, nk in zip(y_refs, k_tiles):
        @pl.when((k >= lo) & (k < lo + nk))
        def _(y_ref=y_ref):
            acc_ref[...] += jnp.dot(y_ref[...], w_ref[...], preferred_element_type=F32)
        lo += nk

    @pl.when(k == lo - 1)
    def _():
        xn = res_ref[...] + acc_ref[...]
        if final_norm:
            ms = jnp.mean(xn * xn, axis=-1, keepdims=True)
            xn = xn * lax.rsqrt(ms + EPS) * g_ref[...]
        o_ref[...] = xn


def _out_proj(ys, w, res, gain, *, final_norm, tm=512, tk=1024, name):
    m, d = res.shape
    tm = min(tm, m)
    k_tiles = [y.shape[1] // tk for y in ys]
    assert sum(k_tiles) * tk == w.shape[0]
    y_specs = []
    lo = 0
    for nk in k_tiles:
        y_specs.append(pl.BlockSpec((tm, tk), lambda i, k, lo=lo, nk=nk: (i, jnp.clip(k - lo, 0, nk - 1))))
        lo += nk
    return pl.pallas_call(
        functools.partial(_out_proj_body, n_y=len(ys), k_tiles=k_tiles, final_norm=final_norm),
        grid=(m // tm, lo),
        in_specs=y_specs + [
            pl.BlockSpec((tk, d), lambda i, k: (k, 0)),
            pl.BlockSpec((tm, d), lambda i, k: (i, 0)),
            pl.BlockSpec((1, d), lambda i, k: (0, 0)),
        ],
        out_specs=pl.BlockSpec((tm, d), lambda i, k: (i, 0)),
        out_shape=jax.ShapeDtypeStruct((m, d), F32),
        scratch_shapes=[pltpu.VMEM((tm, d), F32)],
        compiler_params=_cparams(("parallel", "arbitrary"), 48),
        name=name,
    )(*ys, w, res, gain.reshape(1, d).astype(F32))


def _mlstm_body(bi_ref, bf_ref, q_ref, k_ref, v_ref, op_ref, z_ref, g_ref, gain_ref, y_ref,
                c_scr, n_scr, m_scr):
    L = q_ref.shape[0]
    h = pl.program_id(1)

    @pl.when(pl.program_id(2) == 0)
    def _():
        c_scr[...] = jnp.zeros_like(c_scr)
        n_scr[...] = jnp.zeros_like(n_scr)
        m_scr[...] = jnp.zeros_like(m_scr)

    gates = g_ref[...]
    ic = gates[0:1, :] + bi_ref[h]
    fpre = gates[1:2, :] + bf_ref[h]
    fc = -_softplus(-fpre)

    ri = lax.broadcasted_iota(I32, (L, L), 0)
    ci = lax.broadcasted_iota(I32, (L, L), 1)
    causal = ci <= ri
    eye = ci == ri
    bcum_col = jnp.sum(jnp.where(causal, fc, 0.0), axis=1, keepdims=True)
    fc_col = jnp.sum(jnp.where(eye, fc, 0.0), axis=1, keepdims=True)
    bcum_row = jnp.sum(jnp.where(ri <= ci, fc_col, 0.0), axis=0, keepdims=True)

    m_st = m_scr[...]
    d = jnp.where(causal, bcum_col - bcum_row + ic, -jnp.inf)
    inter = bcum_col + m_st
    m_row = jnp.maximum(inter, jnp.max(d, axis=1, keepdims=True))
    w_inter = jnp.exp(inter - m_row)

    q = q_ref[...]
    k = k_ref[...] * (ML_DK ** -0.5)
    v = v_ref[...]
    qb = q.astype(BF16)
    vb = v.astype(BF16)
    p = jnp.exp(d - m_row) * _bdot_nt(qb, k)
    c_st = c_scr[...]
    n_st = n_scr[...]
    num = w_inter * _bdot(qb, c_st) + _bdot(p, vb)
    den = w_inter * jnp.sum(q * n_st, axis=1, keepdims=True) + jnp.sum(p, axis=1, keepdims=True)
    hout = num / jnp.maximum(jnp.abs(den), jnp.exp(-m_row))

    b_last = jnp.sum(fc, axis=1, keepdims=True)
    g_row = b_last - bcum_row + ic
    m_new = jnp.maximum(b_last + m_st, jnp.max(g_row, axis=1, keepdims=True))
    w_old = jnp.exp(b_last + m_st - m_new)
    w_in_col = jnp.sum(jnp.where(eye, jnp.exp(g_row - m_new), 0.0), axis=1, keepdims=True)
    kw = k * w_in_col
    c_scr[...] = w_old * c_st + _bdot_tn(kw, vb)
    n_scr[...] = w_old * n_st + jnp.sum(kw, axis=0, keepdims=True)
    m_scr[...] = m_new

    hn = hout * lax.rsqrt(jnp.mean(hout * hout, axis=-1, keepdims=True) + EPS) * gain_ref[...]
    y_ref[...] = (hn * _sigmoid(op_ref[...]) * _silu(z_ref[...])).astype(y_ref.dtype)


def _mlstm(proj, gates, b_i, b_f, gain, bsz, t):
    L = min(ML_CHUNK, t)
    nc = t // L
    row = lambda b, h, c: b * nc + c
    return pl.pallas_call(
        _mlstm_body,
        grid=(bsz, ML_HEADS, nc),
        in_specs=[
            pl.BlockSpec(memory_space=pltpu.SMEM),
            pl.BlockSpec(memory_space=pltpu.SMEM),
            pl.BlockSpec((L, ML_DK), lambda b, h, c: (row(b, h, c), h)),
            pl.BlockSpec((L, ML_DK), lambda b, h, c: (row(b, h, c), ML_HEADS + h)),
            pl.BlockSpec((L, ML_DV), lambda b, h, c: (row(b, h, c), ML_HEADS + h)),
            pl.BlockSpec((L, ML_DV), lambda b, h, c: (row(b, h, c), 2 * ML_HEADS + h)),
            pl.BlockSpec((L, ML_DV), lambda b, h, c: (row(b, h, c), 3 * ML_HEADS + h)),
            pl.BlockSpec((None, None, None, 2, L), lambda b, h, c: (b, h, c, 0, 0)),
            pl.BlockSpec((None, 1, ML_DV), lambda b, h, c: (h, 0, 0)),
        ],
        out_specs=pl.BlockSpec((L, ML_DV), lambda b, h, c: (row(b, h, c), h)),
        out_shape=jax.ShapeDtypeStruct((bsz * t, ML_HEADS * ML_DV), BF16),
        scratch_shapes=[pltpu.VMEM((ML_DK, ML_DV), F32), pltpu.VMEM((1, ML_DK), F32), pltpu.VMEM((1, 1), F32)],
        compiler_params=_cparams(("parallel", "parallel", "arbitrary"), 40),
        name="mlstm",
    )(b_i.astype(F32), b_f.astype(F32), proj, proj, proj, proj, proj, gates,
      gain.reshape(ML_HEADS, 1, ML_DV).astype(F32))


def _causal_conv(x, tail_ref, w):
    L = x.shape[0]
    xx = jnp.concatenate([tail_ref[...], x], axis=0)
    y = w[CONV_WIDTH - 1:CONV_WIDTH, :] * x
    for j in range(CONV_WIDTH - 1):
        y = y + w[j:j + 1, :] * pltpu.roll(xx, CONV_WIDTH - 1 - j, axis=0)[8:, :]
    tail_ref[...] = x[L - 8:, :]
    return y


def _head_expand(n_heads, width):
    r = lax.broadcasted_iota(I32, (n_heads, n_heads * width), 0)
    c = lax.broadcasted_iota(I32, (n_heads, n_heads * width), 1)
    return ((c >= r * width) & (c < (r + 1) * width)).astype(F32)


def _ssd_body(x_ref, bm_ref, cm_ref, z_ref, dtc_ref, dtr_ref, wx_ref, wb_ref, wc_ref, bx_ref, bb_ref, bc_ref,
              dtbc_ref, dtbr_ref, alc_ref, alr_ref, dsk_ref, gain_ref, y_ref, st_scr, tx_scr, tb_scr, tc_scr):
    L = x_ref.shape[0]
    nh, hd = SSM_HEADS_PER_GROUP, SSM_HEAD_DIM

    @pl.when(pl.program_id(2) == 0)
    def _():
        st_scr[...] = jnp.zeros_like(st_scr)
        tx_scr[...] = jnp.zeros_like(tx_scr)
        tb_scr[...] = jnp.zeros_like(tb_scr)
        tc_scr[...] = jnp.zeros_like(tc_scr)

    xc = _silu(_causal_conv(x_ref[...], tx_scr, wx_ref[...]) + bx_ref[...])
    bc = _silu(_causal_conv(bm_ref[...], tb_scr, wb_ref[...]) + bb_ref[...])
    cc = _silu(_causal_conv(cm_ref[...], tc_scr, wc_ref[...]) + bc_ref[...])

    ri = lax.broadcasted_iota(I32, (L, L), 0)
    ci = lax.broadcasted_iota(I32, (L, L), 1)
    causal = ci <= ri
    dt_col = _softplus(dtc_ref[...] + dtbc_ref[...])
    acum = _dot_sel_lhs(causal, dt_col * (-jnp.exp(alc_ref[...])))
    dt_row = _softplus(dtr_ref[...] + dtbr_ref[...])
    acum_row = _dot_sel_rhs(dt_row * (-jnp.exp(alr_ref[...])), ri <= ci)

    expand = _head_expand(nh, hd)
    acum_e = _dot_sel_rhs(acum, expand)
    xdt = xc * _dot_sel_rhs(dt_col, expand)
    xdt_b = xdt.astype(BF16)
    cb = _bdot_nt(cc, bc)

    lane = lax.broadcasted_iota(I32, (L, 2 * hd), 1)
    parts = []
    for hp in range(nh // 2):
        xpair = xdt_b[:, 2 * hd * hp:2 * hd * (hp + 1)]
        acc = None
        for e in range(2):
            hh = 2 * hp + e
            seg = acum[:, hh:hh + 1] - acum_row[hh:hh + 1, :]
            dec = jnp.exp(jnp.where(causal, seg, -jnp.inf))
            xsel = jnp.where((lane >= hd * e) & (lane < hd * (e + 1)), xpair, jnp.zeros_like(xpair))
            t = _bdot(cb * dec, xsel)
            acc = t if acc is None else acc + t
        parts.append(acc)
    y = jnp.concatenate(parts, axis=1)

    st = st_scr[...]
    y = y + _bdot(cc, st) * jnp.exp(acum_e)
    a_last = acum_e[L - 1:L, :]
    st_scr[...] = jnp.exp(a_last) * st + _bdot_tn(bc, xdt * jnp.exp(a_last - acum_e))

    y = y + xc * dsk_ref[...]
    y = y * _silu(z_ref[...])
    y = y * lax.rsqrt(jnp.mean(y * y, axis=-1, keepdims=True) + EPS) * gain_ref[...]
    y_ref[...] = y.astype(y_ref.dtype)


def _ssd(proj, dt_col, dt_row, conv_w, conv_b, dt_bias, a_log, d_skip, gain, bsz, t):
    L = min(SSD_CHUNK, t)
    nc = t // L
    G, nh, gw, ns = SSM_GROUPS, SSM_HEADS_PER_GROUP, SSM_GROUP_WIDTH, SSM_STATE
    row = lambda b, g, c: b * nc + c
    x_blk = 10240 // gw
    z_blk = 8192 // gw
    b_blk = 12288 // ns
    c_blk = 12800 // ns
    wb_blk = G * gw // ns
    conv_b2 = conv_b.reshape(1, -1).astype(F32)
    f = lambda a: a.astype(F32)
    return pl.pallas_call(
        _ssd_body,
        grid=(bsz, G, nc),
        in_specs=[
            pl.BlockSpec((L, gw), lambda b, g, c: (row(b, g, c), x_blk + g)),
            pl.BlockSpec((L, ns), lambda b, g, c: (row(b, g, c), b_blk + g)),
            pl.BlockSpec((L, ns), lambda b, g, c: (row(b, g, c), c_blk + g)),
            pl.BlockSpec((L, gw), lambda b, g, c: (row(b, g, c), z_blk + g)),
            pl.BlockSpec((None, None, L, nh), lambda b, g, c: (b, g, c, 0)),
            pl.BlockSpec((None, None, nh, L), lambda b, g, c: (b, g, 0, c)),
            pl.BlockSpec((CONV_WIDTH, gw), lambda b, g, c: (0, g)),
            pl.BlockSpec((CONV_WIDTH, ns), lambda b, g, c: (0, wb_blk + g)),
            pl.BlockSpec((CONV_WIDTH, ns), lambda b, g, c: (0, wb_blk + G + g)),
            pl.BlockSpec((1, gw), lambda b, g, c: (0, g)),
            pl.BlockSpec((1, ns), lambda b, g, c: (0, wb_blk + g)),
            pl.BlockSpec((1, ns), lambda b, g, c: (0, wb_blk + G + g)),
            pl.BlockSpec((None, 1, nh), lambda b, g, c: (g, 0, 0)),
            pl.BlockSpec((None, nh, 1), lambda b, g, c: (g, 0, 0)),
            pl.BlockSpec((None, 1, nh), lambda b, g, c: (g, 0, 0)),
            pl.BlockSpec((None, nh, 1), lambda b, g, c: (g, 0, 0)),
            pl.BlockSpec((None, 1, gw), lambda b, g, c: (g, 0, 0)),
            pl.BlockSpec((None, 1, gw), lambda b, g, c: (g, 0, 0)),
        ],
        out_specs=pl.BlockSpec((L, gw), lambda b, g, c: (row(b, g, c), g)),
        out_shape=jax.ShapeDtypeStruct((bsz * t, G * gw), BF16),
        scratch_shapes=[pltpu.VMEM((ns, gw), F32), pltpu.VMEM((8, gw), F32),
                        pltpu.VMEM((8, ns), F32), pltpu.VMEM((8, ns), F32)],
        compiler_params=_cparams(("parallel", "parallel", "arbitrary"), 40),
        name="ssd",
    )(proj, proj, proj, proj, dt_col, dt_row, f(conv_w), f(conv_w), f(conv_w), conv_b2, conv_b2, conv_b2,
      f(dt_bias).reshape(G, 1, nh), f(dt_bias).reshape(G, nh, 1), f(a_log).reshape(G, 1, nh),
      f(a_log).reshape(G, nh, 1), jnp.repeat(f(d_skip), SSM_HEAD_DIM).reshape(G, 1, gw), f(gain).reshape(G, 1, gw))


def _layer_ab(x2, bsz, t, norm_g, w_in, ml_b_i, ml_b_f, ml_norm, ssm_conv_w, ssm_conv_b, ssm_dt_bias,
              ssm_a_log, ssm_d, ssm_norm, w_out):
    m = bsz * t
    small = jnp.concatenate([w_in[:, 8192:8200], w_in[:, 13320:13352]], axis=1)
    w_main = jnp.concatenate(
        [w_in[:, :8192], w_in[:, 8200:13320], jnp.pad(small, ((0, 0), (0, 512 - small.shape[1])))], axis=1
    ).astype(BF16)
    proj = _norm_matmul(x2, 0, D_MODEL, norm_g, w_main, out_dtype=F32, tn=1536, name="proj_ab")
    gate_cols = proj[:, 13312:13352]
    ml_gates = gate_cols[:, :8].reshape(bsz, t // min(ML_CHUNK, t), min(ML_CHUNK, t), 2, ML_HEADS)
    ml_gates = jnp.transpose(ml_gates, (0, 4, 1, 3, 2))
    dt_raw = gate_cols[:, 8:40].reshape(bsz, t, SSM_GROUPS, SSM_HEADS_PER_GROUP)
    dt_col = jnp.transpose(dt_raw, (0, 2, 1, 3))
    dt_row = jnp.transpose(dt_raw, (0, 2, 3, 1))
    y_a = _mlstm(proj, ml_gates, ml_b_i, ml_b_f, ml_norm, bsz, t)
    y_b = _ssd(proj, dt_col, dt_row, ssm_conv_w, ssm_conv_b, ssm_dt_bias, ssm_a_log, ssm_d, ssm_norm, bsz, t)
    return _out_proj([y_a, y_b], w_out.astype(BF16), x2, jnp.ones((D_MODEL,), F32), final_norm=False,
                     name="out_ab")


def _indexer_body(iq_ref, iw_ref, ik_ref, mask_ref, hi_scr, lo_scr, pos_scr, *, top_k):
    n_kt = hi_scr.shape[0]
    KT, Q = IDX_KT, IDX_Q
    i = pl.program_id(1)
    start = i * Q
    n_tiles = (start + Q + KT - 1) // KT
    iqb = iq_ref[...].astype(BF16)
    iw = iw_ref[...] * (IDX_HEADS ** -0.5 * IDX_DIM ** -0.5)
    qpos = start + lax.broadcasted_iota(I32, (1, Q), 1)
    limit = (lax.shift_right_logical(qpos, int(math.log2(DSA_CHUNK))) + 1) * DSA_CHUNK
    kiota = lax.broadcasted_iota(I32, (KT, 1), 0)

    def score_tile(tt, carry):
        acc = jnp.zeros((KT, Q), F32)
        for p in range(IDX_HEADS // 2):
            xq = iqb[:, 2 * IDX_DIM * p:2 * IDX_DIM * (p + 1)]
            s0 = _bdot_nt(ik_ref[tt, 0], xq)
            s1 = _bdot_nt(ik_ref[tt, 1], xq)
            acc = acc + iw[2 * p:2 * p + 1, :] * jnp.maximum(s0, 0.0)
            acc = acc + iw[2 * p + 1:2 * p + 2, :] * jnp.maximum(s1, 0.0)
        bits = lax.bitcast_convert_type(acc + 0.0, I32)
        key = jnp.where(bits < 0, bits ^ jnp.int32(0x7FFFFFFF), bits)
        key = jnp.where(tt * KT + kiota < limit, key, jnp.int32(INT_MIN))
        hi_scr[tt] = lax.shift_right_arithmetic(key, 16).astype(I16)
        lo_scr[tt] = ((key & jnp.int32(0xFFFF)) - jnp.int32(HALF16)).astype(I16)
        return carry

    lax.fori_loop(0, n_tiles, score_tile, 0)

    def count16(pred):
        def body(tt, accs):
            accs = list(accs)
            w = jnp.where(pred(tt), jnp.int16(1), jnp.int16(0))
            for r in range(KT // 16):
                accs[r % 4] = accs[r % 4] + w[16 * r:16 * (r + 1), :]
            return tuple(accs)
        zero = jnp.zeros((16, Q), I16)
        a0, a1, a2, a3 = lax.fori_loop(0, n_tiles, body, (zero, zero, zero, zero))
        return jnp.sum(((a0 + a1) + (a2 + a3)).astype(F32), axis=0, keepdims=True)

    def rows16(v):
        return jnp.concatenate([jnp.broadcast_to(v, (16, Q)).astype(I16)] * (KT // 16), axis=0)

    def bisect16(ref, n_above):
        t16 = jnp.full((1, Q), -HALF16, I32)
        for bit in range(15, -1, -1):
            cand = jnp.zeros_like(t16) if bit == 15 else t16 + jnp.int32(1 << bit)
            ct = rows16(cand)
            t16 = jnp.where(n_above + count16(lambda tt, ct=ct: ref[tt] >= ct) >= top_k, cand, t16)
        return t16

    thr_hi = bisect16(hi_scr, 0.0)
    th = rows16(thr_hi)
    n_hi_gt = count16(lambda tt: hi_scr[tt] > th)

    def keep_lo(tt, carry):
        lo_scr[tt] = jnp.where(hi_scr[tt] == th, lo_scr[tt], jnp.int16(-HALF16))
        return carry

    lax.fori_loop(0, n_tiles, keep_lo, 0)
    thr_lo = bisect16(lo_scr, n_hi_gt)
    tl = rows16(thr_lo)

    def pos16(tt):
        return jnp.broadcast_to(tt * KT + kiota, (KT, Q)).astype(I16)

    n_gt = n_hi_gt + count16(lambda tt: lo_scr[tt] > tl)
    n_ge = n_hi_gt + count16(lambda tt: lo_scr[tt] >= tl)
    room = top_k - n_gt
    has_thr = (thr_hi > -HALF16) | (thr_lo > -HALF16)
    pos_scr[...] = jnp.full((1, Q), 2 * n_kt * KT, I32)

    @pl.when(jnp.max(jnp.where(has_thr, n_ge, 0.0)) > top_k)
    def _():
        pcut = jnp.zeros((1, Q), I32)
        for bit in range(int(math.log2(n_kt * KT)), -1, -1):
            cand = pcut + jnp.int32(1 << bit)
            c16 = rows16(cand)
            n_tie = count16(lambda tt, c16=c16: (hi_scr[tt] == th) & (lo_scr[tt] == tl) & (pos16(tt) < c16))
            pcut = jnp.where(n_tie <= room, cand, pcut)
        pos_scr[...] = pcut

    p16 = rows16(pos_scr[...])
    lim16 = rows16(limit)

    def write_tile(tt, carry):
        hi, lo, pos = hi_scr[tt], lo_scr[tt], pos16(tt)
        sel = (hi > th) | ((hi == th) & ((lo > tl) | ((lo == tl) & (pos < p16))))
        sel = sel & (pos < lim16)
        mask_ref[tt] = jnp.transpose(jnp.where(sel, jnp.int16(1), jnp.int16(0)).astype(F32)).astype(jnp.int8)
        return carry

    lax.fori_loop(0, n_tiles, write_tile, 0)

    def zero_tile(tt, carry):
        mask_ref[tt] = jnp.zeros((Q, KT), jnp.int8)
        return carry

    lax.fori_loop(n_tiles, n_kt, zero_tile, 0)


def _indexer(proj, iq_blk, iw_t, ik_pad, bsz, t):
    nq = t // IDX_Q
    n_kt = t // IDX_KT
    top_k = min(TOPK_MAX, t // 4)
    return pl.pallas_call(
        functools.partial(_indexer_body, top_k=top_k),
        grid=(bsz, nq),
        in_specs=[
            pl.BlockSpec((IDX_Q, IDX_HEADS * IDX_DIM), lambda b, i: (b * nq + i, iq_blk)),
            pl.BlockSpec((None, IDX_HEADS, IDX_Q), lambda b, i: (b * nq + i, 0, 0)),
            pl.BlockSpec((None, n_kt, 2, IDX_KT, 2 * IDX_DIM), lambda b, i: (b, 0, 0, 0, 0)),
        ],
        out_specs=pl.BlockSpec((None, n_kt, IDX_Q, IDX_KT), lambda b, i: (b * nq + i, 0, 0, 0)),
        out_shape=jax.ShapeDtypeStruct((bsz * nq, n_kt, IDX_Q, IDX_KT), jnp.int8),
        scratch_shapes=[pltpu.VMEM((n_kt, IDX_KT, IDX_Q), I16), pltpu.VMEM((n_kt, IDX_KT, IDX_Q), I16),
                        pltpu.VMEM((1, IDX_Q), I32)],
        compiler_params=_cparams(("parallel", "arbitrary"), 48),
        name="dsa_indexer",
    )(proj, iw_t, ik_pad)


NEG_BIG = -3.0e38
LOG2_E = math.log2(math.e)


def _attn_body(q_ref, k_ref, v_ref, mask_ref, z_ref, tb_ref, o_ref, acc_scr, m_scr):
    T = ATT_T
    D = ATT_HEAD_DIM
    nsub = T // ATT_SUB
    i = pl.program_id(1)
    j = pl.program_id(2)

    @pl.when(j == 0)
    def _():
        acc_scr[...] = jnp.zeros_like(acc_scr)
        m_scr[...] = jnp.full_like(m_scr, NEG_BIG)

    def tile_step(near):
        sel = jnp.concatenate([mask_ref[a, 0].astype(F32) for a in range(mask_ref.shape[0])], axis=0)
        mask_bias = (sel - 1.0) * (-NEG_BIG)
        ones = jnp.ones((T, D), BF16)

        def pair_body(p, carry):
            q2, k2, v2 = q_ref[p], k_ref[p], v_ref[p]
            m_old = [m_scr[2 * p + e] for e in range(2)]
            a_old = [acc_scr[2 * p + e] for e in range(2)]
            new = []
            logits = [_bdot_nt(q2[:, D * e:D * (e + 1)], k2[:, D * e:D * (e + 1)]) for e in range(2)]
            for e in range(2):
                s = logits[e] + mask_bias
                if near is not None:
                    tb = tb_ref[2 * p + e]
                    zero = jnp.zeros((ATT_SUB, ATT_SUB), F32)
                    rows = []
                    for a in range(nsub):
                        cols = []
                        for bb in range(nsub):
                            rel_blocks = (bb - a) if near == "diag" else (bb - a - nsub)
                            cols.append(tb[0] if rel_blocks == 0 else tb[1] if rel_blocks == -1 else zero)
                        rows.append(jnp.concatenate(cols, axis=1))
                    s = s + jnp.concatenate(rows, axis=0)
                m_new = jnp.maximum(m_old[e], jnp.max(s, axis=1, keepdims=True))
                alpha = jnp.exp2(m_old[e] - m_new)
                pr = jnp.exp2(s - jnp.concatenate([m_new] * (T // LANES), axis=1))
                v_aug = jnp.concatenate([v2[:, D * e:D * (e + 1)], ones], axis=1)
                a_new = jnp.concatenate([alpha] * (2 * D // LANES), axis=1) * a_old[e] + _bdot(pr, v_aug)
                new.append((m_new, a_new))
            for e in range(2):
                m_scr[2 * p + e] = new[e][0]
                acc_scr[2 * p + e] = new[e][1]
            return carry

        lax.fori_loop(0, ATT_HEADS // 2, pair_body, 0, unroll=8)

    @pl.when(j < i - 1)
    def _():
        tile_step(None)

    @pl.when(j == i - 1)
    def _():
        tile_step("prev")

    @pl.when(j == i)
    def _():
        tile_step("diag")
        for p in range(ATT_HEADS // 2):
            for e in range(2):
                hh = 2 * p + e
                a = acc_scr[hh]
                o = a[:, :D] / a[:, D:]
                o = o * _silu(z_ref[p, :, D * e:D * (e + 1)].astype(F32))
                o_ref[:, D * hh:D * (hh + 1)] = o.astype(o_ref.dtype)


def _attention(qz, kv, mask, tb, bsz, t):
    T = ATT_T
    nq = t // T
    npair = ATT_HEADS // 2
    pw = 2 * ATT_HEAD_DIM
    nrow = T // IDX_Q
    ncol = T // IDX_KT
    return pl.pallas_call(
        _attn_body,
        grid=(bsz, nq, nq),
        in_specs=[
            pl.BlockSpec((npair, T, pw), lambda b, i, j: (0, b * nq + i, 0)),
            pl.BlockSpec((npair, T, pw), lambda b, i, j: (0, b * nq + jnp.minimum(j, i), 0)),
            pl.BlockSpec((npair, T, pw), lambda b, i, j: (1, b * nq + jnp.minimum(j, i), 0)),
            pl.BlockSpec((nrow, ncol, IDX_Q, IDX_KT), lambda b, i, j: (b * nq + i, jnp.minimum(j, i), 0, 0)),
            pl.BlockSpec((npair, T, pw), lambda b, i, j: (1, b * nq + i, 0)),
            pl.BlockSpec((ATT_HEADS, 2, ATT_SUB, ATT_SUB), lambda b, i, j: (0, 0, 0, 0)),
        ],
        out_specs=pl.BlockSpec((T, ATT_HEADS * ATT_HEAD_DIM), lambda b, i, j: (b * nq + i, 0)),
        out_shape=jax.ShapeDtypeStruct((bsz * t, ATT_HEADS * ATT_HEAD_DIM), BF16),
        scratch_shapes=[pltpu.VMEM((ATT_HEADS, T, 2 * ATT_HEAD_DIM), F32), pltpu.VMEM((ATT_HEADS, T, LANES), F32)],
        compiler_params=_cparams(("parallel", "parallel", "arbitrary"), 56),
        name="dsa_attention",
    )(qz, kv, kv, mask, qz, tb)


def _t5_bucket(rel):
    nb = REL_BUCKETS // 2
    max_exact = nb // 2
    ret = jnp.where(rel > 0, nb, 0)
    n = jnp.abs(rel)
    nf = jnp.maximum(n, max_exact).astype(F32)
    large = max_exact + (jnp.log(nf / max_exact) / math.log(REL_MAX_DIST / max_exact)
                         * (nb - max_exact)).astype(jnp.int32)
    large = jnp.minimum(large, nb - 1)
    return ret + jnp.where(n < max_exact, n, large)


def _rel_bias_blocks(rel_bias):
    r = jnp.arange(ATT_SUB)[:, None]
    c = jnp.arange(ATT_SUB)[None, :]
    rel = jnp.stack([c - r, c - r - ATT_SUB])
    assert ATT_SUB >= REL_MAX_DIST and ATT_SUB % DSA_CHUNK == 0
    rb = rel_bias.astype(F32)
    far = rb[_t5_bucket(jnp.int32(-2 * ATT_SUB))]
    onehot = (_t5_bucket(rel)[..., None] == jnp.arange(REL_BUCKETS)).astype(F32)
    table = jnp.einsum("vrcb,bh->hvrc", onehot, rb - far, precision=HI)
    return table * LOG2_E


GDN_PREP_ROWS = 256
GDN_SOLVE_LANES = 512
GDN_SCAN_CHUNKS = 2


def _conv_with_prev(x, prev, w):
    xx = jnp.concatenate([prev, x], axis=0)
    y = w[CONV_WIDTH - 1:CONV_WIDTH, :] * x
    for j in range(CONV_WIDTH - 1):
        y = y + w[j:j + 1, :] * pltpu.roll(xx, CONV_WIDTH - 1 - j, axis=0)[8:, :]
    return y


def _gdn_prep_body(v_ref, q_ref, k_ref, pv_ref, pq_ref, pk_ref, bc_ref, ac_ref, ar_ref, wq_ref, wk_ref, wv_ref,
                   dtbc_ref, dtbr_ref, alc_ref, alr_ref,
                   qn_ref, rhs_ref, lu_ref, a_ref, egc_ref, egl_ref):
    L = GDN_CHUNK
    D = GDN_HEAD_DIM
    nv = GDN_V_HEADS // GDN_GROUPS
    rep = GDN_V_HEADS // GDN_QK_HEADS
    seq_start = pl.program_id(2) == 0

    def conv(x_ref, p_ref, w_ref):
        prev = jnp.where(seq_start, 0.0, p_ref[...])
        return _silu(_conv_with_prev(x_ref[...], prev, w_ref[...]))

    qc = conv(q_ref, pq_ref, wq_ref)
    kc = conv(k_ref, pk_ref, wk_ref)
    vc = conv(v_ref, pv_ref, wv_ref)
    qn, kn = [], []
    for kh in range(nv // rep):
        qh = qc[:, D * kh:D * (kh + 1)]
        kk_ = kc[:, D * kh:D * (kh + 1)]
        qh = qh * lax.rsqrt(jnp.sum(qh * qh, axis=-1, keepdims=True) + EPS) * (D ** -0.5)
        kn.append(kk_ * lax.rsqrt(jnp.sum(kk_ * kk_, axis=-1, keepdims=True) + EPS))
        qn.append(qh.astype(BF16))
        qn_ref[:, D * kh:D * (kh + 1)] = qn[kh]

    beta = _sigmoid(bc_ref[...])
    g_col = -jnp.exp(alc_ref[...]) * _softplus(ac_ref[...] + dtbc_ref[...])
    ri = lax.broadcasted_iota(I32, (L, L), 0)
    ci = lax.broadcasted_iota(I32, (L, L), 1)
    causal = ci <= ri
    strict = ci < ri
    for cc in range(a_ref.shape[0]):
        rows = slice(L * cc, L * (cc + 1))
        gc_col = _dot_sel_lhs(causal, g_col[rows, :])
        g_row = -jnp.exp(alr_ref[...]) * _softplus(ar_ref[cc] + dtbr_ref[...])
        gc_row = _dot_sel_rhs(g_row, ri <= ci)
        eg_col = jnp.exp(gc_col)
        egc_ref[rows, :] = eg_col
        for kh in range(nv // rep):
            kf = kn[kh][rows, :]
            kb = kf.astype(BF16)
            kk = _bdot_nt(kb, kb)
            qk = _bdot_nt(qn[kh][rows, :], kb)
            for e in range(rep):
                vh = rep * kh + e
                gcol = gc_col[:, vh:vh + 1]
                bcol = beta[rows, vh:vh + 1]
                dec = jnp.exp(jnp.where(causal, gcol - gc_row[vh:vh + 1, :], -jnp.inf))
                a_ref[cc, vh] = jnp.where(strict, bcol * kk * dec, 0.0).astype(a_ref.dtype)
                g_last = gcol[L - 1:L, :]
                kw_t = jnp.transpose(kf * jnp.exp(g_last - gcol))
                lu_ref[cc, vh] = jnp.concatenate([qk * dec, kw_t], axis=0).astype(lu_ref.dtype)
                egl_ref[cc, vh:vh + 1, :] = jnp.broadcast_to(jnp.exp(g_last), (1, LANES))
                rhs_ref[rows, 2 * D * vh:2 * D * (vh + 1)] = jnp.concatenate(
                    [vc[rows, D * vh:D * (vh + 1)] * bcol, kf * (bcol * eg_col[:, vh:vh + 1])], axis=1
                ).astype(rhs_ref.dtype)


def _gdn_solve_body(a_in_ref, t_ref, t_scr, a_ref):
    L, _, PB = t_scr.shape
    UN = 8
    sub = lax.broadcasted_iota(I32, (UN, PB), 0)
    t_scr[...] = jnp.zeros_like(t_scr)
    a_ref[...] = jnp.transpose(a_in_ref[...].astype(F32)).reshape(L, L, PB)

    for ib in range(L // UN):
        def row_body(r, carry, ib=ib):
            i = ib * UN + r
            acc = [jnp.zeros((UN, PB), F32) for _ in range(ib + 1)]
            for jb in range(ib + 1):
                a_blk = a_ref[i, UN * jb:UN * (jb + 1), :]
                for jj in range(UN):
                    a_ij = a_blk[jj:jj + 1, :]
                    for cg in range(jb + 1):
                        acc[cg] = acc[cg] + a_ij * t_scr[UN * jb + jj, UN * cg:UN * (cg + 1), :]
            for cg in range(ib + 1):
                unit = jnp.where(sub == r, 1.0, 0.0) if cg == ib else 0.0
                t_scr[i, UN * cg:UN * (cg + 1), :] = unit - acc[cg]
            return carry

        lax.fori_loop(0, UN, row_body, 0)
    t_ref[...] = jnp.transpose(t_scr[...].reshape(L * L, PB)).astype(t_ref.dtype)


def _gdn_scan_body(qn_ref, rhs_ref, lu_ref, t_ref, z_ref, egc_ref, egl_ref, gain_ref, y_ref, s_scr):
    L = GDN_CHUNK
    D = GDN_HEAD_DIM
    nv = GDN_V_HEADS // GDN_GROUPS
    rep = GDN_V_HEADS // GDN_QK_HEADS
    ncs = t_ref.shape[0]

    @pl.when(pl.program_id(2) == 0)
    def _():
        s_scr[...] = jnp.zeros_like(s_scr)

    heads = range(nv)
    sol = [[_bdot(t_ref[cc, vh], rhs_ref[L * cc:L * (cc + 1), 2 * D * vh:2 * D * (vh + 1)]) for vh in heads]
           for cc in range(ncs)]
    s_st = [s_scr[vh] for vh in heads]
    for cc in range(ncs):
        rows = slice(L * cc, L * (cc + 1))
        eg_col = egc_ref[rows, :]
        r_s = [_bdot(jnp.concatenate([sol[cc][vh][:, D:].astype(BF16),
                                      qn_ref[rows, D * (vh // rep):D * (vh // rep + 1)]], axis=0), s_st[vh])
               for vh in heads]
        u = [sol[cc][vh][:, :D] - r_s[vh][:L] for vh in heads]
        r_u = [_bdot(lu_ref[cc, vh], u[vh]) for vh in heads]
        s_st = [egl_ref[cc, vh:vh + 1, :] * s_st[vh] + r_u[vh][L:] for vh in heads]
        for vh in heads:
            o = r_s[vh][L:] * eg_col[:, vh:vh + 1] + r_u[vh][:L]
            o = o * lax.rsqrt(jnp.mean(o * o, axis=-1, keepdims=True) + EPS) * gain_ref[...]
            y_ref[rows, D * vh:D * (vh + 1)] = (o * _silu(z_ref[rows, D * vh:D * (vh + 1)])).astype(y_ref.dtype)
    for vh in heads:
        s_scr[vh] = s_st[vh]


def _gdn(proj, b_col, a_col, a_row, conv_w, dt_bias, a_log, gain, bsz, t):
    L = GDN_CHUNK
    nc = t // L
    G = GDN_GROUPS
    nv = GDN_V_HEADS // G
    qw = GDN_QK_HEADS // G * GDN_HEAD_DIM
    vw = nv * GDN_HEAD_DIM
    m = bsz * t
    z_blk, q_blk, k_blk = 4096 // vw, 8192 // qw, 10240 // qw
    f = lambda a: a.astype(F32)
    cw = f(conv_w)
    pc_spec = pl.BlockSpec((None, 1, nv), lambda b, g, c: (g, 0, 0))
    pr_spec = pl.BlockSpec((None, nv, 1), lambda b, g, c: (g, 0, 0))
    wq_spec = pl.BlockSpec((CONV_WIDTH, qw), lambda b, g, c: (0, g))
    wk_spec = pl.BlockSpec((CONV_WIDTH, qw), lambda b, g, c: (0, G + g))
    wv_spec = pl.BlockSpec((CONV_WIDTH, vw), lambda b, g, c: (0, (2 * G * qw) // vw + g))

    LB = min(GDN_PREP_ROWS, t)
    ncb = LB // L
    nb = t // LB
    rowb = lambda b, g, c: b * nb + c
    prev8 = lambda b, g, c: jnp.maximum((b * t + c * LB) // 8 - 1, 0)
    D = GDN_HEAD_DIM
    qn, rhs, lu, a_mat, egc, egl = pl.pallas_call(
        _gdn_prep_body,
        grid=(bsz, G, nb),
        in_specs=[
            pl.BlockSpec((LB, vw), lambda b, g, c: (rowb(b, g, c), g)),
            pl.BlockSpec((LB, qw), lambda b, g, c: (rowb(b, g, c), q_blk + g)),
            pl.BlockSpec((LB, qw), lambda b, g, c: (rowb(b, g, c), k_blk + g)),
            pl.BlockSpec((8, vw), lambda b, g, c: (prev8(b, g, c), g)),
            pl.BlockSpec((8, qw), lambda b, g, c: (prev8(b, g, c), q_blk + g)),
            pl.BlockSpec((8, qw), lambda b, g, c: (prev8(b, g, c), k_blk + g)),
            pl.BlockSpec((None, None, LB, nv), lambda b, g, c: (b, g, c, 0)),
            pl.BlockSpec((None, None, LB, nv), lambda b, g, c: (b, g, c, 0)),
            pl.BlockSpec((None, None, ncb, nv, L), lambda b, g, c: (b, g, c, 0, 0)),
            wq_spec, wk_spec, wv_spec, pc_spec, pr_spec, pc_spec, pr_spec,
        ],
        out_specs=[
            pl.BlockSpec((LB, qw), lambda b, g, c: (rowb(b, g, c), g)),
            pl.BlockSpec((LB, 2 * vw), lambda b, g, c: (rowb(b, g, c), g)),
            pl.BlockSpec((None, None, ncb, nv, L + D, L), lambda b, g, c: (b, g, c, 0, 0, 0)),
            pl.BlockSpec((None, None, ncb, nv, L, L), lambda b, g, c: (b, g, c, 0, 0, 0)),
            pl.BlockSpec((None, None, LB, nv), lambda b, g, c: (b, g, c, 0)),
            pl.BlockSpec((None, None, ncb, nv, LANES), lambda b, g, c: (b, g, c, 0, 0)),
        ],
        out_shape=[
            jax.ShapeDtypeStruct((m, G * qw), BF16),
            jax.ShapeDtypeStruct((m, 2 * G * vw), BF16),
            jax.ShapeDtypeStruct((bsz, G, nc, nv, L + D, L), BF16),
            jax.ShapeDtypeStruct((bsz, G, nc, nv, L, L), BF16),
            jax.ShapeDtypeStruct((bsz, G, t, nv), F32),
            jax.ShapeDtypeStruct((bsz, G, nc, nv, LANES), F32),
        ],
        compiler_params=_cparams(("parallel", "parallel", "parallel"), 48),
        name="gdn_prep",
    )(proj, proj, proj, proj, proj, proj, b_col, a_col, a_row, cw, cw, cw,
      f(dt_bias).reshape(G, 1, nv), f(dt_bias).reshape(G, nv, 1), f(a_log).reshape(G, 1, nv),
      f(a_log).reshape(G, nv, 1))

    n_prob = bsz * G * nc * nv
    pb = min(GDN_SOLVE_LANES, n_prob)
    a_flat = a_mat.reshape(n_prob, L * L)
    t_t = pl.pallas_call(
        _gdn_solve_body,
        grid=(n_prob // pb,),
        in_specs=[pl.BlockSpec((pb, L * L), lambda p: (p, 0))],
        out_specs=pl.BlockSpec((pb, L * L), lambda p: (p, 0)),
        out_shape=jax.ShapeDtypeStruct((n_prob, L * L), BF16),
        scratch_shapes=[pltpu.VMEM((L, L, pb), F32), pltpu.VMEM((L, L, pb), F32)],
        compiler_params=_cparams(("parallel",), 40),
        name="gdn_solve",
    )(a_flat)
    t_mat = t_t.reshape(bsz, G, nc, nv, L, L)

    ncs = min(GDN_SCAN_CHUNKS, nc)
    ns = nc // ncs
    row = lambda b, g, c: b * ns + c
    return pl.pallas_call(
        _gdn_scan_body,
        grid=(bsz, G, ns),
        in_specs=[
            pl.BlockSpec((ncs * L, qw), lambda b, g, c: (row(b, g, c), g)),
            pl.BlockSpec((ncs * L, 2 * vw), lambda b, g, c: (row(b, g, c), g)),
            pl.BlockSpec((None, None, ncs, nv, L + D, L), lambda b, g, c: (b, g, c, 0, 0, 0)),
            pl.BlockSpec((None, None, ncs, nv, L, L), lambda b, g, c: (b, g, c, 0, 0, 0)),
            pl.BlockSpec((ncs * L, vw), lambda b, g, c: (row(b, g, c), z_blk + g)),
            pl.BlockSpec((None, None, ncs * L, nv), lambda b, g, c: (b, g, c, 0)),
            pl.BlockSpec((None, None, ncs, nv, LANES), lambda b, g, c: (b, g, c, 0, 0)),
            pl.BlockSpec((1, GDN_HEAD_DIM), lambda b, g, c: (0, 0)),
        ],
        out_specs=pl.BlockSpec((ncs * L, vw), lambda b, g, c: (row(b, g, c), g)),
        out_shape=jax.ShapeDtypeStruct((m, GDN_V_HEADS * GDN_HEAD_DIM), BF16),
        scratch_shapes=[pltpu.VMEM((nv, GDN_HEAD_DIM, GDN_HEAD_DIM), F32)],
        compiler_params=_cparams(("parallel", "parallel", "arbitrary"), 40),
        name="gdn_scan",
    )(qn, rhs, lu, t_mat, proj, egc, egl, f(gain).reshape(1, GDN_HEAD_DIM))


def _layer_cd(x2, bsz, t, norm_g, w_in, kv_norm, w_uk, w_uv, gdn_conv_w, gdn_dt_bias, gdn_a_log, gdn_norm,
              w_out, rel_bias, final_g, final_norm):
    m = bsz * t
    small = jnp.concatenate([w_in[:, 5632:5712], w_in[:, 18000:18064]], axis=1)
    w_main = jnp.concatenate(
        [w_in[:, 9808:13904], w_in[:, 13904:18000], w_in[:, 5712:9808], w_in[:, 4608:5632], w_in[:, 2048:2560],
         jnp.pad(small, ((0, 0), (0, 512 - small.shape[1])))], axis=1).astype(BF16)
    proj = _norm_matmul(x2, 0, D_MODEL, norm_g, w_main, out_dtype=F32, tn=1024, name="proj_cd")
    w_attn = jnp.concatenate([w_in[:, :2048] * (ATT_HEAD_DIM ** -0.5 * LOG2_E), w_in[:, 2560:4608]],
                             axis=1).astype(BF16)
    qz = _norm_matmul(x2, 0, D_MODEL, norm_g, w_attn, out_dtype=BF16, blocked_width=2 * ATT_HEAD_DIM,
                      tn=1024, name="proj_cd_attn")
    w_kv = jnp.concatenate([w_uk.reshape(KV_RANK, -1), w_uv.reshape(KV_RANK, -1)], axis=1).astype(BF16)
    kv = _norm_matmul(proj, 13312 // KV_RANK, KV_RANK, kv_norm, w_kv, out_dtype=BF16,
                      blocked_width=2 * ATT_HEAD_DIM, tn=1024, name="dsa_kv")

    gate_cols = proj[:, 13824:13824 + 144]
    n_kt = t // IDX_KT
    ik = gate_cols[:, :IDX_DIM].reshape(bsz, n_kt, 1, IDX_KT, IDX_DIM).astype(BF16)
    zk = jnp.zeros_like(ik)
    ik_pad = jnp.concatenate([jnp.concatenate([ik, zk], axis=-1), jnp.concatenate([zk, ik], axis=-1)], axis=2)
    iw_t = jnp.transpose(gate_cols[:, IDX_DIM:IDX_DIM + IDX_HEADS].reshape(m // IDX_Q, IDX_Q, IDX_HEADS), (0, 2, 1))
    mask = _indexer(proj, 12288 // (IDX_HEADS * IDX_DIM), iw_t, ik_pad, bsz, t)
    y_c = _attention(qz, kv, mask, _rel_bias_blocks(rel_bias), bsz, t)

    nv = GDN_V_HEADS // GDN_GROUPS
    b_pre = gate_cols[:, 80:112].reshape(bsz, t, GDN_GROUPS, nv)
    a_pre = gate_cols[:, 112:144].reshape(bsz, t, GDN_GROUPS, nv)
    col = lambda a: jnp.transpose(a, (0, 2, 1, 3))
    rowf = lambda a: jnp.transpose(a.reshape(bsz, t // GDN_CHUNK, GDN_CHUNK, GDN_GROUPS, nv), (0, 3, 1, 4, 2))
    y_d = _gdn(proj, col(b_pre), col(a_pre), rowf(a_pre), gdn_conv_w, gdn_dt_bias, gdn_a_log, gdn_norm, bsz, t)
    return _out_proj([y_c, y_d], w_out.astype(BF16), x2, final_g, final_norm=final_norm, name="out_cd")


def kernel(x, rel_bias, ab_norm, ab_w_in, ab_ml_b_i, ab_ml_b_f, ab_ml_norm, ab_ssm_conv_w, ab_ssm_conv_b, ab_ssm_dt_bias, ab_ssm_a_log, ab_ssm_d, ab_ssm_norm, ab_w_out, cd_norm, cd_w_in, cd_kv_norm, cd_w_uk, cd_w_uv, cd_gdn_conv_w, cd_gdn_dt_bias, cd_gdn_a_log, cd_gdn_norm, cd_w_out, final_norm):
    bsz, t, d = x.shape
    x2 = x.reshape(bsz * t, d)
    x2 = _layer_ab(x2, bsz, t, ab_norm[0], ab_w_in[0], ab_ml_b_i[0], ab_ml_b_f[0], ab_ml_norm[0],
                   ab_ssm_conv_w[0], ab_ssm_conv_b[0], ab_ssm_dt_bias[0], ab_ssm_a_log[0], ab_ssm_d[0],
                   ab_ssm_norm[0], ab_w_out[0])
    x2 = _layer_cd(x2, bsz, t, cd_norm[0], cd_w_in[0], cd_kv_norm[0], cd_w_uk[0], cd_w_uv[0], cd_gdn_conv_w[0],
                   cd_gdn_dt_bias[0], cd_gdn_a_log[0], cd_gdn_norm[0], cd_w_out[0], rel_bias, final_norm, True)
    return x2.reshape(bsz, t, d)
```
